```python
import math
import jax
import jax.numpy as jnp
from jax import lax
import numpy as np

D_MODEL = 1024
BATCH = 4
SEQ = 4096
DEPTH = 4
DEC_BATCH = 128
DEC_SEQ = 4
PAST_LEN = 8192
PAGE_SIZE = 128

BRANCH_WIDTH = D_MODEL
N_BRANCH = 4
SSD_D_INNER = BRANCH_WIDTH
SSD_HEAD_DIM = 64
SSD_HEADS = SSD_D_INNER // SSD_HEAD_DIM
SSD_GROUPS = 4
SSD_STATE = 64
SSD_CONV = 4
SSD_CHUNK = 128
SSD_CONV_DIM = SSD_D_INNER + 2 * SSD_GROUPS * SSD_STATE
SC_WIDTH = BRANCH_WIDTH
SC_CONV = 3
ATTN_HEAD_DIM = 64
ATTN_HEADS = BRANCH_WIDTH // ATTN_HEAD_DIM
ATTN_KV_HEADS = 4
ATTN_GROUP = ATTN_HEADS // ATTN_KV_HEADS
WINDOW = 128
GM_WIDTH = BRANCH_WIDTH
GM_GROUPS = 8
GM_CHUNK = 128
D_FF = 2816
FFN_CONV = 3
EPS = 1e-6

IN_SIZES = (SSD_D_INNER, SSD_CONV_DIM, SSD_HEADS, 3 * SC_WIDTH, ATTN_HEADS * ATTN_HEAD_DIM,
            ATTN_KV_HEADS * ATTN_HEAD_DIM, ATTN_KV_HEADS * ATTN_HEAD_DIM, 2 * GM_WIDTH, N_BRANCH * D_MODEL)
IN_OFFSETS = tuple(sum(IN_SIZES[:i + 1]) for i in range(len(IN_SIZES) - 1))
D_IN = sum(IN_SIZES)

kernel_name = 'hybrid_gated_ssd_conv_swa_gmlp_decoder_step'


def rmsnorm(x, g):
    xf = x.astype(jnp.float32)
    xf = xf * lax.rsqrt(jnp.mean(xf * xf, axis=-1, keepdims=True) + EPS)
    return (xf * g.astype(jnp.float32)).astype(x.dtype)


def layernorm(x, g, b):
    xf = x.astype(jnp.float32)
    xc = xf - jnp.mean(xf, axis=-1, keepdims=True)
    xf = xc * lax.rsqrt(jnp.mean(xc * xc, axis=-1, keepdims=True) + EPS)
    return (xf * g.astype(jnp.float32) + b.astype(jnp.float32)).astype(x.dtype)


def causal_dwconv(x, w, prev, b=None):
    K = w.shape[0]
    L = x.shape[1]
    xp = jnp.concatenate([prev.astype(x.dtype), x], axis=1)
    y = xp[:, 0:L] * w[0]
    for j in range(1, K):
        y = y + xp[:, j:j + L] * w[j]
    if b is not None:
        y = y + b
    return y, xp[:, L:]


def ssd_scan(xdt, dtA, Bm, Cm, s0):
    f32 = jnp.float32
    b, l, h, p = xdt.shape
    g, n = Bm.shape[2], Bm.shape[3]
    r = h // g
    T = min(SSD_CHUNK, l)
    c = l // T
    X = xdt.astype(f32).reshape(b, c, T, g, r, p)
    Bc = Bm.astype(f32).reshape(b, c, T, g, n)
    Cc = Cm.astype(f32).reshape(b, c, T, g, n)
    A_cs = jnp.cumsum(dtA.astype(f32).reshape(b, c, T, g, r), axis=2)
    diff = A_cs[:, :, :, None] - A_cs[:, :, None, :]
    causal = jnp.tril(jnp.ones((T, T), dtype=bool))[None, None, :, :, None, None]
    Lmat = jnp.exp(jnp.where(causal, diff, -jnp.inf))
    CB = jnp.einsum('bctgn,bcsgn->bctsg', Cc, Bc)
    y_diag = jnp.einsum('bctsgr,bcsgrp->bctgrp', CB[..., None] * Lmat, X)
    decay_s = jnp.exp(A_cs[:, :, -1:] - A_cs)
    states = jnp.einsum('bcsgn,bcsgr,bcsgrp->bcgrpn', Bc, decay_s, X)
    chunk_decay = jnp.exp(A_cs[:, :, -1])

    def step(S, inp):
        st, dec = inp
        return S * dec[..., None, None] + st, S

    S0 = s0.astype(f32).reshape(b, g, r, p, n)
    S_fin, S_in = lax.scan(step, S0, (jnp.moveaxis(states, 1, 0), jnp.moveaxis(chunk_decay, 1, 0)))
    S_in = jnp.moveaxis(S_in, 0, 1)
    y_off = jnp.einsum('bctgn,bcgrpn,bctgr->bctgrp', Cc, S_in, jnp.exp(A_cs))
    y = (y_diag + y_off).reshape(b, l, h, p)
    return y, S_fin.reshape(b, h, p, n)


def ssd_mixer(z, xbc, dtr, conv_prev, ssm_prev, conv_w, conv_b, dt_bias, a_log, d_skip, norm_g):
    f32 = jnp.float32
    b, l = xbc.shape[:2]
    xbc, conv_new = causal_dwconv(xbc, conv_w, conv_prev, conv_b)
    xbc = jax.nn.silu(xbc)
    xs, Bm, Cm = jnp.split(xbc, [SSD_D_INNER, SSD_D_INNER + SSD_GROUPS * SSD_STATE], axis=-1)
    xh = xs.reshape(b, l, SSD_HEADS, SSD_HEAD_DIM).astype(f32)
    Bm = Bm.reshape(b, l, SSD_GROUPS, SSD_STATE)
    Cm = Cm.reshape(b, l, SSD_GROUPS, SSD_STATE)
    dt = jax.nn.softplus(dtr.astype(f32) + dt_bias.astype(f32))
    A = -jnp.exp(a_log.astype(f32))
    y, s_new = ssd_scan(xh * dt[..., None], dt * A, Bm, Cm, ssm_prev)
    y = y + xh * d_skip.astype(f32)[:, None]
    y = y.reshape(b, l, SSD_D_INNER) * jax.nn.silu(z.astype(f32))
    yg = y.reshape(b, l, SSD_GROUPS, SSD_D_INNER // SSD_GROUPS)
    yg = yg * lax.rsqrt(jnp.mean(yg * yg, axis=-1, keepdims=True) + EPS)
    y = yg.reshape(b, l, SSD_D_INNER) * norm_g.astype(f32)
    return y.astype(z.dtype), conv_new, s_new.astype(ssm_prev.dtype)


def shortconv_mixer(bcx, conv_prev, conv_w):
    Bg, Cg, xs = jnp.split(bcx, 3, axis=-1)
    y, conv_new = causal_dwconv(Cg * xs, conv_w, conv_prev)
    return Bg * y, conv_new


def alibi_slopes():
    return 2.0 ** (-8.0 * jnp.arange(1, ATTN_HEADS + 1, dtype=jnp.float32) / ATTN_HEADS)


def attn_core(q, k, v, dist, valid, sinks):
    f32 = jnp.float32
    slopes = alibi_slopes().reshape(ATTN_KV_HEADS, ATTN_GROUP)[:, :, None, None]
    s = jnp.einsum('bnqhgd,bnkhd->bnhgqk', q.astype(f32), k.astype(f32)) * (ATTN_HEAD_DIM ** -0.5)
    s = s - slopes * dist[:, None, None]
    s = jnp.where(valid[:, None, None], s, -jnp.inf)
    sink = sinks.astype(f32).reshape(ATTN_KV_HEADS, ATTN_GROUP)[:, :, None, None]
    m = jnp.maximum(jnp.max(s, axis=-1, keepdims=True), sink)
    pr = jnp.exp(s - m)
    pr = pr / (jnp.sum(pr, axis=-1, keepdims=True) + jnp.exp(sink - m))
    return jnp.einsum('bnhgqk,bnkhd->bnqhgd', pr, v.astype(f32))


def swa_prompt(q, k, v, sinks):
    b, l = q.shape[:2]
    nb = l // WINDOW
    qb = q.reshape(b, nb, WINDOW, ATTN_KV_HEADS, ATTN_GROUP, ATTN_HEAD_DIM)
    kb = k.reshape(b, nb, WINDOW, ATTN_KV_HEADS, ATTN_HEAD_DIM)
    vb = v.reshape(b, nb, WINDOW, ATTN_KV_HEADS, ATTN_HEAD_DIM)
    pad = ((0, 0), (1, 0), (0, 0), (0, 0), (0, 0))
    kk = jnp.concatenate([jnp.pad(kb, pad)[:, :-1], kb], axis=2)
    vv = jnp.concatenate([jnp.pad(vb, pad)[:, :-1], vb], axis=2)
    qpos = jnp.arange(WINDOW)
    kpos = jnp.arange(2 * WINDOW) - WINDOW
    dist = qpos[:, None] - kpos[None, :]
    band = (dist >= 0) & (dist <= WINDOW)
    blk = jnp.arange(nb)
    valid = band[None] & ((blk[:, None, None] > 0) | (kpos[None, None, :] >= 0))
    o = attn_core(qb, kk, vv, dist[None].astype(jnp.float32), valid, sinks)
    nbuf = min(WINDOW, l)
    return o.reshape(b, l, ATTN_HEADS * ATTN_HEAD_DIM).astype(q.dtype), k[:, l - nbuf:], v[:, l - nbuf:]


def swa_sample(q, k, v, k_buf, v_buf, sinks):
    b, l = q.shape[:2]
    nbuf = k_buf.shape[1]
    kk = jnp.concatenate([k_buf.astype(k.dtype), k], axis=1)
    vv = jnp.concatenate([v_buf.astype(v.dtype), v], axis=1)
    dist = (nbuf + jnp.arange(l))[:, None] - jnp.arange(nbuf + l)[None, :]
    valid = (dist >= 0) & (dist <= WINDOW)
    o = attn_core(q[:, None], kk[:, None], vv[:, None], dist[None].astype(jnp.float32), valid[None], sinks)
    return o.reshape(b, l, ATTN_HEADS * ATTN_HEAD_DIM).astype(q.dtype), kk[:, l:], vv[:, l:]


def gmlp_mixer(uv, ln_g, ln_b, w_s, b_s):
    uv = jax.nn.gelu(uv)
    u, v = jnp.split(uv, 2, axis=-1)
    v = layernorm(v, ln_g, ln_b)
    b, l = v.shape[:2]
    T = min(GM_CHUNK, l)
    c = l // T
    vg = v.reshape(b, c, T, GM_GROUPS, GM_WIDTH // GM_GROUPS)
    W = jnp.tril(w_s[:, :T, :T])
    mixed = jnp.einsum('gts,bcsgf->bctgf', W, vg) + b_s[:, :T].T[None, None, :, :, None]
    return u * mixed.reshape(b, l, GM_WIDTH).astype(u.dtype), v


def conv_ffn(h, conv_prev, w_up, conv_w, conv_b, w_down):
    up, conv_new = causal_dwconv(h @ w_up, conv_w, conv_prev, conv_b)
    a, g = jnp.split(up, 2, axis=-1)
    return (jax.nn.silu(a) * g) @ w_down, conv_new


def run_group(x, c, is_prompt, ssm_st, ssd_conv_st, sc_conv_st, k_st, v_st, ffn_conv_st,
              w_ada, b_ada, g_norm_mix, w_in, ssd_conv_w, ssd_conv_b, ssd_dt_bias, ssd_a_log, ssd_d,
              ssd_norm_g, sc_conv_w, attn_sinks, gm_ln_g, gm_ln_b, gm_w_s, gm_b_s, w_branch, w_o,
              g_norm_ffn, ffn_w_up, ffn_conv_w, ffn_conv_b, ffn_w_down, g_final):
    b, l = x.shape[:2]
    dtp = x.dtype
    new = [[] for _ in range(6 if is_prompt else 7)]
    for i in range(DEPTH):
        mod = (jax.nn.silu(c) @ w_ada[i] + b_ada[i])[:, None, :]
        sh1, sc1, ga1, sh2, sc2, ga2 = jnp.split(mod, 6, axis=-1)
        h = rmsnorm(x, g_norm_mix[i]) * (1 + sc1) + sh1
        z, xbc, dtr, bcx, q, k, v, uv, gates = jnp.split(h @ w_in[i], IN_OFFSETS, axis=-1)
        if is_prompt:
            ssm0 = jnp.zeros((b, SSD_HEADS, SSD_HEAD_DIM, SSD_STATE), dtp)
            ssdc0 = jnp.zeros((b, SSD_CONV - 1, SSD_CONV_DIM), dtp)
            scc0 = jnp.zeros((b, SC_CONV - 1, SC_WIDTH), dtp)
            ffc0 = jnp.zeros((b, FFN_CONV - 1, 2 * D_FF), dtp)
        else:
            ssm0, ssdc0, scc0, ffc0 = ssm_st[i], ssd_conv_st[i], sc_conv_st[i], ffn_conv_st[i]
        ya, ssdc1, ssm1 = ssd_mixer(z, xbc, dtr, ssdc0, ssm0, ssd_conv_w[i], ssd_conv_b[i],
                                    ssd_dt_bias[i], ssd_a_log[i], ssd_d[i], ssd_norm_g[i])
        yb, scc1 = shortconv_mixer(bcx, scc0, sc_conv_w[i])
        q = q.reshape(b, l, ATTN_KV_HEADS, ATTN_GROUP, ATTN_HEAD_DIM)
        k = k.reshape(b, l, ATTN_KV_HEADS, ATTN_HEAD_DIM)
        v = v.reshape(b, l, ATTN_KV_HEADS, ATTN_HEAD_DIM)
        if is_prompt:
            yc, k1, v1 = swa_prompt(q, k, v, attn_sinks[i])
        else:
            yc, k1, v1 = swa_sample(q, k, v, k_st[i], v_st[i], attn_sinks[i])
        yd, v_rows = gmlp_mixer(uv, gm_ln_g[i], gm_ln_b[i], gm_w_s[i], gm_b_s[i])
        branches = jnp.stack([ya, yb.astype(dtp), yc, yd], axis=2)
        proj = jnp.einsum('blif,ifd->blid', branches, w_branch[i])
        gate = jax.nn.sigmoid(gates.astype(jnp.float32)).reshape(b, l, N_BRANCH, D_MODEL)
        merged = jnp.sum(gate * proj, axis=2).astype(dtp)
        x = x + ga1 * (merged @ w_o[i])
        h2 = rmsnorm(x, g_norm_ffn[i]) * (1 + sc2) + sh2
        yf, ffc1 = conv_ffn(h2, ffc0, ffn_w_up[i], ffn_conv_w[i], ffn_conv_b[i], ffn_w_down[i])
        x = x + ga2 * yf
        vals = (ssm1, ssdc1, scc1, k1, v1, ffc1) if is_prompt else (ssm1, ssdc1, scc1, k1, v1, ffc1, v_rows)
        for lst, val in zip(new, vals):
            lst.append(val)
    return rmsnorm(x, g_final), tuple(jnp.stack(lst, axis=0) for lst in new)


def setup_inputs(seed: int = 0) -> dict:
    key = jax.random.key(seed)
    keys = iter(jax.random.split(key, 64))

    def nrm(shape, scale):
        return scale * jax.random.normal(next(keys), shape, jnp.float32)

    n_buf = min(WINDOW, PAST_LEN)
    dt0 = jnp.exp(jax.random.uniform(next(keys), (DEPTH, SSD_HEADS), jnp.float32,
                                     math.log(1e-3), math.log(1e-1)))
    a0 = jax.random.uniform(next(keys), (DEPTH, SSD_HEADS), jnp.float32, 1.0, 16.0)
    return {
        'x_prompt': nrm((BATCH, SEQ, D_MODEL), 1.0),
        'x_sample': nrm((DEC_BATCH, DEC_SEQ, D_MODEL), 1.0),
        'c_prompt': nrm((BATCH, D_MODEL), 1.0),
        'c_sample': nrm((DEC_BATCH, D_MODEL), 1.0),
        'state_ssm': nrm((DEPTH, DEC_BATCH, SSD_HEADS, SSD_HEAD_DIM, SSD_STATE), 0.1),
        'state_ssd_conv': nrm((DEPTH, DEC_BATCH, SSD_CONV - 1, SSD_CONV_DIM), 1.0),
        'state_sc_conv': nrm((DEPTH, DEC_BATCH, SC_CONV - 1, SC_WIDTH), 1.0),
        'cache_k': nrm((DEPTH, DEC_BATCH, n_buf, ATTN_KV_HEADS, ATTN_HEAD_DIM), 1.0),
        'cache_v': nrm((DEPTH, DEC_BATCH, n_buf, ATTN_KV_HEADS, ATTN_HEAD_DIM), 1.0),
        'state_ffn_conv': nrm((DEPTH, DEC_BATCH, FFN_CONV - 1, 2 * D_FF), 1.0),
        'w_ada': nrm((DEPTH, D_MODEL, 6 * D_MODEL), 0.5 * D_MODEL ** -0.5),
        'b_ada': nrm((DEPTH, 6 * D_MODEL), 0.02),
        'g_norm_mix': 1.0 + nrm((DEPTH, D_MODEL), 0.02),
        'w_in': nrm((DEPTH, D_MODEL, D_IN), D_MODEL ** -0.5),
        'ssd_conv_w': nrm((DEPTH, SSD_CONV, SSD_CONV_DIM), SSD_CONV ** -0.5),
        'ssd_conv_b': nrm((DEPTH, SSD_CONV_DIM), 0.02),
        'ssd_dt_bias': dt0 + jnp.log(-jnp.expm1(-dt0)),
        'ssd_a_log': jnp.log(a0),
        'ssd_d': 1.0 + nrm((DEPTH, SSD_HEADS), 0.1),
        'ssd_norm_g': 1.0 + nrm((DEPTH, SSD_D_INNER), 0.02),
        'sc_conv_w': nrm((DEPTH, SC_CONV, SC_WIDTH), SC_CONV ** -0.5),
        'attn_sinks': nrm((DEPTH, ATTN_HEADS), 0.5),
        'gm_ln_g': 1.0 + nrm((DEPTH, GM_WIDTH), 0.02),
        'gm_ln_b': nrm((DEPTH, GM_WIDTH), 0.02),
        'gm_w_s': nrm((DEPTH, GM_GROUPS, GM_CHUNK, GM_CHUNK), GM_CHUNK ** -0.5),
        'gm_b_s': 1.0 + nrm((DEPTH, GM_GROUPS, GM_CHUNK), 0.1),
        'w_branch': nrm((DEPTH, N_BRANCH, BRANCH_WIDTH, D_MODEL), BRANCH_WIDTH ** -0.5),
        'w_o': nrm((DEPTH, D_MODEL, D_MODEL), D_MODEL ** -0.5),
        'g_norm_ffn': 1.0 + nrm((DEPTH, D_MODEL), 0.02),
        'ffn_w_up': nrm((DEPTH, D_MODEL, 2 * D_FF), D_MODEL ** -0.5),
        'ffn_conv_w': nrm((DEPTH, FFN_CONV, 2 * D_FF), FFN_CONV ** -0.5),
        'ffn_conv_b': nrm((DEPTH, 2 * D_FF), 0.02),
        'ffn_w_down': nrm((DEPTH, D_FF, D_MODEL), D_FF ** -0.5),
        'g_final': 1.0 + nrm((D_MODEL,), 0.02),
    }


def reference(x_prompt, x_sample, c_prompt, c_sample, state_ssm, state_ssd_conv, state_sc_conv,
              cache_k, cache_v, state_ffn_conv, w_ada, b_ada, g_norm_mix, w_in, ssd_conv_w, ssd_conv_b,
              ssd_dt_bias, ssd_a_log, ssd_d, ssd_norm_g, sc_conv_w, attn_sinks, gm_ln_g, gm_ln_b,
              gm_w_s, gm_b_s, w_branch, w_o, g_norm_ffn, ffn_w_up, ffn_conv_w, ffn_conv_b, ffn_w_down,
              g_final):
    weights = (w_ada, b_ada, g_norm_mix, w_in, ssd_conv_w, ssd_conv_b, ssd_dt_bias, ssd_a_log, ssd_d,
               ssd_norm_g, sc_conv_w, attn_sinks, gm_ln_g, gm_ln_b, gm_w_s, gm_b_s, w_branch, w_o,
               g_norm_ffn, ffn_w_up, ffn_conv_w, ffn_conv_b, ffn_w_down, g_final)
    y_prompt, (p_ssm, p_ssd_conv, p_sc_conv, p_k, p_v, p_ffn_conv) = run_group(
        x_prompt, c_prompt, True, None, None, None, None, None, None, *weights)
    y_sample, (s_ssm, s_ssd_conv, s_sc_conv, s_k, s_v, s_ffn_conv, s_gm_v) = run_group(
        x_sample, c_sample, False, state_ssm, state_ssd_conv, state_sc_conv, cache_k, cache_v,
        state_ffn_conv, *weights)
    return (y_prompt, y_sample, p_ssm, p_ssd_conv, p_sc_conv, p_k, p_v, p_ffn_conv,
            s_ssm, s_ssd_conv, s_sc_conv, s_k, s_v, s_ffn_conv, s_gm_v)
```

```python
import functools
import math

import jax
import jax.numpy as jnp
from jax import lax
from jax.experimental import pallas as pl
from jax.experimental.pallas import tpu as pltpu

F32 = jnp.float32
BF16 = jnp.bfloat16

D_MODEL = 1024
DEPTH = 4
N_BRANCH = 4
SSD_HEADS = 16
SSD_HEAD_DIM = 64
SSD_GROUPS = 4
SSD_STATE = 64
SSD_CONV = 4
SSD_D_INNER = 1024
SSD_BC = SSD_GROUPS * SSD_STATE
SSD_CONV_DIM = SSD_D_INNER + 2 * SSD_BC
SC_CONV = 3
ATTN_HEADS = 16
ATTN_KV_HEADS = 4
ATTN_GROUP = 4
ATTN_HEAD_DIM = 64
ATTN_KV = ATTN_KV_HEADS * ATTN_HEAD_DIM
WINDOW = 128
GM_GROUPS = 8
GM_GROUP_WIDTH = 128
D_FF = 2816
FFN_CONV = 3
EPS = 1e-6

CHUNK = 128
SAMPLE_ROWS = 8
CARRY_ROWS = 8
LANES = 128
MXU_WIDTH = 256
VMEM_LIMIT = 56 * 1024 * 1024

(Y_Z, Y_XS, Y_BG, Y_CG, Y_XSC, Y_Q, Y_U, Y_VGM, Y_G0) = range(9)
Y_WIDE = 12
Y_SSDB, Y_SSDC, Y_K, Y_V = (Y_WIDE * 4 + i for i in range(4))
Y_COLS = Y_WIDE * D_MODEL + 4 * SSD_BC

_IN_OFF_XBC = 1024
_IN_OFF_DT = 2560
_IN_OFF_BCX = 2576
_IN_OFF_Q = 5648
_IN_OFF_K = 6672
_IN_OFF_V = 6928
_IN_OFF_UV = 7184
_IN_OFF_GATES = 9232
_W_IN_PIECES = (
    (0, 1024),
    (_IN_OFF_XBC, _IN_OFF_XBC + 1024),
    (_IN_OFF_BCX, _IN_OFF_BCX + 1024),
    (_IN_OFF_BCX + 1024, _IN_OFF_BCX + 2048),
    (_IN_OFF_BCX + 2048, _IN_OFF_BCX + 3072),
    (_IN_OFF_Q, _IN_OFF_Q + 1024),
    (_IN_OFF_UV, _IN_OFF_UV + 1024),
    (_IN_OFF_UV + 1024, _IN_OFF_UV + 2048),
    (_IN_OFF_GATES, _IN_OFF_GATES + 4096),
    (_IN_OFF_XBC + 1024, _IN_OFF_XBC + 1280),
    (_IN_OFF_XBC + 1280, _IN_OFF_XBC + 1536),
    (_IN_OFF_K, _IN_OFF_K + 256),
    (_IN_OFF_V, _IN_OFF_V + 256),
)


def _cparams(*sem):
    return pltpu.CompilerParams(dimension_semantics=sem, vmem_limit_bytes=VMEM_LIMIT)


def _const_spec(shape):
    zeros = (0,) * len(shape)
    return pl.BlockSpec(shape, lambda *_: zeros)


def _silu(x):
    return x * jax.nn.sigmoid(x)


def _mod_rows(ref):
    return ref[0] if len(ref.shape) == 3 else ref[...]


def _modnorm(x, g, sc, sh):
    r = lax.rsqrt(jnp.mean(x * x, axis=-1, keepdims=True) + EPS)
    return (x * r * g) * (1.0 + sc) + sh


def _split3(x):
    hi = x.astype(BF16)
    r1 = x - hi.astype(F32)
    mid = r1.astype(BF16)
    lo = (r1 - mid.astype(F32)).astype(BF16)
    return hi, mid, lo


def _dot(a, b):
    return jnp.dot(a, b, preferred_element_type=F32)


def _dot_nt(a, b):
    return lax.dot_general(a, b, (((1,), (1,)), ((), ())), preferred_element_type=F32)


def _dot_tn(a, b):
    return lax.dot_general(a, b, (((0,), (0,)), ((), ())), preferred_element_type=F32)


def _dot_exact_lhs(a_bf16, x):
    hi, mid, lo = _split3(x)
    return _dot(a_bf16, hi) + _dot(a_bf16, mid) + _dot(a_bf16, lo)


def _dot_exact_rhs(x, e_bf16):
    hi, mid, lo = _split3(x)
    return _dot(hi, e_bf16) + _dot(mid, e_bf16) + _dot(lo, e_bf16)


def _chunk_rows(ref, rows):
    x = ref[...].astype(F32)
    if x.shape[0] == rows:
        return x
    pad = jnp.zeros((rows - x.shape[0], x.shape[1]), F32)
    return jnp.concatenate([x, pad], axis=0)


def _ada_kernel(c_ref, w_ref, b_ref, o_ref):
    s = _silu(c_ref[...]).astype(BF16)
    o_ref[0] = _dot(s, w_ref[0].astype(BF16)) + b_ref[0]


def _ada(c_all, w_ada, b_ada):
    rows = c_all.shape[0]
    n_tiles = w_ada.shape[-1] // D_MODEL
    return pl.pallas_call(
        _ada_kernel,
        grid=(DEPTH, n_tiles),
        in_specs=[
            _const_spec((rows, D_MODEL)),
            pl.BlockSpec((1, D_MODEL, D_MODEL), lambda l, n: (l, 0, n)),
            pl.BlockSpec((1, 1, D_MODEL), lambda l, n: (l, 0, n)),
        ],
        out_specs=pl.BlockSpec((1, rows, D_MODEL), lambda l, n: (l, 0, n)),
        out_shape=jax.ShapeDtypeStruct((DEPTH, rows, 6 * D_MODEL), F32),
        compiler_params=_cparams("arbitrary", "arbitrary"),
        name="ada",
    )(c_all, w_ada, b_ada.reshape(DEPTH, 1, 6 * D_MODEL))


def _inproj_kernel(x_ref, g_ref, sc_ref, sh_ref, w_ref, wdt_ref, y_ref, dtr_ref, h_scr):
    @pl.when(pl.program_id(1) == 0)
    def _():
        h = _modnorm(x_ref[...], g_ref[...], _mod_rows(sc_ref), _mod_rows(sh_ref)).astype(BF16)
        h_scr[...] = h
        dtr_ref[...] = _dot(h, wdt_ref[...])

    y_ref[...] = _dot(h_scr[...], w_ref[...]).astype(y_ref.dtype)


def _mod_spec(group, tm, piece):
    if group.per_token_mod:
        return pl.BlockSpec((tm, D_MODEL), lambda i, *_: (i, piece))
    tiles_per_seq = group.seq_rows // tm
    return pl.BlockSpec((1, 1, D_MODEL), lambda i, *_: (i // tiles_per_seq, 0, piece))


def _inproj(group, x, mod, g, w_main, w_dt):
    tm = group.tm_inproj
    rows = x.shape[0]
    tn = D_MODEL
    return pl.pallas_call(
        _inproj_kernel,
        grid=(rows // tm, Y_COLS // tn),
        in_specs=[
            pl.BlockSpec((tm, D_MODEL), lambda i, n: (i, 0)),
            _const_spec((1, D_MODEL)),
            _mod_spec(group, tm, 1),
            _mod_spec(group, tm, 0),
            pl.BlockSpec((D_MODEL, tn), lambda i, n: (0, n)),
            _const_spec((D_MODEL, LANES)),
        ],
        out_specs=[
            pl.BlockSpec((tm, tn), lambda i, n: (i, n)),
            pl.BlockSpec((tm, LANES), lambda i, n: (i, 0)),
        ],
        out_shape=[
            jax.ShapeDtypeStruct((rows, Y_COLS), BF16),
            jax.ShapeDtypeStruct((rows, LANES), F32),
        ],
        scratch_shapes=[pltpu.VMEM((tm, D_MODEL), BF16)],
        compiler_params=_cparams("arbitrary", "arbitrary"),
        name="inproj",
    )(x, g, mod, mod, w_main, w_dt)


def _causal_conv(xp_ref, w_ref, width, rows):
    first = CARRY_ROWS - (width - 1)
    acc = w_ref[0:1, :] * xp_ref[pl.ds(first, rows), :]
    for j in range(1, width):
        acc = acc + w_ref[j:j + 1, :] * xp_ref[pl.ds(first + j, rows), :]
    return acc


def _chunk_spec(group, width, piece):
    nc = group.chunks_per_seq
    return pl.BlockSpec((group.chunk_rows, width), lambda b, c: (b * nc + c, piece))


def _ssd_kernel(l_valid, n_chunks,
                z_ref, xs_ref, bm_ref, cm_ref, dtr_ref, s0_ref, cp_ref,
                cw_ref, cb_ref, dtb_ref, alog_ref, dsk_ref, ng_ref, e_ref,
                y_ref, sfin_ref, cst_ref,
                s_scr, xp_scr):
    T = CHUNK
    c = pl.program_id(1)

    @pl.when(c == 0)
    def _():
        s_scr[...] = s0_ref[0]
        xp_scr[0:CARRY_ROWS, :] = cp_ref[0]

    xp_scr[CARRY_ROWS:CARRY_ROWS + T, 0:SSD_D_INNER] = _chunk_rows(xs_ref, T)
    xp_scr[CARRY_ROWS:CARRY_ROWS + T, SSD_D_INNER:SSD_D_INNER + SSD_BC] = _chunk_rows(bm_ref, T)
    xp_scr[CARRY_ROWS:CARRY_ROWS + T, SSD_D_INNER + SSD_BC:SSD_CONV_DIM] = _chunk_rows(cm_ref, T)
    xbc = _silu(_causal_conv(xp_scr, cw_ref, SSD_CONV, T) + cb_ref[...])
    xs = xbc[:, 0:SSD_D_INNER]
    bmat = xbc[:, SSD_D_INNER:SSD_D_INNER + SSD_BC].astype(BF16)
    cmat = xbc[:, SSD_D_INNER + SSD_BC:SSD_CONV_DIM].astype(BF16)

    pre = _chunk_rows(dtr_ref, T) + dtb_ref[...]
    dt = jnp.maximum(pre, 0.0) + jnp.log1p(jnp.exp(-jnp.abs(pre)))
    row = lax.broadcasted_iota(jnp.int32, (T, T), 0)
    col = lax.broadcasted_iota(jnp.int32, (T, T), 1)
    if l_valid < T:
        rvalid = lax.broadcasted_iota(jnp.int32, (T, LANES), 0) < l_valid
        dt = jnp.where(rvalid, dt, 0.0)
    dta = dt * (-jnp.exp(alog_ref[...]))
    causal = row >= col
    tri = jnp.where(causal, 1.0, 0.0).astype(BF16)
    cs = _dot_exact_lhs(tri, dta)
    cs_t = cs.T
    total = cs[T - 1:T, :]
    e = e_ref[...]
    dt_x = _dot_exact_rhs(dt, e)
    ecs_x = _dot_exact_rhs(jnp.exp(cs), e)
    dec_x = _dot_exact_rhs(jnp.exp(total - cs), e)
    chunk_decay = jnp.exp(total)

    x_dt = xs * dt_x
    x_b = x_dt.astype(BF16)
    x_dec = (x_dt * dec_x).astype(BF16)

    lane = lax.broadcasted_iota(jnp.int32, (T, 2 * SSD_HEAD_DIM), 1)
    low_half = lane < SSD_HEAD_DIM

    y_parts = []
    for g in range(SSD_GROUPS):
        gs = slice(g * SSD_STATE, (g + 1) * SSD_STATE)
        cb = _dot_nt(cmat[:, gs], bmat[:, gs])
        heads_per_group = SSD_HEADS // SSD_GROUPS
        s_g = s_scr[g * heads_per_group:(g + 1) * heads_per_group].reshape(
            heads_per_group * SSD_HEAD_DIM, SSD_STATE)
        y_off = _dot_nt(cmat[:, gs], s_g.astype(BF16))
        width = heads_per_group * SSD_HEAD_DIM
        ds_g = _dot_tn(x_dec[:, g * width:(g + 1) * width], bmat[:, gs])
        pair_out = []
        for pr in range(heads_per_group // 2):
            h0 = g * heads_per_group + 2 * pr
            xp = x_b[:, h0 * SSD_HEAD_DIM:(h0 + 2) * SSD_HEAD_DIM]
            acc = None
            for k in range(2):
                h = h0 + k
                diff = cs[:, h:h + 1] - cs_t[h:h + 1, :]
                lmat = jnp.where(causal, jnp.exp(jnp.where(causal, diff, 0.0)), 0.0)
                m = (cb * lmat).astype(BF16)
                keep = low_half if k == 0 else jnp.logical_not(low_half)
                part = _dot(m, jnp.where(keep, xp, jnp.zeros_like(xp)))
                acc = part if acc is None else acc + part
            pair_out.append(acc)
        y_parts.append(jnp.concatenate(pair_out, axis=1) + y_off * ecs_x[:, g * width:(g + 1) * width])
        for r in range(heads_per_group):
            h = g * heads_per_group + r
            s_scr[h] = (s_scr[h] * chunk_decay[:, h:h + 1]
                        + ds_g[r * SSD_HEAD_DIM:(r + 1) * SSD_HEAD_DIM, :])

    y = jnp.concatenate(y_parts, axis=1) + xs * dsk_ref[...]
    y = y * _silu(_chunk_rows(z_ref, T))
    gw = SSD_D_INNER // SSD_GROUPS
    normed = []
    for g in range(SSD_GROUPS):
        yg = y[:, g * gw:(g + 1) * gw]
        normed.append(yg * lax.rsqrt(jnp.mean(yg * yg, axis=-1, keepdims=True) + EPS))
    y = jnp.concatenate(normed, axis=1) * ng_ref[...]
    y_ref[...] = y[0:y_ref.shape[0], :].astype(y_ref.dtype)

    @pl.when(c == n_chunks - 1)
    def _():
        sfin_ref[0] = s_scr[...]
        cst_ref[0] = xp_scr[pl.ds(CARRY_ROWS + l_valid - (SSD_CONV - 1), SSD_CONV - 1), :]

    if n_chunks > 1:
        xp_scr[0:CARRY_ROWS, :] = xp_scr[T:T + CARRY_ROWS, :]


def _ssd(group, y, dtr, s0, conv_prev8, p):
    nb, nc = group.n_seq, group.chunks_per_seq
    rows = y.shape[0]
    kern = functools.partial(_ssd_kernel, group.last_chunk_valid, nc)
    return pl.pallas_call(
        kern,
        grid=(nb, nc),
        in_specs=[
            _chunk_spec(group, D_MODEL, Y_Z),
            _chunk_spec(group, D_MODEL, Y_XS),
            _chunk_spec(group, SSD_BC, Y_SSDB),
            _chunk_spec(group, SSD_BC, Y_SSDC),
            _chunk_spec(group, LANES, 0),
            pl.BlockSpec((1, SSD_HEADS, SSD_HEAD_DIM, SSD_STATE), lambda b, c: (b, 0, 0, 0)),
            pl.BlockSpec((1, CARRY_ROWS, SSD_CONV_DIM), lambda b, c: (b, 0, 0)),
            _const_spec((SSD_CONV, SSD_CONV_DIM)),
            _const_spec((1, SSD_CONV_DIM)),
            _const_spec((1, LANES)),
            _const_spec((1, LANES)),
            _const_spec((1, SSD_D_INNER)),
            _const_spec((1, SSD_D_INNER)),
            _const_spec((LANES, SSD_D_INNER)),
        ],
        out_specs=[
            _chunk_spec(group, D_MODEL, 0),
            pl.BlockSpec((1, SSD_HEADS, SSD_HEAD_DIM, SSD_STATE), lambda b, c: (b, 0, 0, 0)),
            pl.BlockSpec((1, SSD_CONV - 1, SSD_CONV_DIM), lambda b, c: (b, 0, 0)),
        ],
        out_shape=[
            jax.ShapeDtypeStruct((rows, D_MODEL), BF16),
            jax.ShapeDtypeStruct((nb, SSD_HEADS, SSD_HEAD_DIM, SSD_STATE), F32),
            jax.ShapeDtypeStruct((nb, SSD_CONV - 1, SSD_CONV_DIM), F32),
        ],
        scratch_shapes=[
            pltpu.VMEM((SSD_HEADS, SSD_HEAD_DIM, SSD_STATE), F32),
            pltpu.VMEM((CARRY_ROWS + CHUNK, SSD_CONV_DIM), F32),
        ],
        compiler_params=_cparams("arbitrary", "arbitrary"),
        name="ssd",
    )(y, y, y, y, dtr, s0, conv_prev8,
      p["ssd_conv_w"], p["ssd_conv_b"], p["ssd_dt_bias"], p["ssd_a_log"], p["ssd_d"],
      p["ssd_norm_g"], p["ssd_expand"])


def _sconv_kernel(l_valid, n_chunks, bg_ref, cg_ref, xs_ref, cp_ref, cw_ref, y_ref, cst_ref, xp_scr):
    T = CHUNK
    c = pl.program_id(1)

    @pl.when(c == 0)
    def _():
        xp_scr[0:CARRY_ROWS, :] = cp_ref[0]

    xp_scr[CARRY_ROWS:CARRY_ROWS + T, :] = _chunk_rows(cg_ref, T) * _chunk_rows(xs_ref, T)
    y = _chunk_rows(bg_ref, T) * _causal_conv(xp_scr, cw_ref, SC_CONV, T)
    y_ref[...] = y[0:y_ref.shape[0], :].astype(y_ref.dtype)

    @pl.when(c == n_chunks - 1)
    def _():
        cst_ref[0] = xp_scr[pl.ds(CARRY_ROWS + l_valid - (SC_CONV - 1), SC_CONV - 1), :]

    if n_chunks > 1:
        xp_scr[0:CARRY_ROWS, :] = xp_scr[T:T + CARRY_ROWS, :]


def _sconv(group, y, conv_prev8, p):
    nb, nc = group.n_seq, group.chunks_per_seq
    rows = y.shape[0]
    kern = functools.partial(_sconv_kernel, group.last_chunk_valid, nc)
    return pl.pallas_call(
        kern,
        grid=(nb, nc),
        in_specs=[
            _chunk_spec(group, D_MODEL, Y_BG),
            _chunk_spec(group, D_MODEL, Y_CG),
            _chunk_spec(group, D_MODEL, Y_XSC),
            pl.BlockSpec((1, CARRY_ROWS, D_MODEL), lambda b, c: (b, 0, 0)),
            _const_spec((SC_CONV, D_MODEL)),
        ],
        out_specs=[
            _chunk_spec(group, D_MODEL, 0),
            pl.BlockSpec((1, SC_CONV - 1, D_MODEL), lambda b, c: (b, 0, 0)),
        ],
        out_shape=[
            jax.ShapeDtypeStruct((rows, D_MODEL), BF16),
            jax.ShapeDtypeStruct((nb, SC_CONV - 1, D_MODEL), F32),
        ],
        scratch_shapes=[pltpu.VMEM((CARRY_ROWS + CHUNK, D_MODEL), F32)],
        compiler_params=_cparams("arbitrary", "arbitrary"),
        name="sconv",
    )(y, y, y, conv_prev8, p["sc_conv_w"])


def _attn_kernel(prev_always_valid, q_ref, k_ref, v_ref, kp_ref, vp_ref, sink_ref, y_ref):
    T = CHUNK
    c = pl.program_id(1)
    q = _chunk_rows(q_ref, T)
    kp = kp_ref[0] if len(kp_ref.shape) == 3 else kp_ref[...]
    vp = vp_ref[0] if len(vp_ref.shape) == 3 else vp_ref[...]
    kk = jnp.concatenate([kp.astype(F32), _chunk_rows(k_ref, T)], axis=0)
    vv = jnp.concatenate([vp.astype(F32), _chunk_rows(v_ref, T)], axis=0)

    tq = lax.broadcasted_iota(jnp.int32, (T, 2 * T), 0)
    jk = lax.broadcasted_iota(jnp.int32, (T, 2 * T), 1)
    dist = WINDOW + tq - jk
    valid = (dist >= 0) & (dist <= WINDOW)
    if not prev_always_valid:
        first_key = jnp.where(c > 0, 0, T)
        valid = valid & (jk >= first_key)
    dist_f = dist.astype(F32)
    scale = ATTN_HEAD_DIM ** -0.5

    outs = []
    for h in range(ATTN_HEADS):
        g = h // ATTN_GROUP
        slope = 2.0 ** (-8.0 * (h + 1) / ATTN_HEADS)
        qh = q[:, h * ATTN_HEAD_DIM:(h + 1) * ATTN_HEAD_DIM].astype(BF16)
        kg = kk[:, g * ATTN_HEAD_DIM:(g + 1) * ATTN_HEAD_DIM].astype(BF16)
        vg = vv[:, g * ATTN_HEAD_DIM:(g + 1) * ATTN_HEAD_DIM].astype(BF16)
        s = _dot_nt(qh, kg) * scale - slope * dist_f
        s = jnp.where(valid, s, -1e30)
        sink = sink_ref[:, h:h + 1]
        m = jnp.maximum(jnp.max(s, axis=-1, keepdims=True), sink)
        pr = jnp.where(valid, jnp.exp(s - m), 0.0)
        den = jnp.sum(pr, axis=-1, keepdims=True) + jnp.exp(sink - m)
        pr = pr / den
        outs.append(_dot(pr.astype(BF16), vg))
    y = jnp.concatenate(outs, axis=1)
    y_ref[...] = y[0:y_ref.shape[0], :].astype(y_ref.dtype)


def _attn(group, y, k_prev, v_prev, p):
    nb, nc = group.n_seq, group.chunks_per_seq
    rows = y.shape[0]
    if group.is_prompt:
        prev_k = pl.BlockSpec((CHUNK, ATTN_KV), lambda b, c: (b * nc + jnp.maximum(c - 1, 0), Y_K))
        prev_v = pl.BlockSpec((CHUNK, ATTN_KV), lambda b, c: (b * nc + jnp.maximum(c - 1, 0), Y_V))
        k_prev, v_prev = y, y
    else:
        prev_k = pl.BlockSpec((1, WINDOW, ATTN_KV), lambda b, c: (b, 0, 0))
        prev_v = pl.BlockSpec((1, WINDOW, ATTN_KV), lambda b, c: (b, 0, 0))
    kern = functools.partial(_attn_kernel, not group.is_prompt)
    return pl.pallas_call(
        kern,
        grid=(nb, nc),
        in_specs=[
            _chunk_spec(group, D_MODEL, Y_Q),
            _chunk_spec(group, ATTN_KV, Y_K),
            _chunk_spec(group, ATTN_KV, Y_V),
            prev_k,
            prev_v,
            _const_spec((1, LANES)),
        ],
        out_specs=_chunk_spec(group, D_MODEL, 0),
        out_shape=jax.ShapeDtypeStruct((rows, D_MODEL), BF16),
        compiler_params=_cparams("arbitrary", "arbitrary"),
        name="attn",
    )(y, y, y, k_prev, v_prev, p["attn_sinks"])


def _gelu_tanh(x):
    return 0.5 * x * (1.0 + jnp.tanh(math.sqrt(2.0 / math.pi) * (x + 0.044715 * (x * x * x))))


def _gmlp_kernel(u_ref, v_ref, lg_ref, lb_ref, ws_ref, bs_ref, y_ref, *maybe_v_out):
    T = CHUNK
    u = _gelu_tanh(_chunk_rows(u_ref, T))
    v = _gelu_tanh(_chunk_rows(v_ref, T))
    vc = v - jnp.mean(v, axis=-1, keepdims=True)
    v = vc * lax.rsqrt(jnp.mean(vc * vc, axis=-1, keepdims=True) + EPS) * lg_ref[...] + lb_ref[...]
    row = lax.broadcasted_iota(jnp.int32, (T, T), 0)
    col = lax.broadcasted_iota(jnp.int32, (T, T), 1)
    causal = row >= col
    vb = v.astype(BF16)
    mixed = []
    for g in range(GM_GROUPS):
        w = jnp.where(causal, ws_ref[g], 0.0).astype(BF16)
        gs = slice(g * GM_GROUP_WIDTH, (g + 1) * GM_GROUP_WIDTH)
        mixed.append(_dot(w, vb[:, gs]) + bs_ref[:, gs])
    y = u * jnp.concatenate(mixed, axis=1)
    y_ref[...] = y[0:y_ref.shape[0], :].astype(y_ref.dtype)
    if maybe_v_out:
        v_out = maybe_v_out[0]
        v_out[...] = v[0:v_out.shape[0], :]


def _gmlp(group, y, p):
    nb, nc = group.n_seq, group.chunks_per_seq
    rows = y.shape[0]
    out_specs = [_chunk_spec(group, D_MODEL, 0)]
    out_shape = [jax.ShapeDtypeStruct((rows, D_MODEL), BF16)]
    if not group.is_prompt:
        out_specs.append(_chunk_spec(group, D_MODEL, 0))
        out_shape.append(jax.ShapeDtypeStruct((rows, D_MODEL), F32))
    res = pl.pallas_call(
        _gmlp_kernel,
        grid=(nb, nc),
        in_specs=[
            _chunk_spec(group, D_MODEL, Y_U),
            _chunk_spec(group, D_MODEL, Y_VGM),
            _const_spec((1, D_MODEL)),
            _const_spec((1, D_MODEL)),
            _const_spec((GM_GROUPS, CHUNK, CHUNK)),
            _const_spec((CHUNK, D_MODEL)),
        ],
        out_specs=out_specs,
        out_shape=out_shape,
        compiler_params=_cparams("arbitrary", "arbitrary"),
        name="gmlp",
    )(y, y, p["gm_ln_g"], p["gm_ln_b"], p["gm_w_s"], p["gm_b_exp"])
    return (res[0], res[1]) if not group.is_prompt else (res[0], None)


def _merge_kernel(x_ref, ga_ref, ya_ref, yb_ref, yc_ref, yd_ref, g0_ref, g1_ref, g2_ref, g3_ref,
                  wb_ref, wo_ref, o_ref):
    merged = None
    for i, (y_ref, g_ref) in enumerate(((ya_ref, g0_ref), (yb_ref, g1_ref), (yc_ref, g2_ref),
                                        (yd_ref, g3_ref))):
        gate = jax.nn.sigmoid(g_ref[...].astype(F32))
        term = gate * _dot(y_ref[...], wb_ref[i])
        merged = term if merged is None else merged + term
    o = _dot(merged.astype(BF16), wo_ref[...])
    o_ref[...] = x_ref[...] + _mod_rows(ga_ref) * o


def _merge(group, x, mod, branches, y, p):
    tm = group.tm_merge
    rows = x.shape[0]
    tok = lambda piece: pl.BlockSpec((tm, D_MODEL), lambda i: (i, piece))
    return pl.pallas_call(
        _merge_kernel,
        grid=(rows // tm,),
        in_specs=[tok(0), _mod_spec(group, tm, 2)] + [tok(0)] * 4 + [tok(Y_G0 + i) for i in range(4)] + [
            pl.BlockSpec((N_BRANCH, D_MODEL, D_MODEL), lambda i: (0, 0, 0), pipeline_mode=pl.Buffered(1)),
            pl.BlockSpec((D_MODEL, D_MODEL), lambda i: (0, 0), pipeline_mode=pl.Buffered(1)),
        ],
        out_specs=tok(0),
        out_shape=jax.ShapeDtypeStruct((rows, D_MODEL), F32),
        compiler_params=_cparams("arbitrary"),
        name="merge",
    )(x, mod, *branches, y, y, y, y, p["w_branch"], p["w_o"])


def _ffn_kernel(seg8, l_valid, tiles_per_seq, final_norm,
                x_ref, g_ref, sc_ref, sh_ref, ga_ref, cp_ref, wa_ref, wg_ref, cw_ref, cb_ref, wd_ref,
                gf_ref, o_ref, cst_ref, carry_scr, xp_scr, act_scr):
    tm = x_ref.shape[0]
    i = pl.program_id(0)
    x = x_ref[...]
    h = _modnorm(x, g_ref[...], _mod_rows(sc_ref), _mod_rows(sh_ref)).astype(BF16)
    first = CARRY_ROWS - (FFN_CONV - 1)

    if not seg8:
        @pl.when(i % tiles_per_seq == 0)
        def _():
            carry_scr[...] = cp_ref[0]

    def conv_piece(w_ref, half, cs):
        off = half * D_FF
        up = _dot(h, w_ref[:, cs])
        col = slice(off + cs.start, off + cs.stop)
        if seg8:
            tb = tm // SAMPLE_ROWS
            xp_scr[:, 0:CARRY_ROWS, :] = cp_ref[:, :, col]
            xp_scr[:, CARRY_ROWS:CARRY_ROWS + SAMPLE_ROWS, :] = up.reshape(tb, SAMPLE_ROWS, MXU_WIDTH)
            acc = None
            for j in range(FFN_CONV):
                term = cw_ref[j:j + 1, col] * xp_scr[:, pl.ds(first + j, SAMPLE_ROWS), :]
                acc = term if acc is None else acc + term
            cst_ref[:, :, col] = xp_scr[:, pl.ds(CARRY_ROWS + l_valid - (FFN_CONV - 1), FFN_CONV - 1), :]
            return acc.reshape(tm, MXU_WIDTH) + cb_ref[:, col]
        xp_scr[0:CARRY_ROWS, :] = carry_scr[:, col]
        xp_scr[CARRY_ROWS:CARRY_ROWS + tm, :] = up
        acc = None
        for j in range(FFN_CONV):
            term = cw_ref[j:j + 1, col] * xp_scr[pl.ds(first + j, tm), :]
            acc = term if acc is None else acc + term
        carry_scr[:, col] = xp_scr[tm:tm + CARRY_ROWS, :]
        return acc + cb_ref[:, col]

    for k in range(D_FF // MXU_WIDTH):
        cs = slice(k * MXU_WIDTH, (k + 1) * MXU_WIDTH)
        a = conv_piece(wa_ref, 0, cs)
        g = conv_piece(wg_ref, 1, cs)
        act_scr[:, cs] = (_silu(a) * g).astype(BF16)

    if not seg8:
        @pl.when(i % tiles_per_seq == tiles_per_seq - 1)
        def _():
            cst_ref[0] = carry_scr[CARRY_ROWS - (FFN_CONV - 1):CARRY_ROWS, :]

    out = x + _mod_rows(ga_ref) * _dot(act_scr[...], wd_ref[...])
    if final_norm:
        out = out * lax.rsqrt(jnp.mean(out * out, axis=-1, keepdims=True) + EPS) * gf_ref[...]
    o_ref[...] = out


def _ffn(group, x, mod, conv_prev8, p, g_final, final_norm):
    tm = group.tm_ffn
    rows = x.shape[0]
    seg8 = not group.is_prompt
    tiles_per_seq = 1 if seg8 else group.seq_rows // tm
    single = dict(pipeline_mode=pl.Buffered(1))
    if seg8:
        tb = tm // SAMPLE_ROWS
        cp_spec = pl.BlockSpec((tb, CARRY_ROWS, 2 * D_FF), lambda i: (i, 0, 0))
        cst_spec = pl.BlockSpec((tb, FFN_CONV - 1, 2 * D_FF), lambda i: (i, 0, 0))
        xp_shape = (tb, CARRY_ROWS + SAMPLE_ROWS, MXU_WIDTH)
    else:
        cp_spec = pl.BlockSpec((1, CARRY_ROWS, 2 * D_FF), lambda i: (i // tiles_per_seq, 0, 0))
        cst_spec = pl.BlockSpec((1, FFN_CONV - 1, 2 * D_FF), lambda i: (i // tiles_per_seq, 0, 0))
        xp_shape = (CARRY_ROWS + tm, MXU_WIDTH)
    kern = functools.partial(_ffn_kernel, seg8, group.last_chunk_valid, tiles_per_seq, final_norm)
    return pl.pallas_call(
        kern,
        grid=(rows // tm,),
        in_specs=[
            pl.BlockSpec((tm, D_MODEL), lambda i: (i, 0)),
            _const_spec((1, D_MODEL)),
            _mod_spec(group, tm, 4),
            _mod_spec(group, tm, 3),
            _mod_spec(group, tm, 5),
            cp_spec,
            pl.BlockSpec((D_MODEL, D_FF), lambda i: (0, 0), **single),
            pl.BlockSpec((D_MODEL, D_FF), lambda i: (0, 0), **single),
            _const_spec((FFN_CONV, 2 * D_FF)),
            _const_spec((1, 2 * D_FF)),
            pl.BlockSpec((D_FF, D_MODEL), lambda i: (0, 0), **single),
            _const_spec((1, D_MODEL)),
        ],
        out_specs=[pl.BlockSpec((tm, D_MODEL), lambda i: (i, 0)), cst_spec],
        out_shape=[
            jax.ShapeDtypeStruct((rows, D_MODEL), F32),
            jax.ShapeDtypeStruct((group.n_seq, FFN_CONV - 1, 2 * D_FF), F32),
        ],
        scratch_shapes=[
            pltpu.VMEM((CARRY_ROWS, 2 * D_FF), F32),
            pltpu.VMEM(xp_shape, F32),
            pltpu.VMEM((tm, D_FF), BF16),
        ],
        compiler_params=_cparams("arbitrary"),
        name="ffn",
    )(x, p["g_norm_ffn"], mod, mod, mod, conv_prev8, p["w_up_a"], p["w_up_g"], p["ffn_conv_w"],
      p["ffn_conv_b"], p["w_down"], g_final)


class _Group:
    def __init__(self, is_prompt, n_seq, seq_len):
        self.is_prompt = is_prompt
        self.n_seq = n_seq
        if is_prompt:
            self.seq_rows = seq_len
            self.chunk_rows = CHUNK
            self.chunks_per_seq = seq_len // CHUNK
            self.last_chunk_valid = CHUNK
            self.per_token_mod = False
            self.tm_inproj = 1024
            self.tm_merge = 512
            self.tm_ffn = 512
        else:
            self.seq_rows = SAMPLE_ROWS
            self.chunk_rows = SAMPLE_ROWS
            self.chunks_per_seq = 1
            self.last_chunk_valid = seq_len
            self.per_token_mod = True
            self.tm_inproj = 512
            self.tm_merge = 512
            self.tm_ffn = 128


def _pad_front(state, rows):
    return jnp.pad(state, ((0, 0), (rows - state.shape[1], 0), (0, 0)))


def _layer_params(i, w):
    pad_heads = lambda v: jnp.pad(v, (0, LANES - SSD_HEADS)).reshape(1, LANES)
    w_in = w["w_in"][i]
    head_of_channel = jnp.arange(SSD_D_INNER) // SSD_HEAD_DIM
    expand = (jnp.arange(LANES)[:, None] == head_of_channel[None, :]).astype(BF16)
    return {
        "g_norm_mix": w["g_norm_mix"][i].reshape(1, D_MODEL),
        "w_main": jnp.concatenate([w_in[:, a:b] for a, b in _W_IN_PIECES], axis=1).astype(BF16),
        "w_dt": jnp.pad(w_in[:, _IN_OFF_DT:_IN_OFF_DT + SSD_HEADS],
                        ((0, 0), (0, LANES - SSD_HEADS))).astype(BF16),
        "ssd_conv_w": w["ssd_conv_w"][i],
        "ssd_conv_b": w["ssd_conv_b"][i].reshape(1, SSD_CONV_DIM),
        "ssd_dt_bias": pad_heads(w["ssd_dt_bias"][i]),
        "ssd_a_log": pad_heads(w["ssd_a_log"][i]),
        "ssd_d": jnp.repeat(w["ssd_d"][i], SSD_HEAD_DIM).reshape(1, SSD_D_INNER),
        "ssd_norm_g": w["ssd_norm_g"][i].reshape(1, SSD_D_INNER),
        "ssd_expand": expand,
        "sc_conv_w": w["sc_conv_w"][i],
        "attn_sinks": jnp.pad(w["attn_sinks"][i], (0, LANES - ATTN_HEADS)).reshape(1, LANES),
        "gm_ln_g": w["gm_ln_g"][i].reshape(1, D_MODEL),
        "gm_ln_b": w["gm_ln_b"][i].reshape(1, D_MODEL),
        "gm_w_s": w["gm_w_s"][i],
        "gm_b_exp": jnp.repeat(w["gm_b_s"][i].T, GM_GROUP_WIDTH, axis=1),
        "w_branch": w["w_branch"][i].astype(BF16),
        "w_o": w["w_o"][i].astype(BF16),
        "g_norm_ffn": w["g_norm_ffn"][i].reshape(1, D_MODEL),
        "w_up_a": w["ffn_w_up"][i][:, :D_FF].astype(BF16),
        "w_up_g": w["ffn_w_up"][i][:, D_FF:].astype(BF16),
        "ffn_conv_w": w["ffn_conv_w"][i],
        "ffn_conv_b": w["ffn_conv_b"][i].reshape(1, 2 * D_FF),
        "w_down": w["ffn_w_down"][i].astype(BF16),
    }


def _run_group(group, x, mod_all, states, params, g_final):
    n = group.n_seq
    outs = []
    for i in range(DEPTH):
        p = params[i]
        mod = mod_all[i]
        ssm0, ssdc0, scc0, k0, v0, ffc0 = (None if s is None else s[i] for s in states)
        if group.is_prompt:
            ssm0 = jnp.zeros((n, SSD_HEADS, SSD_HEAD_DIM, SSD_STATE), F32)
            ssdc0 = jnp.zeros((n, SSD_CONV - 1, SSD_CONV_DIM), F32)
            scc0 = jnp.zeros((n, SC_CONV - 1, D_MODEL), F32)
            ffc0 = jnp.zeros((n, FFN_CONV - 1, 2 * D_FF), F32)
        else:
            k0 = k0.reshape(n, WINDOW, ATTN_KV)
            v0 = v0.reshape(n, WINDOW, ATTN_KV)
        y, dtr = _inproj(group, x, mod, p["g_norm_mix"], p["w_main"], p["w_dt"])
        ya, ssm1, ssdc1 = _ssd(group, y, dtr, ssm0, _pad_front(ssdc0, CARRY_ROWS), p)
        yb, scc1 = _sconv(group, y, _pad_front(scc0, CARRY_ROWS), p)
        yc = _attn(group, y, k0, v0, p)
        yd, v_rows = _gmlp(group, y, p)
        x = _merge(group, x, mod, (ya, yb, yc, yd), y, p)
        x, ffc1 = _ffn(group, x, mod, _pad_front(ffc0, CARRY_ROWS), p, g_final, i == DEPTH - 1)

        kcol = slice(Y_K * SSD_BC, (Y_K + 1) * SSD_BC)
        vcol = slice(Y_V * SSD_BC, (Y_V + 1) * SSD_BC)
        if group.is_prompt:
            y3 = y.reshape(n, group.seq_rows, Y_COLS)
            k1 = y3[:, group.seq_rows - WINDOW:, kcol].astype(F32)
            v1 = y3[:, group.seq_rows - WINDOW:, vcol].astype(F32)
            k1 = k1.reshape(n, WINDOW, ATTN_KV_HEADS, ATTN_HEAD_DIM)
            v1 = v1.reshape(n, WINDOW, ATTN_KV_HEADS, ATTN_HEAD_DIM)
            outs.append((ssm1, ssdc1, scc1, k1, v1, ffc1))
        else:
            l = group.last_chunk_valid
            y3 = y.reshape(n, SAMPLE_ROWS, Y_COLS)
            k1 = jnp.concatenate([k0[:, l:], y3[:, :l, kcol].astype(F32)], axis=1)
            v1 = jnp.concatenate([v0[:, l:], y3[:, :l, vcol].astype(F32)], axis=1)
            k1 = k1.reshape(n, WINDOW, ATTN_KV_HEADS, ATTN_HEAD_DIM)
            v1 = v1.reshape(n, WINDOW, ATTN_KV_HEADS, ATTN_HEAD_DIM)
            gv = v_rows.reshape(n, SAMPLE_ROWS, D_MODEL)[:, :l]
            outs.append((ssm1, ssdc1, scc1, k1, v1, ffc1, gv))
    stacked = tuple(jnp.stack([o[j] for o in outs], axis=0) for j in range(len(outs[0])))
    return x, stacked


def kernel(x_prompt, x_sample, c_prompt, c_sample, state_ssm, state_ssd_conv, state_sc_conv, cache_k,
           cache_v, state_ffn_conv, w_ada, b_ada, g_norm_mix, w_in, ssd_conv_w, ssd_conv_b, ssd_dt_bias,
           ssd_a_log, ssd_d, ssd_norm_g, sc_conv_w, attn_sinks, gm_ln_g, gm_ln_b, gm_w_s, gm_b_s,
           w_branch, w_o, g_norm_ffn, ffn_w_up, ffn_conv_w, ffn_conv_b, ffn_w_down, g_final):
    weights = dict(g_norm_mix=g_norm_mix, w_in=w_in, ssd_conv_w=ssd_conv_w, ssd_conv_b=ssd_conv_b,
                   ssd_dt_bias=ssd_dt_bias, ssd_a_log=ssd_a_log, ssd_d=ssd_d, ssd_norm_g=ssd_norm_g,
                   sc_conv_w=sc_conv_w, attn_sinks=attn_sinks, gm_ln_g=gm_ln_g, gm_ln_b=gm_ln_b,
                   gm_w_s=gm_w_s, gm_b_s=gm_b_s, w_branch=w_branch, w_o=w_o, g_norm_ffn=g_norm_ffn,
                   ffn_w_up=ffn_w_up, ffn_conv_w=ffn_conv_w, ffn_conv_b=ffn_conv_b, ffn_w_down=ffn_w_down)
    params = [_layer_params(i, weights) for i in range(DEPTH)]
    gf = g_final.reshape(1, D_MODEL)

    nb_p, len_p, _ = x_prompt.shape
    nb_s, len_s, _ = x_sample.shape
    assert len_p % CHUNK == 0 and len_s <= SAMPLE_ROWS and len_s >= SSD_CONV - 1
    prompt = _Group(True, nb_p, len_p)
    sample = _Group(False, nb_s, len_s)

    c_rows = nb_p + nb_s
    c_pad = -c_rows % SAMPLE_ROWS
    c_all = jnp.pad(jnp.concatenate([c_prompt, c_sample], axis=0), ((0, c_pad), (0, 0)))
    mod = _ada(c_all, w_ada, b_ada)
    mod_p = mod[:, :nb_p].reshape(DEPTH, nb_p, 1, 6 * D_MODEL)
    mod_s = jnp.repeat(mod[:, nb_p:nb_p + nb_s], SAMPLE_ROWS, axis=1)

    xp = x_prompt.reshape(nb_p * len_p, D_MODEL)
    xs = jnp.pad(x_sample, ((0, 0), (0, SAMPLE_ROWS - len_s), (0, 0))).reshape(nb_s * SAMPLE_ROWS, D_MODEL)

    y_p, st_p = _run_group(prompt, xp, mod_p, (None,) * 6, params, gf)
    y_s, st_s = _run_group(sample, xs, mod_s,
                           (state_ssm, state_ssd_conv, state_sc_conv, cache_k, cache_v, state_ffn_conv),
                           params, gf)
    y_prompt = y_p.reshape(nb_p, len_p, D_MODEL)
    y_sample = y_s.reshape(nb_s, SAMPLE_ROWS, D_MODEL)[:, :len_s]
    return (y_prompt, y_sample) + st_p + st_s
```

```python
import functools
import math

import jax
import jax.numpy as jnp
from jax import lax
from jax.experimental import pallas as pl
from jax.experimental.pallas import tpu as pltpu

F32 = jnp.float32
BF16 = jnp.bfloat16

D_MODEL = 1024
DEPTH = 4
N_BRANCH = 4
SSD_HEADS = 16
SSD_HEAD_DIM = 64
SSD_GROUPS = 4
SSD_STATE = 64
SSD_CONV = 4
SSD_D_INNER = 1024
SSD_BC = SSD_GROUPS * SSD_STATE
SSD_CONV_DIM = SSD_D_INNER + 2 * SSD_BC
SC_CONV = 3
ATTN_HEADS = 16
ATTN_KV_HEADS = 4
ATTN_GROUP = 4
ATTN_HEAD_DIM = 64
ATTN_KV = ATTN_KV_HEADS * ATTN_HEAD_DIM
WINDOW = 128
GM_GROUPS = 8
GM_GROUP_WIDTH = 128
D_FF = 2816
FFN_CONV = 3
EPS = 1e-6

CHUNK = 128
SAMPLE_ROWS = 8
CARRY_ROWS = 8
LANES = 128
BF16_SUBLANES = 16
MXU_WIDTH = 256
VMEM_LIMIT = 56 * 1024 * 1024

(Y_Z, Y_XS, Y_BG, Y_CG, Y_XSC, Y_Q, Y_U, Y_VGM, Y_G0) = range(9)
Y_WIDE = 12
Y_SSDB, Y_SSDC, Y_K, Y_V = (Y_WIDE * 4 + i for i in range(4))
Y_COLS = Y_WIDE * D_MODEL + 4 * SSD_BC

_IN_OFF_XBC = 1024
_IN_OFF_DT = 2560
_IN_OFF_BCX = 2576
_IN_OFF_Q = 5648
_IN_OFF_K = 6672
_IN_OFF_V = 6928
_IN_OFF_UV = 7184
_IN_OFF_GATES = 9232
_W_IN_PIECES = (
    (0, 1024),
    (_IN_OFF_XBC, _IN_OFF_XBC + 1024),
    (_IN_OFF_BCX, _IN_OFF_BCX + 1024),
    (_IN_OFF_BCX + 1024, _IN_OFF_BCX + 2048),
    (_IN_OFF_BCX + 2048, _IN_OFF_BCX + 3072),
    (_IN_OFF_Q, _IN_OFF_Q + 1024),
    (_IN_OFF_UV, _IN_OFF_UV + 1024),
    (_IN_OFF_UV + 1024, _IN_OFF_UV + 2048),
    (_IN_OFF_GATES, _IN_OFF_GATES + 4096),
    (_IN_OFF_XBC + 1024, _IN_OFF_XBC + 1280),
    (_IN_OFF_XBC + 1280, _IN_OFF_XBC + 1536),
    (_IN_OFF_K, _IN_OFF_K + 256),
    (_IN_OFF_V, _IN_OFF_V + 256),
)


def _cparams(*sem):
    return pltpu.CompilerParams(dimension_semantics=sem, vmem_limit_bytes=VMEM_LIMIT)


def _const_spec(shape):
    zeros = (0,) * len(shape)
    return pl.BlockSpec(shape, lambda *_: zeros)


def _silu(x):
    return x * jax.nn.sigmoid(x)


def _mod_rows(ref):
    return ref[0] if len(ref.shape) == 3 else ref[...]


def _modnorm(x, g, sc, sh):
    r = lax.rsqrt(jnp.mean(x * x, axis=-1, keepdims=True) + EPS)
    return (x * r * g) * (1.0 + sc) + sh


def _split3(x):
    hi = x.astype(BF16)
    r1 = x - hi.astype(F32)
    mid = r1.astype(BF16)
    lo = (r1 - mid.astype(F32)).astype(BF16)
    return hi, mid, lo


def _dot(a, b):
    return jnp.dot(a, b, preferred_element_type=F32)


def _dot_nt(a, b):
    return lax.dot_general(a, b, (((1,), (1,)), ((), ())), preferred_element_type=F32)


def _dot_tn(a, b):
    return lax.dot_general(a, b, (((0,), (0,)), ((), ())), preferred_element_type=F32)


def _dot_exact_lhs(a_bf16, x):
    hi, mid, lo = _split3(x)
    return _dot(a_bf16, hi) + _dot(a_bf16, mid) + _dot(a_bf16, lo)


def _dot_exact_rhs(x, e_bf16):
    hi, mid, lo = _split3(x)
    return _dot(hi, e_bf16) + _dot(mid, e_bf16) + _dot(lo, e_bf16)


def _chunk_rows(ref, rows):
    x = ref[...].astype(F32)
    if x.shape[0] == rows:
        return x
    pad = jnp.zeros((rows - x.shape[0], x.shape[1]), F32)
    return jnp.concatenate([x, pad], axis=0)


def _ada_kernel(c_ref, w_ref, b_ref, o_ref):
    s = _silu(c_ref[...]).astype(BF16)
    o_ref[0] = _dot(s, w_ref[0].astype(BF16)) + b_ref[0]


def _ada(c_all, w_ada, b_ada):
    rows = c_all.shape[0]
    n_tiles = w_ada.shape[-1] // D_MODEL
    return pl.pallas_call(
        _ada_kernel,
        grid=(DEPTH, n_tiles),
        in_specs=[
            _const_spec((rows, D_MODEL)),
            pl.BlockSpec((1, D_MODEL, D_MODEL), lambda l, n: (l, 0, n)),
            pl.BlockSpec((1, 1, D_MODEL), lambda l, n: (l, 0, n)),
        ],
        out_specs=pl.BlockSpec((1, rows, D_MODEL), lambda l, n: (l, 0, n)),
        out_shape=jax.ShapeDtypeStruct((DEPTH, rows, 6 * D_MODEL), F32),
        compiler_params=_cparams("arbitrary", "arbitrary"),
        name="ada",
    )(c_all, w_ada, b_ada.reshape(DEPTH, 1, 6 * D_MODEL))


def _inproj_kernel(x_ref, g_ref, sc_ref, sh_ref, w_ref, wdt_ref, y_ref, dtr_ref, h_scr):
    @pl.when(pl.program_id(1) == 0)
    def _():
        h = _modnorm(x_ref[...], g_ref[...], _mod_rows(sc_ref), _mod_rows(sh_ref)).astype(BF16)
        h_scr[...] = h
        dtr_ref[...] = _dot(h, wdt_ref[...])

    y_ref[...] = _dot(h_scr[...], w_ref[...]).astype(y_ref.dtype)


def _mod_spec(group, tm, piece):
    if group.per_token_mod:
        return pl.BlockSpec((tm, D_MODEL), lambda i, *_: (i, piece))
    tiles_per_seq = group.seq_rows // tm
    return pl.BlockSpec((1, 1, D_MODEL), lambda i, *_: (i // tiles_per_seq, 0, piece))


def _inproj(group, x, mod, g, w_main, w_dt):
    tm = group.tm_inproj
    rows = x.shape[0]
    tn = D_MODEL
    return pl.pallas_call(
        _inproj_kernel,
        grid=(rows // tm, Y_COLS // tn),
        in_specs=[
            pl.BlockSpec((tm, D_MODEL), lambda i, n: (i, 0)),
            _const_spec((1, D_MODEL)),
            _mod_spec(group, tm, 1),
            _mod_spec(group, tm, 0),
            pl.BlockSpec((D_MODEL, tn), lambda i, n: (0, n)),
            _const_spec((D_MODEL, LANES)),
        ],
        out_specs=[
            pl.BlockSpec((tm, tn), lambda i, n: (i, n)),
            pl.BlockSpec((tm, LANES), lambda i, n: (i, 0)),
        ],
        out_shape=[
            jax.ShapeDtypeStruct((rows, Y_COLS), BF16),
            jax.ShapeDtypeStruct((rows, LANES), F32),
        ],
        scratch_shapes=[pltpu.VMEM((tm, D_MODEL), BF16)],
        compiler_params=_cparams("arbitrary", "arbitrary"),
        name="inproj",
    )(x, g, mod, mod, w_main, w_dt)


def _causal_conv(xp_ref, w_ref, width, rows):
    first = CARRY_ROWS - (width - 1)
    acc = w_ref[0:1, :] * xp_ref[pl.ds(first, rows), :]
    for j in range(1, width):
        acc = acc + w_ref[j:j + 1, :] * xp_ref[pl.ds(first + j, rows), :]
    return acc


def _chunk_spec(group, width, piece):
    nc = group.chunks_per_seq
    return pl.BlockSpec((group.chunk_rows, width), lambda b, c: (b * nc + c, piece))


def _ssd_kernel(l_valid, n_chunks,
                z_ref, xs_ref, bm_ref, cm_ref, dtr_ref, s0_ref, cp_ref,
                cw_ref, cb_ref, dtb_ref, alog_ref, dsk_ref, ng_ref, e_ref,
                y_ref, sfin_ref, cst_ref,
                s_scr, xp_scr):
    T = CHUNK
    c = pl.program_id(1)

    @pl.when(c == 0)
    def _():
        s_scr[...] = s0_ref[0]
        xp_scr[0:CARRY_ROWS, :] = cp_ref[0]

    xp_scr[CARRY_ROWS:CARRY_ROWS + T, 0:SSD_D_INNER] = _chunk_rows(xs_ref, T)
    xp_scr[CARRY_ROWS:CARRY_ROWS + T, SSD_D_INNER:SSD_D_INNER + SSD_BC] = _chunk_rows(bm_ref, T)
    xp_scr[CARRY_ROWS:CARRY_ROWS + T, SSD_D_INNER + SSD_BC:SSD_CONV_DIM] = _chunk_rows(cm_ref, T)
    xbc = _silu(_causal_conv(xp_scr, cw_ref, SSD_CONV, T) + cb_ref[...])
    xs = xbc[:, 0:SSD_D_INNER]
    bmat = xbc[:, SSD_D_INNER:SSD_D_INNER + SSD_BC].astype(BF16)
    cmat = xbc[:, SSD_D_INNER + SSD_BC:SSD_CONV_DIM].astype(BF16)

    pre = _chunk_rows(dtr_ref, T) + dtb_ref[...]
    dt = jnp.maximum(pre, 0.0) + jnp.log1p(jnp.exp(-jnp.abs(pre)))
    row = lax.broadcasted_iota(jnp.int32, (T, T), 0)
    col = lax.broadcasted_iota(jnp.int32, (T, T), 1)
    if l_valid < T:
        rvalid = lax.broadcasted_iota(jnp.int32, (T, LANES), 0) < l_valid
        dt = jnp.where(rvalid, dt, 0.0)
    dta = dt * (-jnp.exp(alog_ref[...]))
    causal = row >= col
    tri = jnp.where(causal, 1.0, 0.0).astype(BF16)
    cs = _dot_exact_lhs(tri, dta)
    cs_t = cs.T
    total = cs[T - 1:T, :]
    e = e_ref[...]
    dt_x = _dot_exact_rhs(dt, e)
    ecs_x = _dot_exact_rhs(jnp.exp(cs), e)
    dec_x = _dot_exact_rhs(jnp.exp(total - cs), e)
    chunk_decay = jnp.exp(total)

    x_dt = xs * dt_x
    x_b = x_dt.astype(BF16)
    x_dec = (x_dt * dec_x).astype(BF16)

    lane = lax.broadcasted_iota(jnp.int32, (T, 2 * SSD_HEAD_DIM), 1)
    low_half = lane < SSD_HEAD_DIM

    y_parts = []
    for g in range(SSD_GROUPS):
        gs = slice(g * SSD_STATE, (g + 1) * SSD_STATE)
        cb = _dot_nt(cmat[:, gs], bmat[:, gs])
        heads_per_group = SSD_HEADS // SSD_GROUPS
        s_g = s_scr[g * heads_per_group:(g + 1) * heads_per_group].reshape(
            heads_per_group * SSD_HEAD_DIM, SSD_STATE)
        y_off = _dot_nt(cmat[:, gs], s_g.astype(BF16))
        width = heads_per_group * SSD_HEAD_DIM
        ds_g = _dot_tn(x_dec[:, g * width:(g + 1) * width], bmat[:, gs])
        pair_out = []
        for pr in range(heads_per_group // 2):
            h0 = g * heads_per_group + 2 * pr
            xp = x_b[:, h0 * SSD_HEAD_DIM:(h0 + 2) * SSD_HEAD_DIM]
            acc = None
            for k in range(2):
                h = h0 + k
                diff = cs[:, h:h + 1] - cs_t[h:h + 1, :]
                lmat = jnp.where(causal, jnp.exp(jnp.where(causal, diff, 0.0)), 0.0)
                m = (cb * lmat).astype(BF16)
                keep = low_half if k == 0 else jnp.logical_not(low_half)
                part = _dot(m, jnp.where(keep, xp, jnp.zeros_like(xp)))
                acc = part if acc is None else acc + part
            pair_out.append(acc)
        y_parts.append(jnp.concatenate(pair_out, axis=1) + y_off * ecs_x[:, g * width:(g + 1) * width])
        for r in range(heads_per_group):
            h = g * heads_per_group + r
            s_scr[h] = (s_scr[h] * chunk_decay[:, h:h + 1]
                        + ds_g[r * SSD_HEAD_DIM:(r + 1) * SSD_HEAD_DIM, :])

    y = jnp.concatenate(y_parts, axis=1) + xs * dsk_ref[...]
    y = y * _silu(_chunk_rows(z_ref, T))
    gw = SSD_D_INNER // SSD_GROUPS
    normed = []
    for g in range(SSD_GROUPS):
        yg = y[:, g * gw:(g + 1) * gw]
        normed.append(yg * lax.rsqrt(jnp.mean(yg * yg, axis=-1, keepdims=True) + EPS))
    y = jnp.concatenate(normed, axis=1) * ng_ref[...]
    y_ref[...] = y[0:y_ref.shape[0], :].astype(y_ref.dtype)

    @pl.when(c == n_chunks - 1)
    def _():
        sfin_ref[0] = s_scr[...]
        cst_ref[0] = xp_scr[pl.ds(CARRY_ROWS + l_valid - (SSD_CONV - 1), SSD_CONV - 1), :]

    if n_chunks > 1:
        xp_scr[0:CARRY_ROWS, :] = xp_scr[T:T + CARRY_ROWS, :]


def _ssd(group, y, dtr, s0, conv_prev8, p):
    nb, nc = group.n_seq, group.chunks_per_seq
    rows = y.shape[0]
    kern = functools.partial(_ssd_kernel, group.last_chunk_valid, nc)
    return pl.pallas_call(
        kern,
        grid=(nb, nc),
        in_specs=[
            _chunk_spec(group, D_MODEL, Y_Z),
            _chunk_spec(group, D_MODEL, Y_XS),
            _chunk_spec(group, SSD_BC, Y_SSDB),
            _chunk_spec(group, SSD_BC, Y_SSDC),
            _chunk_spec(group, LANES, 0),
            pl.BlockSpec((1, SSD_HEADS, SSD_HEAD_DIM, SSD_STATE), lambda b, c: (b, 0, 0, 0)),
            pl.BlockSpec((1, CARRY_ROWS, SSD_CONV_DIM), lambda b, c: (b, 0, 0)),
            _const_spec((SSD_CONV, SSD_CONV_DIM)),
            _const_spec((1, SSD_CONV_DIM)),
            _const_spec((1, LANES)),
            _const_spec((1, LANES)),
            _const_spec((1, SSD_D_INNER)),
            _const_spec((1, SSD_D_INNER)),
            _const_spec((LANES, SSD_D_INNER)),
        ],
        out_specs=[
            _chunk_spec(group, D_MODEL, 0),
            pl.BlockSpec((1, SSD_HEADS, SSD_HEAD_DIM, SSD_STATE), lambda b, c: (b, 0, 0, 0)),
            pl.BlockSpec((1, SSD_CONV - 1, SSD_CONV_DIM), lambda b, c: (b, 0, 0)),
        ],
        out_shape=[
            jax.ShapeDtypeStruct((rows, D_MODEL), BF16),
            jax.ShapeDtypeStruct((nb, SSD_HEADS, SSD_HEAD_DIM, SSD_STATE), F32),
            jax.ShapeDtypeStruct((nb, SSD_CONV - 1, SSD_CONV_DIM), F32),
        ],
        scratch_shapes=[
            pltpu.VMEM((SSD_HEADS, SSD_HEAD_DIM, SSD_STATE), F32),
            pltpu.VMEM((CARRY_ROWS + CHUNK, SSD_CONV_DIM), F32),
        ],
        compiler_params=_cparams("arbitrary", "arbitrary"),
        name="ssd",
    )(y, y, y, y, dtr, s0, conv_prev8,
      p["ssd_conv_w"], p["ssd_conv_b"], p["ssd_dt_bias"], p["ssd_a_log"], p["ssd_d"],
      p["ssd_norm_g"], p["ssd_expand"])


def _sconv_kernel(l_valid, n_chunks, bg_ref, cg_ref, xs_ref, cp_ref, cw_ref, y_ref, cst_ref, xp_scr):
    T = CHUNK
    c = pl.program_id(1)

    @pl.when(c == 0)
    def _():
        xp_scr[0:CARRY_ROWS, :] = cp_ref[0]

    xp_scr[CARRY_ROWS:CARRY_ROWS + T, :] = _chunk_rows(cg_ref, T) * _chunk_rows(xs_ref, T)
    y = _chunk_rows(bg_ref, T) * _causal_conv(xp_scr, cw_ref, SC_CONV, T)
    y_ref[...] = y[0:y_ref.shape[0], :].astype(y_ref.dtype)

    @pl.when(c == n_chunks - 1)
    def _():
        cst_ref[0] = xp_scr[pl.ds(CARRY_ROWS + l_valid - (SC_CONV - 1), SC_CONV - 1), :]

    if n_chunks > 1:
        xp_scr[0:CARRY_ROWS, :] = xp_scr[T:T + CARRY_ROWS, :]


def _sconv(group, y, conv_prev8, p):
    nb, nc = group.n_seq, group.chunks_per_seq
    rows = y.shape[0]
    kern = functools.partial(_sconv_kernel, group.last_chunk_valid, nc)
    return pl.pallas_call(
        kern,
        grid=(nb, nc),
        in_specs=[
            _chunk_spec(group, D_MODEL, Y_BG),
            _chunk_spec(group, D_MODEL, Y_CG),
            _chunk_spec(group, D_MODEL, Y_XSC),
            pl.BlockSpec((1, CARRY_ROWS, D_MODEL), lambda b, c: (b, 0, 0)),
            _const_spec((SC_CONV, D_MODEL)),
        ],
        out_specs=[
            _chunk_spec(group, D_MODEL, 0),
            pl.BlockSpec((1, SC_CONV - 1, D_MODEL), lambda b, c: (b, 0, 0)),
        ],
        out_shape=[
            jax.ShapeDtypeStruct((rows, D_MODEL), BF16),
            jax.ShapeDtypeStruct((nb, SC_CONV - 1, D_MODEL), F32),
        ],
        scratch_shapes=[pltpu.VMEM((CARRY_ROWS + CHUNK, D_MODEL), F32)],
        compiler_params=_cparams("arbitrary", "arbitrary"),
        name="sconv",
    )(y, y, y, conv_prev8, p["sc_conv_w"])


def _spread_kv(k_rows, v_rows):
    ku = pltpu.bitcast(k_rows, jnp.uint32)
    vu = pltpu.bitcast(v_rows, jnp.uint32)
    blk = lax.broadcasted_iota(jnp.int32, ku.shape, 1) // ATTN_HEAD_DIM
    kspread, vplaced = [], []
    for g in range(ATTN_KV_HEADS):
        tk = jnp.where(blk == g, ku, 0)
        uk = tk | pltpu.roll(tk, ATTN_HEAD_DIM, 1)
        kspread.append(pltpu.bitcast(uk | pltpu.roll(uk, 2 * ATTN_HEAD_DIM, 1), BF16))
        tv = jnp.where(blk == g, vu, 0)
        tv1 = pltpu.roll(tv, ATTN_HEAD_DIM, 1)
        shifted = (tv, tv1, pltpu.roll(tv, 2 * ATTN_HEAD_DIM, 1), pltpu.roll(tv1, 2 * ATTN_HEAD_DIM, 1))
        vplaced.append([pltpu.bitcast(shifted[(r - g) % ATTN_KV_HEADS], BF16) for r in range(ATTN_GROUP)])
    return kspread, vplaced


def _attn_kernel(tq_rows, carry, *refs):
    if carry:
        q_ref, k_ref, v_ref, sink_ref, bias_ref, y_ref, kprev_scr, vprev_scr = refs
    else:
        q_ref, k_ref, v_ref, kp_ref, vp_ref, sink_ref, bias_ref, y_ref = refs
    T = tq_rows
    c = pl.program_id(1)
    scale = ATTN_HEAD_DIM ** -0.5
    if q_ref.shape[0] == T:
        q = q_ref[...] * jnp.asarray(scale, q_ref.dtype)
    else:
        q = (_chunk_rows(q_ref, T) * scale).astype(BF16)
    if k_ref.shape[0] == WINDOW:
        kc, vc = k_ref[...], v_ref[...]
    else:
        kc = _chunk_rows(k_ref, WINDOW).astype(BF16)
        vc = _chunk_rows(v_ref, WINDOW).astype(BF16)
    ks_cur, vp_cur = _spread_kv(kc, vc)
    if carry:
        read_slot = (c + 1) % 2
        write_slot = c % 2

        @pl.when(c == 0)
        def _():
            kprev_scr[read_slot] = jnp.zeros(kprev_scr.shape[1:], BF16)
            vprev_scr[read_slot] = jnp.zeros(vprev_scr.shape[1:], BF16)
        ks_prev = [kprev_scr[read_slot, g] for g in range(ATTN_KV_HEADS)]
        vp_prev = [[vprev_scr[read_slot, g * ATTN_GROUP + r] for r in range(ATTN_GROUP)]
                   for g in range(ATTN_KV_HEADS)]
    else:
        ks_prev, vp_prev = _spread_kv(kp_ref[0].astype(BF16), vp_ref[0].astype(BF16))

    q_blk = lax.broadcasted_iota(jnp.int32, (T, ATTN_KV), 1) // ATTN_HEAD_DIM
    scores = []
    for g in range(ATTN_KV_HEADS):
        qg = q[:, g * ATTN_KV:(g + 1) * ATTN_KV]
        q4 = jnp.concatenate([jnp.where(q_blk == r, qg, jnp.zeros_like(qg))
                              for r in range(ATTN_GROUP)], axis=0)
        keys = jnp.concatenate([ks_prev[g], ks_cur[g]], axis=0)
        scores.append(_dot_nt(q4, keys))
    groups = range(ATTN_KV_HEADS)
    bias = [bias_ref[0, g * ATTN_GROUP:(g + 1) * ATTN_GROUP].reshape(ATTN_GROUP * T, 2 * WINDOW)
            for g in groups]
    sinks = [jnp.concatenate([jnp.broadcast_to(sink_ref[:, h:h + 1], (T, 1))
                              for h in range(g * ATTN_GROUP, (g + 1) * ATTN_GROUP)], axis=0)
             for g in groups]
    s = [scores[g] - bias[g] for g in groups]
    m = [jnp.maximum(jnp.max(s[g], axis=-1, keepdims=True), sinks[g]) for g in groups]
    e = [jnp.exp(s[g] - m[g]) for g in groups]
    den = [jnp.sum(e[g], axis=-1, keepdims=True) + jnp.exp(sinks[g] - m[g]) for g in groups]
    pn = [(e[g] * (1.0 / den[g])).astype(BF16) for g in groups]
    probs = [jnp.concatenate([pn[g][r * T:(r + 1) * T] for r in range(ATTN_GROUP)], axis=1)
             for g in groups]
    outs = []
    for g in range(ATTN_KV_HEADS):
        vals = jnp.concatenate([blk for r in range(ATTN_GROUP)
                                for blk in (vp_prev[g][r], vp_cur[g][r])], axis=0)
        outs.append(_dot(probs[g], vals))
    y = jnp.concatenate(outs, axis=1)
    y_ref[...] = y[0:y_ref.shape[0], :].astype(y_ref.dtype)
    if carry:
        for g in range(ATTN_KV_HEADS):
            kprev_scr[write_slot, g] = ks_cur[g]
            for r in range(ATTN_GROUP):
                vprev_scr[write_slot, g * ATTN_GROUP + r] = vp_cur[g][r]


def _attn_bias_table(q_rows):
    t = jnp.arange(q_rows)[:, None]
    j = jnp.arange(2 * WINDOW)[None, :]
    dist = WINDOW + t - j
    band = (dist >= 0) & (dist <= WINDOW)
    slopes = 2.0 ** (-8.0 * jnp.arange(1, ATTN_HEADS + 1, dtype=F32) / ATTN_HEADS)
    bias = slopes[:, None, None] * dist.astype(F32)[None]
    first = jnp.where(band & (j >= WINDOW), bias, 1e30)
    later = jnp.where(band, bias, 1e30)
    return jnp.stack([first, later], axis=0)


def _attn(group, y, k_prev, v_prev, p):
    nb, nc = group.n_seq, group.chunks_per_seq
    rows = y.shape[0]
    tq = group.attn_q_rows
    carry = group.is_prompt
    in_specs = [
        _chunk_spec(group, D_MODEL, Y_Q),
        _chunk_spec(group, ATTN_KV, Y_K),
        _chunk_spec(group, ATTN_KV, Y_V),
    ]
    args = [y, y, y]
    scratch = []
    if carry:
        bias_spec = pl.BlockSpec((1, ATTN_HEADS, tq, 2 * WINDOW),
                                 lambda b, c: (jnp.minimum(c, 1), 0, 0, 0))
        scratch = [pltpu.VMEM((2, ATTN_KV_HEADS, WINDOW, ATTN_KV), BF16),
                   pltpu.VMEM((2, ATTN_HEADS, WINDOW, ATTN_KV), BF16)]
    else:
        bias_spec = pl.BlockSpec((1, ATTN_HEADS, tq, 2 * WINDOW), lambda b, c: (1, 0, 0, 0))
        in_specs += [pl.BlockSpec((1, WINDOW, ATTN_KV), lambda b, c: (b, 0, 0))] * 2
        args += [k_prev, v_prev]
    in_specs += [_const_spec((1, LANES)), bias_spec]
    args += [p["attn_sinks"], _attn_bias_table(tq)]
    return pl.pallas_call(
        functools.partial(_attn_kernel, tq, carry),
        grid=(nb, nc),
        in_specs=in_specs,
        out_specs=_chunk_spec(group, D_MODEL, 0),
        out_shape=jax.ShapeDtypeStruct((rows, D_MODEL), BF16),
        scratch_shapes=scratch,
        compiler_params=_cparams("arbitrary", "arbitrary"),
        name="attn_p" if group.is_prompt else "attn_s",
    )(*args)


def _gelu_tanh(x):
    return 0.5 * x * (1.0 + jnp.tanh(math.sqrt(2.0 / math.pi) * (x + 0.044715 * (x * x * x))))


def _gmlp_kernel(u_ref, v_ref, lg_ref, lb_ref, ws_ref, bs_ref, y_ref, *maybe_v_out):
    T = CHUNK
    u = _gelu_tanh(_chunk_rows(u_ref, T))
    v = _gelu_tanh(_chunk_rows(v_ref, T))
    vc = v - jnp.mean(v, axis=-1, keepdims=True)
    v = vc * lax.rsqrt(jnp.mean(vc * vc, axis=-1, keepdims=True) + EPS) * lg_ref[...] + lb_ref[...]
    row = lax.broadcasted_iota(jnp.int32, (T, T), 0)
    col = lax.broadcasted_iota(jnp.int32, (T, T), 1)
    causal = row >= col
    vb = v.astype(BF16)
    mixed = []
    for g in range(GM_GROUPS):
        w = jnp.where(causal, ws_ref[g], 0.0).astype(BF16)
        gs = slice(g * GM_GROUP_WIDTH, (g + 1) * GM_GROUP_WIDTH)
        mixed.append(_dot(w, vb[:, gs]) + bs_ref[:, gs])
    y = u * jnp.concatenate(mixed, axis=1)
    y_ref[...] = y[0:y_ref.shape[0], :].astype(y_ref.dtype)
    if maybe_v_out:
        v_out = maybe_v_out[0]
        v_out[...] = v[0:v_out.shape[0], :]


def _gmlp(group, y, p):
    nb, nc = group.n_seq, group.chunks_per_seq
    rows = y.shape[0]
    out_specs = [_chunk_spec(group, D_MODEL, 0)]
    out_shape = [jax.ShapeDtypeStruct((rows, D_MODEL), BF16)]
    if not group.is_prompt:
        out_specs.append(_chunk_spec(group, D_MODEL, 0))
        out_shape.append(jax.ShapeDtypeStruct((rows, D_MODEL), F32))
    res = pl.pallas_call(
        _gmlp_kernel,
        grid=(nb, nc),
        in_specs=[
            _chunk_spec(group, D_MODEL, Y_U),
            _chunk_spec(group, D_MODEL, Y_VGM),
            _const_spec((1, D_MODEL)),
            _const_spec((1, D_MODEL)),
            _const_spec((GM_GROUPS, CHUNK, CHUNK)),
            _const_spec((CHUNK, D_MODEL)),
        ],
        out_specs=out_specs,
        out_shape=out_shape,
        compiler_params=_cparams("arbitrary", "arbitrary"),
        name="gmlp",
    )(y, y, p["gm_ln_g"], p["gm_ln_b"], p["gm_w_s"], p["gm_b_exp"])
    return (res[0], res[1]) if not group.is_prompt else (res[0], None)


def _merge_kernel(x_ref, ga_ref, ya_ref, yb_ref, yc_ref, yd_ref, g0_ref, g1_ref, g2_ref, g3_ref,
                  wb_ref, wo_ref, o_ref):
    merged = None
    for i, (y_ref, g_ref) in enumerate(((ya_ref, g0_ref), (yb_ref, g1_ref), (yc_ref, g2_ref),
                                        (yd_ref, g3_ref))):
        gate = jax.nn.sigmoid(g_ref[...].astype(F32))
        term = gate * _dot(y_ref[...], wb_ref[i])
        merged = term if merged is None else merged + term
    o = _dot(merged.astype(BF16), wo_ref[...])
    o_ref[...] = x_ref[...] + _mod_rows(ga_ref) * o


def _merge(group, x, mod, branches, y, p):
    tm = group.tm_merge
    rows = x.shape[0]
    tok = lambda piece: pl.BlockSpec((tm, D_MODEL), lambda i: (i, piece))
    return pl.pallas_call(
        _merge_kernel,
        grid=(rows // tm,),
        in_specs=[tok(0), _mod_spec(group, tm, 2)] + [tok(0)] * 4 + [tok(Y_G0 + i) for i in range(4)] + [
            pl.BlockSpec((N_BRANCH, D_MODEL, D_MODEL), lambda i: (0, 0, 0), pipeline_mode=pl.Buffered(1)),
            pl.BlockSpec((D_MODEL, D_MODEL), lambda i: (0, 0), pipeline_mode=pl.Buffered(1)),
        ],
        out_specs=tok(0),
        out_shape=jax.ShapeDtypeStruct((rows, D_MODEL), F32),
        compiler_params=_cparams("arbitrary"),
        name="merge",
    )(x, mod, *branches, y, y, y, y, p["w_branch"], p["w_o"])


def _ffn_kernel(seg8, l_valid, tiles_per_seq, final_norm,
                x_ref, g_ref, sc_ref, sh_ref, ga_ref, cp_ref, wa_ref, wg_ref, cw_ref, cb_ref, wd_ref,
                gf_ref, o_ref, cst_ref, carry_scr, xp_scr, act_scr):
    tm = x_ref.shape[0]
    i = pl.program_id(0)
    x = x_ref[...]
    h = _modnorm(x, g_ref[...], _mod_rows(sc_ref), _mod_rows(sh_ref)).astype(BF16)
    first = CARRY_ROWS - (FFN_CONV - 1)

    if not seg8:
        @pl.when(i % tiles_per_seq == 0)
        def _():
            carry_scr[...] = cp_ref[0]

    def conv_piece(w_ref, half, cs):
        off = half * D_FF
        up = _dot(h, w_ref[:, cs])
        col = slice(off + cs.start, off + cs.stop)
        if seg8:
            tb = tm // SAMPLE_ROWS
            xp_scr[:, 0:CARRY_ROWS, :] = cp_ref[:, :, col]
            xp_scr[:, CARRY_ROWS:CARRY_ROWS + SAMPLE_ROWS, :] = up.reshape(tb, SAMPLE_ROWS, MXU_WIDTH)
            acc = None
            for j in range(FFN_CONV):
                term = cw_ref[j:j + 1, col] * xp_scr[:, pl.ds(first + j, SAMPLE_ROWS), :]
                acc = term if acc is None else acc + term
            cst_ref[:, :, col] = xp_scr[:, pl.ds(CARRY_ROWS + l_valid - (FFN_CONV - 1), FFN_CONV - 1), :]
            return acc.reshape(tm, MXU_WIDTH) + cb_ref[:, col]
        xp_scr[0:CARRY_ROWS, :] = carry_scr[:, col]
        xp_scr[CARRY_ROWS:CARRY_ROWS + tm, :] = up
        acc = None
        for j in range(FFN_CONV):
            term = cw_ref[j:j + 1, col] * xp_scr[pl.ds(first + j, tm), :]
            acc = term if acc is None else acc + term
        carry_scr[:, col] = xp_scr[tm:tm + CARRY_ROWS, :]
        return acc + cb_ref[:, col]

    for k in range(D_FF // MXU_WIDTH):
        cs = slice(k * MXU_WIDTH, (k + 1) * MXU_WIDTH)
        a = conv_piece(wa_ref, 0, cs)
        g = conv_piece(wg_ref, 1, cs)
        act_scr[:, cs] = (_silu(a) * g).astype(BF16)

    if not seg8:
        @pl.when(i % tiles_per_seq == tiles_per_seq - 1)
        def _():
            cst_ref[0] = carry_scr[CARRY_ROWS - (FFN_CONV - 1):CARRY_ROWS, :]

    out = x + _mod_rows(ga_ref) * _dot(act_scr[...], wd_ref[...])
    if final_norm:
        out = out * lax.rsqrt(jnp.mean(out * out, axis=-1, keepdims=True) + EPS) * gf_ref[...]
    o_ref[...] = out


def _ffn(group, x, mod, conv_prev8, p, g_final, final_norm):
    tm = group.tm_ffn
    rows = x.shape[0]
    seg8 = not group.is_prompt
    tiles_per_seq = 1 if seg8 else group.seq_rows // tm
    single = dict(pipeline_mode=pl.Buffered(1))
    if seg8:
        tb = tm // SAMPLE_ROWS
        cp_spec = pl.BlockSpec((tb, CARRY_ROWS, 2 * D_FF), lambda i: (i, 0, 0))
        cst_spec = pl.BlockSpec((tb, FFN_CONV - 1, 2 * D_FF), lambda i: (i, 0, 0))
        xp_shape = (tb, CARRY_ROWS + SAMPLE_ROWS, MXU_WIDTH)
    else:
        cp_spec = pl.BlockSpec((1, CARRY_ROWS, 2 * D_FF), lambda i: (i // tiles_per_seq, 0, 0))
        cst_spec = pl.BlockSpec((1, FFN_CONV - 1, 2 * D_FF), lambda i: (i // tiles_per_seq, 0, 0))
        xp_shape = (CARRY_ROWS + tm, MXU_WIDTH)
    kern = functools.partial(_ffn_kernel, seg8, group.last_chunk_valid, tiles_per_seq, final_norm)
    return pl.pallas_call(
        kern,
        grid=(rows // tm,),
        in_specs=[
            pl.BlockSpec((tm, D_MODEL), lambda i: (i, 0)),
            _const_spec((1, D_MODEL)),
            _mod_spec(group, tm, 4),
            _mod_spec(group, tm, 3),
            _mod_spec(group, tm, 5),
            cp_spec,
            pl.BlockSpec((D_MODEL, D_FF), lambda i: (0, 0), **single),
            pl.BlockSpec((D_MODEL, D_FF), lambda i: (0, 0), **single),
            _const_spec((FFN_CONV, 2 * D_FF)),
            _const_spec((1, 2 * D_FF)),
            pl.BlockSpec((D_FF, D_MODEL), lambda i: (0, 0), **single),
            _const_spec((1, D_MODEL)),
        ],
        out_specs=[pl.BlockSpec((tm, D_MODEL), lambda i: (i, 0)), cst_spec],
        out_shape=[
            jax.ShapeDtypeStruct((rows, D_MODEL), F32),
            jax.ShapeDtypeStruct((group.n_seq, FFN_CONV - 1, 2 * D_FF), F32),
        ],
        scratch_shapes=[
            pltpu.VMEM((CARRY_ROWS, 2 * D_FF), F32),
            pltpu.VMEM(xp_shape, F32),
            pltpu.VMEM((tm, D_FF), BF16),
        ],
        compiler_params=_cparams("arbitrary"),
        name="ffn",
    )(x, p["g_norm_ffn"], mod, mod, mod, conv_prev8, p["w_up_a"], p["w_up_g"], p["ffn_conv_w"],
      p["ffn_conv_b"], p["w_down"], g_final)


class _Group:
    def __init__(self, is_prompt, n_seq, seq_len):
        self.is_prompt = is_prompt
        self.n_seq = n_seq
        if is_prompt:
            self.seq_rows = seq_len
            self.chunk_rows = CHUNK
            self.chunks_per_seq = seq_len // CHUNK
            self.last_chunk_valid = CHUNK
            self.per_token_mod = False
            self.tm_inproj = 1024
            self.tm_merge = 512
            self.tm_ffn = 512
            self.attn_q_rows = CHUNK
        else:
            self.attn_q_rows = BF16_SUBLANES
            self.seq_rows = SAMPLE_ROWS
            self.chunk_rows = SAMPLE_ROWS
            self.chunks_per_seq = 1
            self.last_chunk_valid = seq_len
            self.per_token_mod = True
            self.tm_inproj = 512
            self.tm_merge = 512
            self.tm_ffn = 128


def _pad_front(state, rows):
    return jnp.pad(state, ((0, 0), (rows - state.shape[1], 0), (0, 0)))


def _layer_params(i, w):
    pad_heads = lambda v: jnp.pad(v, (0, LANES - SSD_HEADS)).reshape(1, LANES)
    w_in = w["w_in"][i]
    head_of_channel = jnp.arange(SSD_D_INNER) // SSD_HEAD_DIM
    expand = (jnp.arange(LANES)[:, None] == head_of_channel[None, :]).astype(BF16)
    return {
        "g_norm_mix": w["g_norm_mix"][i].reshape(1, D_MODEL),
        "w_main": jnp.concatenate([w_in[:, a:b] for a, b in _W_IN_PIECES], axis=1).astype(BF16),
        "w_dt": jnp.pad(w_in[:, _IN_OFF_DT:_IN_OFF_DT + SSD_HEADS],
                        ((0, 0), (0, LANES - SSD_HEADS))).astype(BF16),
        "ssd_conv_w": w["ssd_conv_w"][i],
        "ssd_conv_b": w["ssd_conv_b"][i].reshape(1, SSD_CONV_DIM),
        "ssd_dt_bias": pad_heads(w["ssd_dt_bias"][i]),
        "ssd_a_log": pad_heads(w["ssd_a_log"][i]),
        "ssd_d": jnp.repeat(w["ssd_d"][i], SSD_HEAD_DIM).reshape(1, SSD_D_INNER),
        "ssd_norm_g": w["ssd_norm_g"][i].reshape(1, SSD_D_INNER),
        "ssd_expand": expand,
        "sc_conv_w": w["sc_conv_w"][i],
        "attn_sinks": jnp.pad(w["attn_sinks"][i], (0, LANES - ATTN_HEADS)).reshape(1, LANES),
        "gm_ln_g": w["gm_ln_g"][i].reshape(1, D_MODEL),
        "gm_ln_b": w["gm_ln_b"][i].reshape(1, D_MODEL),
        "gm_w_s": w["gm_w_s"][i],
        "gm_b_exp": jnp.repeat(w["gm_b_s"][i].T, GM_GROUP_WIDTH, axis=1),
        "w_branch": w["w_branch"][i].astype(BF16),
        "w_o": w["w_o"][i].astype(BF16),
        "g_norm_ffn": w["g_norm_ffn"][i].reshape(1, D_MODEL),
        "w_up_a": w["ffn_w_up"][i][:, :D_FF].astype(BF16),
        "w_up_g": w["ffn_w_up"][i][:, D_FF:].astype(BF16),
        "ffn_conv_w": w["ffn_conv_w"][i],
        "ffn_conv_b": w["ffn_conv_b"][i].reshape(1, 2 * D_FF),
        "w_down": w["ffn_w_down"][i].astype(BF16),
    }


def _run_group(group, x, mod_all, states, params, g_final):
    n = group.n_seq
    outs = []
    for i in range(DEPTH):
        p = params[i]
        mod = mod_all[i]
        ssm0, ssdc0, scc0, k0, v0, ffc0 = (None if s is None else s[i] for s in states)
        if group.is_prompt:
            ssm0 = jnp.zeros((n, SSD_HEADS, SSD_HEAD_DIM, SSD_STATE), F32)
            ssdc0 = jnp.zeros((n, SSD_CONV - 1, SSD_CONV_DIM), F32)
            scc0 = jnp.zeros((n, SC_CONV - 1, D_MODEL), F32)
            ffc0 = jnp.zeros((n, FFN_CONV - 1, 2 * D_FF), F32)
        else:
            k0 = k0.reshape(n, WINDOW, ATTN_KV)
            v0 = v0.reshape(n, WINDOW, ATTN_KV)
        y, dtr = _inproj(group, x, mod, p["g_norm_mix"], p["w_main"], p["w_dt"])
        ya, ssm1, ssdc1 = _ssd(group, y, dtr, ssm0, _pad_front(ssdc0, CARRY_ROWS), p)
        yb, scc1 = _sconv(group, y, _pad_front(scc0, CARRY_ROWS), p)
        yc = _attn(group, y, k0, v0, p)
        yd, v_rows = _gmlp(group, y, p)
        x = _merge(group, x, mod, (ya, yb, yc, yd), y, p)
        x, ffc1 = _ffn(group, x, mod, _pad_front(ffc0, CARRY_ROWS), p, g_final, i == DEPTH - 1)

        kcol = slice(Y_K * SSD_BC, (Y_K + 1) * SSD_BC)
        vcol = slice(Y_V * SSD_BC, (Y_V + 1) * SSD_BC)
        if group.is_prompt:
            y3 = y.reshape(n, group.seq_rows, Y_COLS)
            k1 = y3[:, group.seq_rows - WINDOW:, kcol].astype(F32)
            v1 = y3[:, group.seq_rows - WINDOW:, vcol].astype(F32)
            k1 = k1.reshape(n, WINDOW, ATTN_KV_HEADS, ATTN_HEAD_DIM)
            v1 = v1.reshape(n, WINDOW, ATTN_KV_HEADS, ATTN_HEAD_DIM)
            outs.append((ssm1, ssdc1, scc1, k1, v1, ffc1))
        else:
            l = group.last_chunk_valid
            y3 = y.reshape(n, SAMPLE_ROWS, Y_COLS)
            k1 = jnp.concatenate([k0[:, l:], y3[:, :l, kcol].astype(F32)], axis=1)
            v1 = jnp.concatenate([v0[:, l:], y3[:, :l, vcol].astype(F32)], axis=1)
            k1 = k1.reshape(n, WINDOW, ATTN_KV_HEADS, ATTN_HEAD_DIM)
            v1 = v1.reshape(n, WINDOW, ATTN_KV_HEADS, ATTN_HEAD_DIM)
            gv = v_rows.reshape(n, SAMPLE_ROWS, D_MODEL)[:, :l]
            outs.append((ssm1, ssdc1, scc1, k1, v1, ffc1, gv))
    stacked = tuple(jnp.stack([o[j] for o in outs], axis=0) for j in range(len(outs[0])))
    return x, stacked


def kernel(x_prompt, x_sample, c_prompt, c_sample, state_ssm, state_ssd_conv, state_sc_conv, cache_k,
           cache_v, state_ffn_conv, w_ada, b_ada, g_norm_mix, w_in, ssd_conv_w, ssd_conv_b, ssd_dt_bias,
           ssd_a_log, ssd_d, ssd_norm_g, sc_conv_w, attn_sinks, gm_ln_g, gm_ln_b, gm_w_s, gm_b_s,
           w_branch, w_o, g_norm_ffn, ffn_w_up, ffn_conv_w, ffn_conv_b, ffn_w_down, g_final):
    weights = dict(g_norm_mix=g_norm_mix, w_in=w_in, ssd_conv_w=ssd_conv_w, ssd_conv_b=ssd_conv_b,
                   ssd_dt_bias=ssd_dt_bias, ssd_a_log=ssd_a_log, ssd_d=ssd_d, ssd_norm_g=ssd_norm_g,
                   sc_conv_w=sc_conv_w, attn_sinks=attn_sinks, gm_ln_g=gm_ln_g, gm_ln_b=gm_ln_b,
                   gm_w_s=gm_w_s, gm_b_s=gm_b_s, w_branch=w_branch, w_o=w_o, g_norm_ffn=g_norm_ffn,
                   ffn_w_up=ffn_w_up, ffn_conv_w=ffn_conv_w, ffn_conv_b=ffn_conv_b, ffn_w_down=ffn_w_down)
    params = [_layer_params(i, weights) for i in range(DEPTH)]
    gf = g_final.reshape(1, D_MODEL)

    nb_p, len_p, _ = x_prompt.shape
    nb_s, len_s, _ = x_sample.shape
    assert len_p % CHUNK == 0 and len_s <= SAMPLE_ROWS and len_s >= SSD_CONV - 1
    prompt = _Group(True, nb_p, len_p)
    sample = _Group(False, nb_s, len_s)

    c_rows = nb_p + nb_s
    c_pad = -c_rows % SAMPLE_ROWS
    c_all = jnp.pad(jnp.concatenate([c_prompt, c_sample], axis=0), ((0, c_pad), (0, 0)))
    mod = _ada(c_all, w_ada, b_ada)
    mod_p = mod[:, :nb_p].reshape(DEPTH, nb_p, 1, 6 * D_MODEL)
    mod_s = jnp.repeat(mod[:, nb_p:nb_p + nb_s], SAMPLE_ROWS, axis=1)

    xp = x_prompt.reshape(nb_p * len_p, D_MODEL)
    xs = jnp.pad(x_sample, ((0, 0), (0, SAMPLE_ROWS - len_s), (0, 0))).reshape(nb_s * SAMPLE_ROWS, D_MODEL)

    y_p, st_p = _run_group(prompt, xp, mod_p, (None,) * 6, params, gf)
    y_s, st_s = _run_group(sample, xs, mod_s,
                           (state_ssm, state_ssd_conv, state_sc_conv, cache_k, cache_v, state_ffn_conv),
                           params, gf)
    y_prompt = y_p.reshape(nb_p, len_p, D_MODEL)
    y_sample = y_s.reshape(nb_s, SAMPLE_ROWS, D_MODEL)[:, :len_s]
    return (y_prompt, y_sample) + st_p + st_s
```

```python
import functools
import math

import jax
import jax.numpy as jnp
from jax import lax
from jax.experimental import pallas as pl
from jax.experimental.pallas import tpu as pltpu

F32 = jnp.float32
BF16 = jnp.bfloat16

D_MODEL = 1024
DEPTH = 4
N_BRANCH = 4
SSD_HEADS = 16
SSD_HEAD_DIM = 64
SSD_GROUPS = 4
SSD_STATE = 64
SSD_CONV = 4
SSD_D_INNER = 1024
SSD_BC = SSD_GROUPS * SSD_STATE
SSD_CONV_DIM = SSD_D_INNER + 2 * SSD_BC
SC_CONV = 3
ATTN_HEADS = 16
ATTN_KV_HEADS = 4
ATTN_GROUP = 4
ATTN_HEAD_DIM = 64
ATTN_KV = ATTN_KV_HEADS * ATTN_HEAD_DIM
WINDOW = 128
GM_GROUPS = 8
GM_GROUP_WIDTH = 128
D_FF = 2816
FFN_CONV = 3
EPS = 1e-6

CHUNK = 128
SAMPLE_ROWS = 8
CARRY_ROWS = 8
LANES = 128
BF16_SUBLANES = 16
MXU_WIDTH = 256
VMEM_LIMIT = 56 * 1024 * 1024

(Y_Z, Y_XS, Y_BG, Y_CG, Y_XSC, Y_Q, Y_U, Y_VGM, Y_G0) = range(9)
Y_WIDE = 12
Y_SSDB, Y_SSDC, Y_K, Y_V = (Y_WIDE * 4 + i for i in range(4))
Y_COLS = Y_WIDE * D_MODEL + 4 * SSD_BC
INPROJ_TN = Y_COLS // 4

_IN_OFF_XBC = 1024
_IN_OFF_DT = 2560
_IN_OFF_BCX = 2576
_IN_OFF_Q = 5648
_IN_OFF_K = 6672
_IN_OFF_V = 6928
_IN_OFF_UV = 7184
_IN_OFF_GATES = 9232
_W_IN_PIECES = (
    (0, 1024),
    (_IN_OFF_XBC, _IN_OFF_XBC + 1024),
    (_IN_OFF_BCX, _IN_OFF_BCX + 1024),
    (_IN_OFF_BCX + 1024, _IN_OFF_BCX + 2048),
    (_IN_OFF_BCX + 2048, _IN_OFF_BCX + 3072),
    (_IN_OFF_Q, _IN_OFF_Q + 1024),
    (_IN_OFF_UV, _IN_OFF_UV + 1024),
    (_IN_OFF_UV + 1024, _IN_OFF_UV + 2048),
    (_IN_OFF_GATES, _IN_OFF_GATES + 4096),
    (_IN_OFF_XBC + 1024, _IN_OFF_XBC + 1280),
    (_IN_OFF_XBC + 1280, _IN_OFF_XBC + 1536),
    (_IN_OFF_K, _IN_OFF_K + 256),
    (_IN_OFF_V, _IN_OFF_V + 256),
)


def _cparams(*sem):
    return pltpu.CompilerParams(dimension_semantics=sem, vmem_limit_bytes=VMEM_LIMIT)


def _const_spec(shape):
    zeros = (0,) * len(shape)
    return pl.BlockSpec(shape, lambda *_: zeros)


def _silu(x):
    return x * jax.nn.sigmoid(x)


def _mod_rows(ref):
    return ref[0] if len(ref.shape) == 3 else ref[...]


def _modnorm(x, g, sc, sh):
    r = lax.rsqrt(jnp.mean(x * x, axis=-1, keepdims=True) + EPS)
    return (x * r * g) * (1.0 + sc) + sh


def _split3(x):
    hi = x.astype(BF16)
    r1 = x - hi.astype(F32)
    mid = r1.astype(BF16)
    lo = (r1 - mid.astype(F32)).astype(BF16)
    return hi, mid, lo


def _dot(a, b):
    return jnp.dot(a, b, preferred_element_type=F32)


def _dot_nt(a, b):
    return lax.dot_general(a, b, (((1,), (1,)), ((), ())), preferred_element_type=F32)


def _dot_tn(a, b):
    return lax.dot_general(a, b, (((0,), (0,)), ((), ())), preferred_element_type=F32)


def _dot_exact_lhs(a_bf16, x):
    hi, mid, lo = _split3(x)
    return _dot(a_bf16, hi) + _dot(a_bf16, mid) + _dot(a_bf16, lo)


def _dot_exact_rhs(x, e_bf16):
    hi, mid, lo = _split3(x)
    return _dot(hi, e_bf16) + _dot(mid, e_bf16) + _dot(lo, e_bf16)


def _chunk_rows(ref, rows):
    x = ref[...].astype(F32)
    if x.shape[0] == rows:
        return x
    pad = jnp.zeros((rows - x.shape[0], x.shape[1]), F32)
    return jnp.concatenate([x, pad], axis=0)


def _ada_kernel(c_ref, w_ref, b_ref, o_ref):
    s = _silu(c_ref[...]).astype(BF16)
    o_ref[0] = _dot(s, w_ref[0].astype(BF16)) + b_ref[0]


def _ada(c_all, w_ada, b_ada):
    rows = c_all.shape[0]
    n_tiles = w_ada.shape[-1] // D_MODEL
    return pl.pallas_call(
        _ada_kernel,
        grid=(DEPTH, n_tiles),
        in_specs=[
            _const_spec((rows, D_MODEL)),
            pl.BlockSpec((1, D_MODEL, D_MODEL), lambda l, n: (l, 0, n)),
            pl.BlockSpec((1, 1, D_MODEL), lambda l, n: (l, 0, n)),
        ],
        out_specs=pl.BlockSpec((1, rows, D_MODEL), lambda l, n: (l, 0, n)),
        out_shape=jax.ShapeDtypeStruct((DEPTH, rows, 6 * D_MODEL), F32),
        compiler_params=_cparams("arbitrary", "arbitrary"),
        name="ada",
    )(c_all, w_ada, b_ada.reshape(DEPTH, 1, 6 * D_MODEL))


def _inproj_kernel(x_ref, g_ref, sc_ref, sh_ref, w_ref, wdt_ref, y_ref, dtr_ref, h_scr):
    @pl.when(pl.program_id(1) == 0)
    def _():
        h = _modnorm(x_ref[...], g_ref[...], _mod_rows(sc_ref), _mod_rows(sh_ref)).astype(BF16)
        h_scr[...] = h
        dtr_ref[...] = _dot(h, wdt_ref[...])

    y_ref[...] = _dot(h_scr[...], w_ref[...]).astype(y_ref.dtype)


def _mod_spec(group, tm, piece):
    if group.per_token_mod:
        return pl.BlockSpec((tm, D_MODEL), lambda i, *_: (i, piece))
    tiles_per_seq = group.seq_rows // tm
    return pl.BlockSpec((1, 1, D_MODEL), lambda i, *_: (i // tiles_per_seq, 0, piece))


def _inproj(group, x, mod, g, w_main, w_dt):
    tm = group.tm_inproj
    rows = x.shape[0]
    tn = INPROJ_TN
    return pl.pallas_call(
        _inproj_kernel,
        grid=(rows // tm, Y_COLS // tn),
        in_specs=[
            pl.BlockSpec((tm, D_MODEL), lambda i, n: (i, 0)),
            _const_spec((1, D_MODEL)),
            _mod_spec(group, tm, 1),
            _mod_spec(group, tm, 0),
            pl.BlockSpec((D_MODEL, tn), lambda i, n: (0, n)),
            _const_spec((D_MODEL, LANES)),
        ],
        out_specs=[
            pl.BlockSpec((tm, tn), lambda i, n: (i, n)),
            pl.BlockSpec((tm, LANES), lambda i, n: (i, 0)),
        ],
        out_shape=[
            jax.ShapeDtypeStruct((rows, Y_COLS), BF16),
            jax.ShapeDtypeStruct((rows, LANES), F32),
        ],
        scratch_shapes=[pltpu.VMEM((tm, D_MODEL), BF16)],
        compiler_params=_cparams("arbitrary", "arbitrary"),
        name="inproj",
    )(x, g, mod, mod, w_main, w_dt)


def _causal_conv(xp_ref, w_ref, width, rows):
    first = CARRY_ROWS - (width - 1)
    acc = w_ref[0:1, :] * xp_ref[pl.ds(first, rows), :]
    for j in range(1, width):
        acc = acc + w_ref[j:j + 1, :] * xp_ref[pl.ds(first + j, rows), :]
    return acc


def _chunk_spec(group, width, piece):
    nc = group.chunks_per_seq
    return pl.BlockSpec((group.chunk_rows, width), lambda b, c: (b * nc + c, piece))


HEADS_PER_GROUP = SSD_HEADS // SSD_GROUPS
GROUP_WIDTH = HEADS_PER_GROUP * SSD_HEAD_DIM


def _softplus(x):
    return jnp.maximum(x, 0.0) + jnp.log1p(jnp.exp(-jnp.abs(x)))


def _ssd_intra_group(g, cs, cs_t, cmat, bmat, x_b, causal):
    T = cs.shape[0]
    gs = slice(g * SSD_STATE, (g + 1) * SSD_STATE)
    cb = _dot_nt(cmat[:, gs], bmat[:, gs])
    low_half = lax.broadcasted_iota(jnp.int32, (T, 2 * SSD_HEAD_DIM), 1) < SSD_HEAD_DIM
    pair_out = []
    for pr in range(HEADS_PER_GROUP // 2):
        h0 = g * HEADS_PER_GROUP + 2 * pr
        xp = x_b[:, h0 * SSD_HEAD_DIM:(h0 + 2) * SSD_HEAD_DIM]
        acc = None
        for k in range(2):
            h = h0 + k
            diff = cs[:, h:h + 1] - cs_t[h:h + 1, :]
            lmat = jnp.where(causal, jnp.exp(jnp.where(causal, diff, 0.0)), 0.0)
            m = (cb * lmat).astype(BF16)
            keep = low_half if k == 0 else jnp.logical_not(low_half)
            part = _dot(m, jnp.where(keep, xp, jnp.zeros_like(xp)))
            acc = part if acc is None else acc + part
        pair_out.append(acc)
    return jnp.concatenate(pair_out, axis=1)


def _ssd_finish(y, xs, z, dsk, ng):
    y = (y + xs * dsk) * _silu(z)
    gw = SSD_D_INNER // SSD_GROUPS
    normed = []
    for g in range(SSD_GROUPS):
        yg = y[:, g * gw:(g + 1) * gw]
        normed.append(yg * lax.rsqrt(jnp.mean(yg * yg, axis=-1, keepdims=True) + EPS))
    return jnp.concatenate(normed, axis=1) * ng


def _seq_pos(rows, width):
    return lax.broadcasted_iota(jnp.int32, (rows, width), 0) % SAMPLE_ROWS


def _shift_rows(x, d, head, pos):
    return jnp.where(pos >= d, pltpu.roll(x, d, 0), head)


def _ssd_s_kernel(l_valid, z_ref, xs_ref, bm_ref, cm_ref, dtr_ref, s0_ref, h1_ref, h2_ref, h3_ref,
                  cw_ref, cb_ref, dtb_ref, alog_ref, dsk_ref, ng_ref, e_ref,
                  y_ref, sfin_ref, raw_ref):
    T = CHUNK
    n_seq = T // SAMPLE_ROWS
    raw = jnp.concatenate([xs_ref[...].astype(F32), bm_ref[...].astype(F32),
                           cm_ref[...].astype(F32)], axis=1)
    raw_ref[...] = raw
    pos = _seq_pos(T, SSD_CONV_DIM)
    conv = (cb_ref[...] + cw_ref[3:4, :] * raw
            + cw_ref[2:3, :] * _shift_rows(raw, 1, h1_ref[...], pos)
            + cw_ref[1:2, :] * _shift_rows(raw, 2, h2_ref[...], pos)
            + cw_ref[0:1, :] * _shift_rows(raw, 3, h3_ref[...], pos))
    xbc = _silu(conv)
    xs = xbc[:, 0:SSD_D_INNER]
    bmat = xbc[:, SSD_D_INNER:SSD_D_INNER + SSD_BC].astype(BF16)
    cmat = xbc[:, SSD_D_INNER + SSD_BC:SSD_CONV_DIM].astype(BF16)

    dt = _softplus(dtr_ref[...] + dtb_ref[...])
    dt = jnp.where(_seq_pos(T, LANES) < l_valid, dt, 0.0)
    dta = dt * (-jnp.exp(alog_ref[...]))
    row = lax.broadcasted_iota(jnp.int32, (T, T), 0)
    col = lax.broadcasted_iota(jnp.int32, (T, T), 1)
    causal = (row >= col) & (row // SAMPLE_ROWS == col // SAMPLE_ROWS)
    tri = jnp.where(causal, 1.0, 0.0).astype(BF16)
    cs = _dot_exact_lhs(tri, dta)
    last = jnp.where(col == (row // SAMPLE_ROWS) * SAMPLE_ROWS + (SAMPLE_ROWS - 1), 1.0, 0.0).astype(BF16)
    total_rows = _dot_exact_lhs(last, cs)
    cs_t = cs.T
    e = e_ref[...]
    dt_x = _dot_exact_rhs(dt, e)
    ecs_x = _dot_exact_rhs(jnp.exp(cs), e)
    dec_x = _dot_exact_rhs(jnp.exp(total_rows - cs), e)
    x_dt = xs * dt_x
    x_b = x_dt.astype(BF16)
    x_dec_t = (x_dt * dec_x).T.astype(BF16)

    seq_of_row_wide = lax.broadcasted_iota(jnp.int32, (T, GROUP_WIDTH), 0) // SAMPLE_ROWS
    seq_of_row = lax.broadcasted_iota(jnp.int32, (T, SSD_STATE), 0) // SAMPLE_ROWS
    y_parts = []
    for g in range(SSD_GROUPS):
        gs = slice(g * SSD_STATE, (g + 1) * SSD_STATE)
        c_g, b_g = cmat[:, gs], bmat[:, gs]
        xt_g = x_dec_t[g * GROUP_WIDTH:(g + 1) * GROUP_WIDTH, :]
        y_off = jnp.zeros((T, GROUP_WIDTH), F32)
        for b in range(n_seq):
            s_bg = s0_ref[b, g * HEADS_PER_GROUP:(g + 1) * HEADS_PER_GROUP].reshape(GROUP_WIDTH, SSD_STATE)
            y_off = jnp.where(seq_of_row_wide == b, _dot_nt(c_g, s_bg.astype(BF16)), y_off)
            ds = _dot(xt_g, jnp.where(seq_of_row == b, b_g, jnp.zeros_like(b_g)))
            decay = jnp.exp(cs[b * SAMPLE_ROWS + SAMPLE_ROWS - 1:(b + 1) * SAMPLE_ROWS, :])
            for r in range(HEADS_PER_GROUP):
                h = g * HEADS_PER_GROUP + r
                sfin_ref[b, h] = (s0_ref[b, h] * decay[:, h:h + 1]
                                  + ds[r * SSD_HEAD_DIM:(r + 1) * SSD_HEAD_DIM, :])
        y_diag = _ssd_intra_group(g, cs, cs_t, cmat, bmat, x_b, causal)
        y_parts.append(y_diag + y_off * ecs_x[:, g * GROUP_WIDTH:(g + 1) * GROUP_WIDTH])
    y = _ssd_finish(jnp.concatenate(y_parts, axis=1), xs, z_ref[...].astype(F32), dsk_ref[...], ng_ref[...])
    y_ref[...] = y.astype(y_ref.dtype)


def _history_tiles(state, width):
    n, k, c = state.shape
    tiles = []
    for d in range(1, width):
        rows = [state[:, k + t - d] if t < d else jnp.zeros((n, c), state.dtype)
                for t in range(SAMPLE_ROWS)]
        tiles.append(jnp.stack(rows, axis=1).reshape(n * SAMPLE_ROWS, c))
    return tiles


def _ssd_s(group, y, dtr, s0, conv_state, p):
    rows = y.shape[0]
    n_seq = CHUNK // SAMPLE_ROWS
    tok = lambda width, piece: pl.BlockSpec((CHUNK, width), lambda i: (i, piece))
    state_spec = pl.BlockSpec((n_seq, SSD_HEADS, SSD_HEAD_DIM, SSD_STATE), lambda i: (i, 0, 0, 0))
    h1, h2, h3 = _history_tiles(conv_state, SSD_CONV)
    ya, sfin, raw = pl.pallas_call(
        functools.partial(_ssd_s_kernel, group.last_chunk_valid),
        grid=(rows // CHUNK,),
        in_specs=[
            tok(D_MODEL, Y_Z), tok(D_MODEL, Y_XS), tok(SSD_BC, Y_SSDB), tok(SSD_BC, Y_SSDC),
            tok(LANES, 0), state_spec,
            tok(SSD_CONV_DIM, 0), tok(SSD_CONV_DIM, 0), tok(SSD_CONV_DIM, 0),
            _const_spec((SSD_CONV, SSD_CONV_DIM)), _const_spec((1, SSD_CONV_DIM)),
            _const_spec((1, LANES)), _const_spec((1, LANES)),
            _const_spec((1, SSD_D_INNER)), _const_spec((1, SSD_D_INNER)),
            _const_spec((LANES, SSD_D_INNER)),
        ],
        out_specs=[tok(D_MODEL, 0), state_spec, tok(SSD_CONV_DIM, 0)],
        out_shape=[
            jax.ShapeDtypeStruct((rows, D_MODEL), BF16),
            jax.ShapeDtypeStruct(s0.shape, F32),
            jax.ShapeDtypeStruct((rows, SSD_CONV_DIM), F32),
        ],
        compiler_params=_cparams("arbitrary"),
        name="ssd_s",
    )(y, y, y, y, dtr, s0, h1, h2, h3,
      p["ssd_conv_w"], p["ssd_conv_b"], p["ssd_dt_bias"], p["ssd_a_log"], p["ssd_d"],
      p["ssd_norm_g"], p["ssd_expand"])
    l = group.last_chunk_valid
    conv_new = raw.reshape(group.n_seq, SAMPLE_ROWS, SSD_CONV_DIM)[:, l - (SSD_CONV - 1):l]
    return ya, sfin, conv_new


def _ssd_kernel(l_valid, n_chunks,
                z_ref, xs_ref, bm_ref, cm_ref, dtr_ref, s0_ref, cp_ref,
                cw_ref, cb_ref, dtb_ref, alog_ref, dsk_ref, ng_ref, e_ref,
                y_ref, sfin_ref, cst_ref,
                s_scr, xp_scr):
    T = CHUNK
    c = pl.program_id(1)

    @pl.when(c == 0)
    def _():
        s_scr[...] = s0_ref[0]
        xp_scr[0:CARRY_ROWS, :] = cp_ref[0]

    xp_scr[CARRY_ROWS:CARRY_ROWS + T, 0:SSD_D_INNER] = _chunk_rows(xs_ref, T)
    xp_scr[CARRY_ROWS:CARRY_ROWS + T, SSD_D_INNER:SSD_D_INNER + SSD_BC] = _chunk_rows(bm_ref, T)
    xp_scr[CARRY_ROWS:CARRY_ROWS + T, SSD_D_INNER + SSD_BC:SSD_CONV_DIM] = _chunk_rows(cm_ref, T)
    xbc = _silu(_causal_conv(xp_scr, cw_ref, SSD_CONV, T) + cb_ref[...])
    xs = xbc[:, 0:SSD_D_INNER]
    bmat = xbc[:, SSD_D_INNER:SSD_D_INNER + SSD_BC].astype(BF16)
    cmat = xbc[:, SSD_D_INNER + SSD_BC:SSD_CONV_DIM].astype(BF16)

    dt = _softplus(_chunk_rows(dtr_ref, T) + dtb_ref[...])
    row = lax.broadcasted_iota(jnp.int32, (T, T), 0)
    col = lax.broadcasted_iota(jnp.int32, (T, T), 1)
    if l_valid < T:
        rvalid = lax.broadcasted_iota(jnp.int32, (T, LANES), 0) < l_valid
        dt = jnp.where(rvalid, dt, 0.0)
    dta = dt * (-jnp.exp(alog_ref[...]))
    causal = row >= col
    tri = jnp.where(causal, 1.0, 0.0).astype(BF16)
    cs = _dot_exact_lhs(tri, dta)
    cs_t = cs.T
    total = cs[T - 1:T, :]
    e = e_ref[...]
    dt_x = _dot_exact_rhs(dt, e)
    ecs_x = _dot_exact_rhs(jnp.exp(cs), e)
    dec_x = _dot_exact_rhs(jnp.exp(total - cs), e)
    chunk_decay = jnp.exp(total)

    x_dt = xs * dt_x
    x_b = x_dt.astype(BF16)
    x_dec = (x_dt * dec_x).astype(BF16)

    y_parts = []
    for g in range(SSD_GROUPS):
        gs = slice(g * SSD_STATE, (g + 1) * SSD_STATE)
        s_g = s_scr[g * HEADS_PER_GROUP:(g + 1) * HEADS_PER_GROUP].reshape(GROUP_WIDTH, SSD_STATE)
        y_diag = _ssd_intra_group(g, cs, cs_t, cmat, bmat, x_b, causal)
        y_off = _dot_nt(cmat[:, gs], s_g.astype(BF16))
        ds_g = _dot_tn(x_dec[:, g * GROUP_WIDTH:(g + 1) * GROUP_WIDTH], bmat[:, gs])
        y_parts.append(y_diag + y_off * ecs_x[:, g * GROUP_WIDTH:(g + 1) * GROUP_WIDTH])
        for r in range(HEADS_PER_GROUP):
            h = g * HEADS_PER_GROUP + r
            s_scr[h] = (s_scr[h] * chunk_decay[:, h:h + 1]
                        + ds_g[r * SSD_HEAD_DIM:(r + 1) * SSD_HEAD_DIM, :])

    y = _ssd_finish(jnp.concatenate(y_parts, axis=1), xs, _chunk_rows(z_ref, T), dsk_ref[...], ng_ref[...])
    y_ref[...] = y[0:y_ref.shape[0], :].astype(y_ref.dtype)

    @pl.when(c == n_chunks - 1)
    def _():
        sfin_ref[0] = s_scr[...]
        cst_ref[0] = xp_scr[pl.ds(CARRY_ROWS + l_valid - (SSD_CONV - 1), SSD_CONV - 1), :]

    if n_chunks > 1:
        xp_scr[0:CARRY_ROWS, :] = xp_scr[T:T + CARRY_ROWS, :]


def _ssd(group, y, dtr, s0, conv_prev8, p):
    nb, nc = group.n_seq, group.chunks_per_seq
    rows = y.shape[0]
    kern = functools.partial(_ssd_kernel, group.last_chunk_valid, nc)
    return pl.pallas_call(
        kern,
        grid=(nb, nc),
        in_specs=[
            _chunk_spec(group, D_MODEL, Y_Z),
            _chunk_spec(group, D_MODEL, Y_XS),
            _chunk_spec(group, SSD_BC, Y_SSDB),
            _chunk_spec(group, SSD_BC, Y_SSDC),
            _chunk_spec(group, LANES, 0),
            pl.BlockSpec((1, SSD_HEADS, SSD_HEAD_DIM, SSD_STATE), lambda b, c: (b, 0, 0, 0)),
            pl.BlockSpec((1, CARRY_ROWS, SSD_CONV_DIM), lambda b, c: (b, 0, 0)),
            _const_spec((SSD_CONV, SSD_CONV_DIM)),
            _const_spec((1, SSD_CONV_DIM)),
            _const_spec((1, LANES)),
            _const_spec((1, LANES)),
            _const_spec((1, SSD_D_INNER)),
            _const_spec((1, SSD_D_INNER)),
            _const_spec((LANES, SSD_D_INNER)),
        ],
        out_specs=[
            _chunk_spec(group, D_MODEL, 0),
            pl.BlockSpec((1, SSD_HEADS, SSD_HEAD_DIM, SSD_STATE), lambda b, c: (b, 0, 0, 0)),
            pl.BlockSpec((1, SSD_CONV - 1, SSD_CONV_DIM), lambda b, c: (b, 0, 0)),
        ],
        out_shape=[
            jax.ShapeDtypeStruct((rows, D_MODEL), BF16),
            jax.ShapeDtypeStruct((nb, SSD_HEADS, SSD_HEAD_DIM, SSD_STATE), F32),
            jax.ShapeDtypeStruct((nb, SSD_CONV - 1, SSD_CONV_DIM), F32),
        ],
        scratch_shapes=[
            pltpu.VMEM((SSD_HEADS, SSD_HEAD_DIM, SSD_STATE), F32),
            pltpu.VMEM((CARRY_ROWS + CHUNK, SSD_CONV_DIM), F32),
        ],
        compiler_params=_cparams("arbitrary", "arbitrary"),
        name="ssd",
    )(y, y, y, y, dtr, s0, conv_prev8,
      p["ssd_conv_w"], p["ssd_conv_b"], p["ssd_dt_bias"], p["ssd_a_log"], p["ssd_d"],
      p["ssd_norm_g"], p["ssd_expand"])


def _sconv_kernel(l_valid, n_chunks, bg_ref, cg_ref, xs_ref, cp_ref, cw_ref, y_ref, cst_ref, xp_scr):
    T = CHUNK
    c = pl.program_id(1)

    @pl.when(c == 0)
    def _():
        xp_scr[0:CARRY_ROWS, :] = cp_ref[0]

    xp_scr[CARRY_ROWS:CARRY_ROWS + T, :] = _chunk_rows(cg_ref, T) * _chunk_rows(xs_ref, T)
    y = _chunk_rows(bg_ref, T) * _causal_conv(xp_scr, cw_ref, SC_CONV, T)
    y_ref[...] = y[0:y_ref.shape[0], :].astype(y_ref.dtype)

    @pl.when(c == n_chunks - 1)
    def _():
        cst_ref[0] = xp_scr[pl.ds(CARRY_ROWS + l_valid - (SC_CONV - 1), SC_CONV - 1), :]

    if n_chunks > 1:
        xp_scr[0:CARRY_ROWS, :] = xp_scr[T:T + CARRY_ROWS, :]


def _sconv(group, y, conv_prev8, p):
    nb, nc = group.n_seq, group.chunks_per_seq
    rows = y.shape[0]
    kern = functools.partial(_sconv_kernel, group.last_chunk_valid, nc)
    return pl.pallas_call(
        kern,
        grid=(nb, nc),
        in_specs=[
            _chunk_spec(group, D_MODEL, Y_BG),
            _chunk_spec(group, D_MODEL, Y_CG),
            _chunk_spec(group, D_MODEL, Y_XSC),
            pl.BlockSpec((1, CARRY_ROWS, D_MODEL), lambda b, c: (b, 0, 0)),
            _const_spec((SC_CONV, D_MODEL)),
        ],
        out_specs=[
            _chunk_spec(group, D_MODEL, 0),
            pl.BlockSpec((1, SC_CONV - 1, D_MODEL), lambda b, c: (b, 0, 0)),
        ],
        out_shape=[
            jax.ShapeDtypeStruct((rows, D_MODEL), BF16),
            jax.ShapeDtypeStruct((nb, SC_CONV - 1, D_MODEL), F32),
        ],
        scratch_shapes=[pltpu.VMEM((CARRY_ROWS + CHUNK, D_MODEL), F32)],
        compiler_params=_cparams("arbitrary", "arbitrary"),
        name="sconv",
    )(y, y, y, conv_prev8, p["sc_conv_w"])


def _sconv_s_kernel(bg_ref, cg_ref, xs_ref, h1_ref, h2_ref, cw_ref, y_ref, u_ref):
    u = cg_ref[...].astype(F32) * xs_ref[...].astype(F32)
    u_ref[...] = u
    pos = _seq_pos(*u.shape)
    conv = (cw_ref[2:3, :] * u + cw_ref[1:2, :] * _shift_rows(u, 1, h1_ref[...], pos)
            + cw_ref[0:1, :] * _shift_rows(u, 2, h2_ref[...], pos))
    y_ref[...] = (bg_ref[...].astype(F32) * conv).astype(y_ref.dtype)


def _sconv_s(group, y, conv_state, p):
    rows = y.shape[0]
    tm = group.tm_rowwise
    tok = lambda piece: pl.BlockSpec((tm, D_MODEL), lambda i: (i, piece))
    h1, h2 = _history_tiles(conv_state, SC_CONV)
    yb, u = pl.pallas_call(
        _sconv_s_kernel,
        grid=(rows // tm,),
        in_specs=[tok(Y_BG), tok(Y_CG), tok(Y_XSC), tok(0), tok(0), _const_spec((SC_CONV, D_MODEL))],
        out_specs=[tok(0), tok(0)],
        out_shape=[jax.ShapeDtypeStruct((rows, D_MODEL), BF16), jax.ShapeDtypeStruct((rows, D_MODEL), F32)],
        compiler_params=_cparams("arbitrary"),
        name="sconv_s",
    )(y, y, y, h1, h2, p["sc_conv_w"])
    l = group.last_chunk_valid
    return yb, u.reshape(group.n_seq, SAMPLE_ROWS, D_MODEL)[:, l - (SC_CONV - 1):l]


def _spread_kv(k_rows, v_rows):
    kf = k_rows.astype(F32)
    vf = v_rows.astype(F32)
    blk = lax.broadcasted_iota(jnp.int32, kf.shape, 1) // ATTN_HEAD_DIM
    kspread, vplaced = [], []
    for g in range(ATTN_KV_HEADS):
        tk = jnp.where(blk == g, kf, 0.0)
        uk = tk + pltpu.roll(tk, ATTN_HEAD_DIM, 1)
        kspread.append((uk + pltpu.roll(uk, 2 * ATTN_HEAD_DIM, 1)).astype(BF16))
        tv = jnp.where(blk == g, vf, 0.0)
        tv1 = pltpu.roll(tv, ATTN_HEAD_DIM, 1)
        shifted = (tv, tv1, pltpu.roll(tv, 2 * ATTN_HEAD_DIM, 1), pltpu.roll(tv1, 2 * ATTN_HEAD_DIM, 1))
        vplaced.append([shifted[(r - g) % ATTN_KV_HEADS].astype(BF16) for r in range(ATTN_GROUP)])
    return kspread, vplaced


def _attn_kernel(tq_rows, carry, *refs):
    if carry:
        q_ref, k_ref, v_ref, sink_ref, bias_ref, y_ref, kprev_scr, vprev_scr = refs
    else:
        q_ref, k_ref, v_ref, kp_ref, vp_ref, sink_ref, bias_ref, y_ref = refs
    T = tq_rows
    c = pl.program_id(1)
    scale = ATTN_HEAD_DIM ** -0.5
    if q_ref.shape[0] == T:
        q = q_ref[...] * jnp.asarray(scale, q_ref.dtype)
    else:
        q = (_chunk_rows(q_ref, T) * scale).astype(BF16)
    if k_ref.shape[0] == WINDOW:
        kc, vc = k_ref[...], v_ref[...]
    else:
        kc = _chunk_rows(k_ref, WINDOW).astype(BF16)
        vc = _chunk_rows(v_ref, WINDOW).astype(BF16)
    ks_cur, vp_cur = _spread_kv(kc, vc)
    if carry:
        read_slot = (c + 1) % 2
        write_slot = c % 2

        @pl.when(c == 0)
        def _():
            kprev_scr[read_slot] = jnp.zeros(kprev_scr.shape[1:], BF16)
            vprev_scr[read_slot] = jnp.zeros(vprev_scr.shape[1:], BF16)
        ks_prev = [kprev_scr[read_slot, g] for g in range(ATTN_KV_HEADS)]
        vp_prev = [[vprev_scr[read_slot, g * ATTN_GROUP + r] for r in range(ATTN_GROUP)]
                   for g in range(ATTN_KV_HEADS)]
    else:
        ks_prev, vp_prev = _spread_kv(kp_ref[0].astype(BF16), vp_ref[0].astype(BF16))

    q_blk = lax.broadcasted_iota(jnp.int32, (T, ATTN_KV), 1) // ATTN_HEAD_DIM
    scores = []
    for g in range(ATTN_KV_HEADS):
        qg = q[:, g * ATTN_KV:(g + 1) * ATTN_KV]
        q4 = jnp.concatenate([jnp.where(q_blk == r, qg, jnp.zeros_like(qg))
                              for r in range(ATTN_GROUP)], axis=0)
        keys = jnp.concatenate([ks_prev[g], ks_cur[g]], axis=0)
        scores.append(_dot_nt(q4, keys))
    groups = range(ATTN_KV_HEADS)
    bias = [bias_ref[0, g * ATTN_GROUP:(g + 1) * ATTN_GROUP].reshape(ATTN_GROUP * T, 2 * WINDOW)
            for g in groups]
    sinks = [jnp.concatenate([jnp.broadcast_to(sink_ref[:, h:h + 1], (T, 1))
                              for h in range(g * ATTN_GROUP, (g + 1) * ATTN_GROUP)], axis=0)
             for g in groups]
    s = [scores[g] - bias[g] for g in groups]
    m = [jnp.maximum(jnp.max(s[g], axis=-1, keepdims=True), sinks[g]) for g in groups]
    e = [jnp.exp(s[g] - m[g]) for g in groups]
    den = [jnp.sum(e[g], axis=-1, keepdims=True) + jnp.exp(sinks[g] - m[g]) for g in groups]
    pn = [(e[g] * (1.0 / den[g])).astype(BF16) for g in groups]
    probs = [jnp.concatenate([pn[g][r * T:(r + 1) * T] for r in range(ATTN_GROUP)], axis=1)
             for g in groups]
    outs = []
    for g in range(ATTN_KV_HEADS):
        vals = jnp.concatenate([blk for r in range(ATTN_GROUP)
                                for blk in (vp_prev[g][r], vp_cur[g][r])], axis=0)
        outs.append(_dot(probs[g], vals))
    y = jnp.concatenate(outs, axis=1)
    y_ref[...] = y[0:y_ref.shape[0], :].astype(y_ref.dtype)
    if carry:
        for g in range(ATTN_KV_HEADS):
            kprev_scr[write_slot, g] = ks_cur[g]
            for r in range(ATTN_GROUP):
                vprev_scr[write_slot, g * ATTN_GROUP + r] = vp_cur[g][r]


def _attn_bias_table(q_rows):
    t = jnp.arange(q_rows)[:, None]
    j = jnp.arange(2 * WINDOW)[None, :]
    dist = WINDOW + t - j
    band = (dist >= 0) & (dist <= WINDOW)
    slopes = 2.0 ** (-8.0 * jnp.arange(1, ATTN_HEADS + 1, dtype=F32) / ATTN_HEADS)
    bias = slopes[:, None, None] * dist.astype(F32)[None]
    first = jnp.where(band & (j >= WINDOW), bias, 1e30)
    later = jnp.where(band, bias, 1e30)
    return jnp.stack([first, later], axis=0)


def _attn(group, y, k_prev, v_prev, p):
    nb, nc = group.n_seq, group.chunks_per_seq
    rows = y.shape[0]
    tq = group.attn_q_rows
    carry = group.is_prompt
    in_specs = [
        _chunk_spec(group, D_MODEL, Y_Q),
        _chunk_spec(group, ATTN_KV, Y_K),
        _chunk_spec(group, ATTN_KV, Y_V),
    ]
    args = [y, y, y]
    scratch = []
    if carry:
        bias_spec = pl.BlockSpec((1, ATTN_HEADS, tq, 2 * WINDOW),
                                 lambda b, c: (jnp.minimum(c, 1), 0, 0, 0))
        scratch = [pltpu.VMEM((2, ATTN_KV_HEADS, WINDOW, ATTN_KV), BF16),
                   pltpu.VMEM((2, ATTN_HEADS, WINDOW, ATTN_KV), BF16)]
    else:
        bias_spec = pl.BlockSpec((1, ATTN_HEADS, tq, 2 * WINDOW), lambda b, c: (1, 0, 0, 0))
        in_specs += [pl.BlockSpec((1, WINDOW, ATTN_KV), lambda b, c: (b, 0, 0))] * 2
        args += [k_prev, v_prev]
    in_specs += [_const_spec((1, LANES)), bias_spec]
    args += [p["attn_sinks"], _attn_bias_table(tq)]
    return pl.pallas_call(
        functools.partial(_attn_kernel, tq, carry),
        grid=(nb, nc),
        in_specs=in_specs,
        out_specs=_chunk_spec(group, D_MODEL, 0),
        out_shape=jax.ShapeDtypeStruct((rows, D_MODEL), BF16),
        scratch_shapes=scratch,
        compiler_params=_cparams("arbitrary", "arbitrary"),
        name="attn_p" if group.is_prompt else "attn_s",
    )(*args)


def _gelu_tanh(x):
    return 0.5 * x * (1.0 + jnp.tanh(math.sqrt(2.0 / math.pi) * (x + 0.044715 * (x * x * x))))


def _gmlp_kernel(u_ref, v_ref, lg_ref, lb_ref, ws_ref, bs_ref, y_ref, *maybe_v_out):
    T = CHUNK
    u = _gelu_tanh(_chunk_rows(u_ref, T))
    v = _gelu_tanh(_chunk_rows(v_ref, T))
    vc = v - jnp.mean(v, axis=-1, keepdims=True)
    v = vc * lax.rsqrt(jnp.mean(vc * vc, axis=-1, keepdims=True) + EPS) * lg_ref[...] + lb_ref[...]
    row = lax.broadcasted_iota(jnp.int32, (T, T), 0)
    col = lax.broadcasted_iota(jnp.int32, (T, T), 1)
    causal = row >= col
    vb = v.astype(BF16)
    mixed = []
    for g in range(GM_GROUPS):
        w = jnp.where(causal, ws_ref[g], 0.0).astype(BF16)
        gs = slice(g * GM_GROUP_WIDTH, (g + 1) * GM_GROUP_WIDTH)
        mixed.append(_dot(w, vb[:, gs]) + bs_ref[:, gs])
    y = u * jnp.concatenate(mixed, axis=1)
    y_ref[...] = y[0:y_ref.shape[0], :].astype(y_ref.dtype)
    if maybe_v_out:
        v_out = maybe_v_out[0]
        v_out[...] = v[0:v_out.shape[0], :]


def _gmlp(group, y, p):
    nb, nc = group.n_seq, group.chunks_per_seq
    rows = y.shape[0]
    out_specs = [_chunk_spec(group, D_MODEL, 0)]
    out_shape = [jax.ShapeDtypeStruct((rows, D_MODEL), BF16)]
    if not group.is_prompt:
        out_specs.append(_chunk_spec(group, D_MODEL, 0))
        out_shape.append(jax.ShapeDtypeStruct((rows, D_MODEL), F32))
    res = pl.pallas_call(
        _gmlp_kernel,
        grid=(nb, nc),
        in_specs=[
            _chunk_spec(group, D_MODEL, Y_U),
            _chunk_spec(group, D_MODEL, Y_VGM),
            _const_spec((1, D_MODEL)),
            _const_spec((1, D_MODEL)),
            _const_spec((GM_GROUPS, CHUNK, CHUNK)),
            _const_spec((CHUNK, D_MODEL)),
        ],
        out_specs=out_specs,
        out_shape=out_shape,
        compiler_params=_cparams("arbitrary", "arbitrary"),
        name="gmlp",
    )(y, y, p["gm_ln_g"], p["gm_ln_b"], p["gm_w_s"], p["gm_b_exp"])
    return (res[0], res[1]) if not group.is_prompt else (res[0], None)


def _gmlp_s_kernel(n_diag, u_ref, v_ref, lg_ref, lb_ref, wd_ref, bs_ref, y_ref, v_out):
    u = _gelu_tanh(u_ref[...].astype(F32))
    v = _gelu_tanh(v_ref[...].astype(F32))
    vc = v - jnp.mean(v, axis=-1, keepdims=True)
    v = vc * lax.rsqrt(jnp.mean(vc * vc, axis=-1, keepdims=True) + EPS) * lg_ref[...] + lb_ref[...]
    mixed = bs_ref[...] + wd_ref[0] * v
    for d in range(1, n_diag):
        mixed = mixed + wd_ref[d] * pltpu.roll(v, d, 0)
    y_ref[...] = (u * mixed).astype(y_ref.dtype)
    v_out[...] = v


def _gmlp_s_tables(group, w_s, b_s):
    l = group.last_chunk_valid
    reps = group.tm_rowwise // SAMPLE_ROWS
    t = jnp.arange(SAMPLE_ROWS)
    expand = lambda per_group: jnp.tile(jnp.repeat(per_group.T, GM_GROUP_WIDTH, axis=1), (reps, 1))
    diags = []
    for d in range(l):
        src = t - d
        w = w_s[:, t, jnp.clip(src, 0, None)]
        diags.append(expand(jnp.where((src >= 0) & (t < l), w, 0.0)))
    return jnp.stack(diags, axis=0), expand(b_s[:, :SAMPLE_ROWS])


def _gmlp_s(group, y, p):
    rows = y.shape[0]
    tm = group.tm_rowwise
    l = group.last_chunk_valid
    tok = lambda piece: pl.BlockSpec((tm, D_MODEL), lambda i: (i, piece))
    wd, bs = _gmlp_s_tables(group, p["gm_w_s"], p["gm_b_s"])
    return pl.pallas_call(
        functools.partial(_gmlp_s_kernel, l),
        grid=(rows // tm,),
        in_specs=[tok(Y_U), tok(Y_VGM), _const_spec((1, D_MODEL)), _const_spec((1, D_MODEL)),
                  _const_spec((l, tm, D_MODEL)), _const_spec((tm, D_MODEL))],
        out_specs=[tok(0), tok(0)],
        out_shape=[jax.ShapeDtypeStruct((rows, D_MODEL), BF16), jax.ShapeDtypeStruct((rows, D_MODEL), F32)],
        compiler_params=_cparams("arbitrary"),
        name="gmlp_s",
    )(y, y, p["gm_ln_g"], p["gm_ln_b"], wd, bs)


def _merge_kernel(x_ref, ga_ref, ya_ref, yb_ref, yc_ref, yd_ref, g0_ref, g1_ref, g2_ref, g3_ref,
                  wb_ref, wo_ref, o_ref):
    merged = None
    for i, (y_ref, g_ref) in enumerate(((ya_ref, g0_ref), (yb_ref, g1_ref), (yc_ref, g2_ref),
                                        (yd_ref, g3_ref))):
        gate = jax.nn.sigmoid(g_ref[...].astype(F32))
        term = gate * _dot(y_ref[...], wb_ref[i])
        merged = term if merged is None else merged + term
    o = _dot(merged.astype(BF16), wo_ref[...])
    o_ref[...] = x_ref[...] + _mod_rows(ga_ref) * o


def _merge(group, x, mod, branches, y, p):
    tm = group.tm_merge
    rows = x.shape[0]
    tok = lambda piece: pl.BlockSpec((tm, D_MODEL), lambda i: (i, piece))
    return pl.pallas_call(
        _merge_kernel,
        grid=(rows // tm,),
        in_specs=[tok(0), _mod_spec(group, tm, 2)] + [tok(0)] * 4 + [tok(Y_G0 + i) for i in range(4)] + [
            pl.BlockSpec((N_BRANCH, D_MODEL, D_MODEL), lambda i: (0, 0, 0), pipeline_mode=pl.Buffered(1)),
            pl.BlockSpec((D_MODEL, D_MODEL), lambda i: (0, 0), pipeline_mode=pl.Buffered(1)),
        ],
        out_specs=tok(0),
        out_shape=jax.ShapeDtypeStruct((rows, D_MODEL), F32),
        compiler_params=_cparams("arbitrary"),
        name="merge",
    )(x, mod, *branches, y, y, y, y, p["w_branch"], p["w_o"])


def _ffn_kernel(seg8, l_valid, tiles_per_seq, final_norm,
                x_ref, g_ref, sc_ref, sh_ref, ga_ref, cp_ref, wa_ref, wg_ref, cw_ref, cb_ref, wd_ref,
                gf_ref, o_ref, cst_ref, carry_scr, xp_scr, act_scr):
    tm = x_ref.shape[0]
    i = pl.program_id(0)
    x = x_ref[...]
    h = _modnorm(x, g_ref[...], _mod_rows(sc_ref), _mod_rows(sh_ref)).astype(BF16)
    first = CARRY_ROWS - (FFN_CONV - 1)

    if not seg8:
        @pl.when(i % tiles_per_seq == 0)
        def _():
            carry_scr[...] = cp_ref[0]

    def conv_piece(w_ref, half, cs):
        off = half * D_FF
        up = _dot(h, w_ref[:, cs])
        col = slice(off + cs.start, off + cs.stop)
        if seg8:
            tb = tm // SAMPLE_ROWS
            xp_scr[:, 0:CARRY_ROWS, :] = cp_ref[:, :, col]
            xp_scr[:, CARRY_ROWS:CARRY_ROWS + SAMPLE_ROWS, :] = up.reshape(tb, SAMPLE_ROWS, MXU_WIDTH)
            acc = None
            for j in range(FFN_CONV):
                term = cw_ref[j:j + 1, col] * xp_scr[:, pl.ds(first + j, SAMPLE_ROWS), :]
                acc = term if acc is None else acc + term
            cst_ref[:, :, col] = xp_scr[:, pl.ds(CARRY_ROWS + l_valid - (FFN_CONV - 1), FFN_CONV - 1), :]
            return acc.reshape(tm, MXU_WIDTH) + cb_ref[:, col]
        xp_scr[0:CARRY_ROWS, :] = carry_scr[:, col]
        xp_scr[CARRY_ROWS:CARRY_ROWS + tm, :] = up
        acc = None
        for j in range(FFN_CONV):
            term = cw_ref[j:j + 1, col] * xp_scr[pl.ds(first + j, tm), :]
            acc = term if acc is None else acc + term
        carry_scr[:, col] = xp_scr[tm:tm + CARRY_ROWS, :]
        return acc + cb_ref[:, col]

    for k in range(D_FF // MXU_WIDTH):
        cs = slice(k * MXU_WIDTH, (k + 1) * MXU_WIDTH)
        a = conv_piece(wa_ref, 0, cs)
        g = conv_piece(wg_ref, 1, cs)
        act_scr[:, cs] = (_silu(a) * g).astype(BF16)

    if not seg8:
        @pl.when(i % tiles_per_seq == tiles_per_seq - 1)
        def _():
            cst_ref[0] = carry_scr[CARRY_ROWS - (FFN_CONV - 1):CARRY_ROWS, :]

    out = x + _mod_rows(ga_ref) * _dot(act_scr[...], wd_ref[...])
    if final_norm:
        out = out * lax.rsqrt(jnp.mean(out * out, axis=-1, keepdims=True) + EPS) * gf_ref[...]
    o_ref[...] = out


def _ffn(group, x, mod, conv_prev8, p, g_final, final_norm):
    tm = group.tm_ffn
    rows = x.shape[0]
    seg8 = not group.is_prompt
    tiles_per_seq = 1 if seg8 else group.seq_rows // tm
    single = dict(pipeline_mode=pl.Buffered(1))
    if seg8:
        tb = tm // SAMPLE_ROWS
        cp_spec = pl.BlockSpec((tb, CARRY_ROWS, 2 * D_FF), lambda i: (i, 0, 0))
        cst_spec = pl.BlockSpec((tb, FFN_CONV - 1, 2 * D_FF), lambda i: (i, 0, 0))
        xp_shape = (tb, CARRY_ROWS + SAMPLE_ROWS, MXU_WIDTH)
    else:
        cp_spec = pl.BlockSpec((1, CARRY_ROWS, 2 * D_FF), lambda i: (i // tiles_per_seq, 0, 0))
        cst_spec = pl.BlockSpec((1, FFN_CONV - 1, 2 * D_FF), lambda i: (i // tiles_per_seq, 0, 0))
        xp_shape = (CARRY_ROWS + tm, MXU_WIDTH)
    kern = functools.partial(_ffn_kernel, seg8, group.last_chunk_valid, tiles_per_seq, final_norm)
    return pl.pallas_call(
        kern,
        grid=(rows // tm,),
        in_specs=[
            pl.BlockSpec((tm, D_MODEL), lambda i: (i, 0)),
            _const_spec((1, D_MODEL)),
            _mod_spec(group, tm, 4),
            _mod_spec(group, tm, 3),
            _mod_spec(group, tm, 5),
            cp_spec,
            pl.BlockSpec((D_MODEL, D_FF), lambda i: (0, 0), **single),
            pl.BlockSpec((D_MODEL, D_FF), lambda i: (0, 0), **single),
            _const_spec((FFN_CONV, 2 * D_FF)),
            _const_spec((1, 2 * D_FF)),
            pl.BlockSpec((D_FF, D_MODEL), lambda i: (0, 0), **single),
            _const_spec((1, D_MODEL)),
        ],
        out_specs=[pl.BlockSpec((tm, D_MODEL), lambda i: (i, 0)), cst_spec],
        out_shape=[
            jax.ShapeDtypeStruct((rows, D_MODEL), F32),
            jax.ShapeDtypeStruct((group.n_seq, FFN_CONV - 1, 2 * D_FF), F32),
        ],
        scratch_shapes=[
            pltpu.VMEM((CARRY_ROWS, 2 * D_FF), F32),
            pltpu.VMEM(xp_shape, F32),
            pltpu.VMEM((tm, D_FF), BF16),
        ],
        compiler_params=_cparams("arbitrary"),
        name="ffn",
    )(x, p["g_norm_ffn"], mod, mod, mod, conv_prev8, p["w_up_a"], p["w_up_g"], p["ffn_conv_w"],
      p["ffn_conv_b"], p["w_down"], g_final)


class _Group:
    def __init__(self, is_prompt, n_seq, seq_len):
        self.is_prompt = is_prompt
        self.n_seq = n_seq
        if is_prompt:
            self.seq_rows = seq_len
            self.chunk_rows = CHUNK
            self.chunks_per_seq = seq_len // CHUNK
            self.last_chunk_valid = CHUNK
            self.per_token_mod = False
            self.tm_inproj = 1024
            self.tm_merge = 512
            self.tm_ffn = 512
            self.attn_q_rows = CHUNK
        else:
            self.attn_q_rows = BF16_SUBLANES
            self.tm_rowwise = 256
            self.seq_rows = SAMPLE_ROWS
            self.chunk_rows = SAMPLE_ROWS
            self.chunks_per_seq = 1
            self.last_chunk_valid = seq_len
            self.per_token_mod = True
            self.tm_inproj = 512
            self.tm_merge = 512
            self.tm_ffn = 128


def _pad_front(state, rows):
    return jnp.pad(state, ((0, 0), (rows - state.shape[1], 0), (0, 0)))


def _layer_params(i, w):
    pad_heads = lambda v: jnp.pad(v, (0, LANES - SSD_HEADS)).reshape(1, LANES)
    w_in = w["w_in"][i]
    head_of_channel = jnp.arange(SSD_D_INNER) // SSD_HEAD_DIM
    expand = (jnp.arange(LANES)[:, None] == head_of_channel[None, :]).astype(BF16)
    return {
        "g_norm_mix": w["g_norm_mix"][i].reshape(1, D_MODEL),
        "w_main": jnp.concatenate([w_in[:, a:b] for a, b in _W_IN_PIECES], axis=1).astype(BF16),
        "w_dt": jnp.pad(w_in[:, _IN_OFF_DT:_IN_OFF_DT + SSD_HEADS],
                        ((0, 0), (0, LANES - SSD_HEADS))).astype(BF16),
        "ssd_conv_w": w["ssd_conv_w"][i],
        "ssd_conv_b": w["ssd_conv_b"][i].reshape(1, SSD_CONV_DIM),
        "ssd_dt_bias": pad_heads(w["ssd_dt_bias"][i]),
        "ssd_a_log": pad_heads(w["ssd_a_log"][i]),
        "ssd_d": jnp.repeat(w["ssd_d"][i], SSD_HEAD_DIM).reshape(1, SSD_D_INNER),
        "ssd_norm_g": w["ssd_norm_g"][i].reshape(1, SSD_D_INNER),
        "ssd_expand": expand,
        "sc_conv_w": w["sc_conv_w"][i],
        "attn_sinks": jnp.pad(w["attn_sinks"][i], (0, LANES - ATTN_HEADS)).reshape(1, LANES),
        "gm_ln_g": w["gm_ln_g"][i].reshape(1, D_MODEL),
        "gm_ln_b": w["gm_ln_b"][i].reshape(1, D_MODEL),
        "gm_w_s": w["gm_w_s"][i],
        "gm_b_s": w["gm_b_s"][i],
        "gm_b_exp": jnp.repeat(w["gm_b_s"][i].T, GM_GROUP_WIDTH, axis=1),
        "w_branch": w["w_branch"][i].astype(BF16),
        "w_o": w["w_o"][i].astype(BF16),
        "g_norm_ffn": w["g_norm_ffn"][i].reshape(1, D_MODEL),
        "w_up_a": w["ffn_w_up"][i][:, :D_FF].astype(BF16),
        "w_up_g": w["ffn_w_up"][i][:, D_FF:].astype(BF16),
        "ffn_conv_w": w["ffn_conv_w"][i],
        "ffn_conv_b": w["ffn_conv_b"][i].reshape(1, 2 * D_FF),
        "w_down": w["ffn_w_down"][i].astype(BF16),
    }


def _run_group(group, x, mod_all, states, params, g_final):
    n = group.n_seq
    outs = []
    for i in range(DEPTH):
        p = params[i]
        mod = mod_all[i]
        ssm0, ssdc0, scc0, k0, v0, ffc0 = (None if s is None else s[i] for s in states)
        if group.is_prompt:
            ssm0 = jnp.zeros((n, SSD_HEADS, SSD_HEAD_DIM, SSD_STATE), F32)
            ssdc0 = jnp.zeros((n, SSD_CONV - 1, SSD_CONV_DIM), F32)
            scc0 = jnp.zeros((n, SC_CONV - 1, D_MODEL), F32)
            ffc0 = jnp.zeros((n, FFN_CONV - 1, 2 * D_FF), F32)
        else:
            k0 = k0.reshape(n, WINDOW, ATTN_KV)
            v0 = v0.reshape(n, WINDOW, ATTN_KV)
        y, dtr = _inproj(group, x, mod, p["g_norm_mix"], p["w_main"], p["w_dt"])
        if group.is_prompt:
            ya, ssm1, ssdc1 = _ssd(group, y, dtr, ssm0, _pad_front(ssdc0, CARRY_ROWS), p)
            yb, scc1 = _sconv(group, y, _pad_front(scc0, CARRY_ROWS), p)
            yd, v_rows = _gmlp(group, y, p)
        else:
            ya, ssm1, ssdc1 = _ssd_s(group, y, dtr, ssm0, ssdc0, p)
            yb, scc1 = _sconv_s(group, y, scc0, p)
            yd, v_rows = _gmlp_s(group, y, p)
        yc = _attn(group, y, k0, v0, p)
        x = _merge(group, x, mod, (ya, yb, yc, yd), y, p)
        x, ffc1 = _ffn(group, x, mod, _pad_front(ffc0, CARRY_ROWS), p, g_final, i == DEPTH - 1)

        kcol = slice(Y_K * SSD_BC, (Y_K + 1) * SSD_BC)
        vcol = slice(Y_V * SSD_BC, (Y_V + 1) * SSD_BC)
        if group.is_prompt:
            y3 = y.reshape(n, group.seq_rows, Y_COLS)
            k1 = y3[:, group.seq_rows - WINDOW:, kcol].astype(F32)
            v1 = y3[:, group.seq_rows - WINDOW:, vcol].astype(F32)
            k1 = k1.reshape(n, WINDOW, ATTN_KV_HEADS, ATTN_HEAD_DIM)
            v1 = v1.reshape(n, WINDOW, ATTN_KV_HEADS, ATTN_HEAD_DIM)
            outs.append((ssm1, ssdc1, scc1, k1, v1, ffc1))
        else:
            l = group.last_chunk_valid
            y3 = y.reshape(n, SAMPLE_ROWS, Y_COLS)
            k1 = jnp.concatenate([k0[:, l:], y3[:, :l, kcol].astype(F32)], axis=1)
            v1 = jnp.concatenate([v0[:, l:], y3[:, :l, vcol].astype(F32)], axis=1)
            k1 = k1.reshape(n, WINDOW, ATTN_KV_HEADS, ATTN_HEAD_DIM)
            v1 = v1.reshape(n, WINDOW, ATTN_KV_HEADS, ATTN_HEAD_DIM)
            gv = v_rows.reshape(n, SAMPLE_ROWS, D_MODEL)[:, :l]
            outs.append((ssm1, ssdc1, scc1, k1, v1, ffc1, gv))
    stacked = tuple(jnp.stack([o[j] for o in outs], axis=0) for j in range(len(outs[0])))
    return x, stacked


def kernel(x_prompt, x_sample, c_prompt, c_sample, state_ssm, state_ssd_conv, state_sc_conv, cache_k,
           cache_v, state_ffn_conv, w_ada, b_ada, g_norm_mix, w_in, ssd_conv_w, ssd_conv_b, ssd_dt_bias,
           ssd_a_log, ssd_d, ssd_norm_g, sc_conv_w, attn_sinks, gm_ln_g, gm_ln_b, gm_w_s, gm_b_s,
           w_branch, w_o, g_norm_ffn, ffn_w_up, ffn_conv_w, ffn_conv_b, ffn_w_down, g_final):
    weights = dict(g_norm_mix=g_norm_mix, w_in=w_in, ssd_conv_w=ssd_conv_w, ssd_conv_b=ssd_conv_b,
                   ssd_dt_bias=ssd_dt_bias, ssd_a_log=ssd_a_log, ssd_d=ssd_d, ssd_norm_g=ssd_norm_g,
                   sc_conv_w=sc_conv_w, attn_sinks=attn_sinks, gm_ln_g=gm_ln_g, gm_ln_b=gm_ln_b,
                   gm_w_s=gm_w_s, gm_b_s=gm_b_s, w_branch=w_branch, w_o=w_o, g_norm_ffn=g_norm_ffn,
                   ffn_w_up=ffn_w_up, ffn_conv_w=ffn_conv_w, ffn_conv_b=ffn_conv_b, ffn_w_down=ffn_w_down)
    params = [_layer_params(i, weights) for i in range(DEPTH)]
    gf = g_final.reshape(1, D_MODEL)

    nb_p, len_p, _ = x_prompt.shape
    nb_s, len_s, _ = x_sample.shape
    assert len_p % CHUNK == 0 and len_s <= SAMPLE_ROWS and len_s >= SSD_CONV - 1
    prompt = _Group(True, nb_p, len_p)
    sample = _Group(False, nb_s, len_s)

    c_rows = nb_p + nb_s
    c_pad = -c_rows % SAMPLE_ROWS
    c_all = jnp.pad(jnp.concatenate([c_prompt, c_sample], axis=0), ((0, c_pad), (0, 0)))
    mod = _ada(c_all, w_ada, b_ada)
    mod_p = mod[:, :nb_p].reshape(DEPTH, nb_p, 1, 6 * D_MODEL)
    mod_s = jnp.repeat(mod[:, nb_p:nb_p + nb_s], SAMPLE_ROWS, axis=1)

    xp = x_prompt.reshape(nb_p * len_p, D_MODEL)
    xs = jnp.pad(x_sample, ((0, 0), (0, SAMPLE_ROWS - len_s), (0, 0))).reshape(nb_s * SAMPLE_ROWS, D_MODEL)

    y_p, st_p = _run_group(prompt, xp, mod_p, (None,) * 6, params, gf)
    y_s, st_s = _run_group(sample, xs, mod_s,
                           (state_ssm, state_ssd_conv, state_sc_conv, cache_k, cache_v, state_ffn_conv),
                           params, gf)
    y_prompt = y_p.reshape(nb_p, len_p, D_MODEL)
    y_sample = y_s.reshape(nb_s, SAMPLE_ROWS, D_MODEL)[:, :len_s]
    return (y_prompt, y_sample) + st_p + st_s
```

```python
import functools
import math

import jax
import jax.numpy as jnp
from jax import lax
from jax.experimental import pallas as pl
from jax.experimental.pallas import tpu as pltpu

F32 = jnp.float32
BF16 = jnp.bfloat16

D_MODEL = 1024
DEPTH = 4
N_BRANCH = 4
SSD_HEADS = 16
SSD_HEAD_DIM = 64
SSD_GROUPS = 4
SSD_STATE = 64
SSD_CONV = 4
SSD_D_INNER = 1024
SSD_BC = SSD_GROUPS * SSD_STATE
SSD_CONV_DIM = SSD_D_INNER + 2 * SSD_BC
SC_CONV = 3
ATTN_HEADS = 16
ATTN_KV_HEADS = 4
ATTN_GROUP = 4
ATTN_HEAD_DIM = 64
ATTN_KV = ATTN_KV_HEADS * ATTN_HEAD_DIM
WINDOW = 128
GM_GROUPS = 8
GM_GROUP_WIDTH = 128
D_FF = 2816
FFN_CONV = 3
EPS = 1e-6

CHUNK = 128
SAMPLE_ROWS = 8
CARRY_ROWS = 8
LANES = 128
BF16_SUBLANES = 16
MXU_WIDTH = 256
VMEM_LIMIT = 56 * 1024 * 1024

(Y_Z, Y_XS, Y_BG, Y_CG, Y_XSC, Y_Q, Y_U, Y_VGM, Y_G0) = range(9)
Y_WIDE = 12
Y_SSDB, Y_SSDC, Y_K, Y_V = (Y_WIDE * 4 + i for i in range(4))
Y_COLS = Y_WIDE * D_MODEL + 4 * SSD_BC
INPROJ_TN = Y_COLS // 4

_IN_OFF_XBC = 1024
_IN_OFF_DT = 2560
_IN_OFF_BCX = 2576
_IN_OFF_Q = 5648
_IN_OFF_K = 6672
_IN_OFF_V = 6928
_IN_OFF_UV = 7184
_IN_OFF_GATES = 9232
_W_IN_PIECES = (
    (0, 1024),
    (_IN_OFF_XBC, _IN_OFF_XBC + 1024),
    (_IN_OFF_BCX, _IN_OFF_BCX + 1024),
    (_IN_OFF_BCX + 1024, _IN_OFF_BCX + 2048),
    (_IN_OFF_BCX + 2048, _IN_OFF_BCX + 3072),
    (_IN_OFF_Q, _IN_OFF_Q + 1024),
    (_IN_OFF_UV, _IN_OFF_UV + 1024),
    (_IN_OFF_UV + 1024, _IN_OFF_UV + 2048),
    (_IN_OFF_GATES, _IN_OFF_GATES + 4096),
    (_IN_OFF_XBC + 1024, _IN_OFF_XBC + 1280),
    (_IN_OFF_XBC + 1280, _IN_OFF_XBC + 1536),
    (_IN_OFF_K, _IN_OFF_K + 256),
    (_IN_OFF_V, _IN_OFF_V + 256),
)


def _cparams(*sem):
    return pltpu.CompilerParams(dimension_semantics=sem, vmem_limit_bytes=VMEM_LIMIT)


def _const_spec(shape):
    zeros = (0,) * len(shape)
    return pl.BlockSpec(shape, lambda *_: zeros)


def _silu(x):
    return x * jax.nn.sigmoid(x)


def _mod_rows(ref):
    return ref[0] if len(ref.shape) == 3 else ref[...]


def _modnorm(x, g, sc, sh):
    r = lax.rsqrt(jnp.mean(x * x, axis=-1, keepdims=True) + EPS)
    return (x * r * g) * (1.0 + sc) + sh


def _split3(x):
    hi = x.astype(BF16)
    r1 = x - hi.astype(F32)
    mid = r1.astype(BF16)
    lo = (r1 - mid.astype(F32)).astype(BF16)
    return hi, mid, lo


def _dot(a, b):
    return jnp.dot(a, b, preferred_element_type=F32)


def _dot_nt(a, b):
    return lax.dot_general(a, b, (((1,), (1,)), ((), ())), preferred_element_type=F32)


def _dot_tn(a, b):
    return lax.dot_general(a, b, (((0,), (0,)), ((), ())), preferred_element_type=F32)


def _dot_exact_lhs(a_bf16, x):
    hi, mid, lo = _split3(x)
    return _dot(a_bf16, hi) + _dot(a_bf16, mid) + _dot(a_bf16, lo)


def _dot_exact_rhs(x, e_bf16):
    hi, mid, lo = _split3(x)
    return _dot(hi, e_bf16) + _dot(mid, e_bf16) + _dot(lo, e_bf16)


def _chunk_rows(ref, rows):
    x = ref[...].astype(F32)
    if x.shape[0] == rows:
        return x
    pad = jnp.zeros((rows - x.shape[0], x.shape[1]), F32)
    return jnp.concatenate([x, pad], axis=0)


def _ada_kernel(c_ref, w_ref, b_ref, o_ref):
    s = _silu(c_ref[...]).astype(BF16)
    o_ref[0] = _dot(s, w_ref[0].astype(BF16)) + b_ref[0]


def _ada(c_all, w_ada, b_ada):
    rows = c_all.shape[0]
    n_tiles = w_ada.shape[-1] // D_MODEL
    return pl.pallas_call(
        _ada_kernel,
        grid=(DEPTH, n_tiles),
        in_specs=[
            _const_spec((rows, D_MODEL)),
            pl.BlockSpec((1, D_MODEL, D_MODEL), lambda l, n: (l, 0, n)),
            pl.BlockSpec((1, 1, D_MODEL), lambda l, n: (l, 0, n)),
        ],
        out_specs=pl.BlockSpec((1, rows, D_MODEL), lambda l, n: (l, 0, n)),
        out_shape=jax.ShapeDtypeStruct((DEPTH, rows, 6 * D_MODEL), F32),
        compiler_params=_cparams("arbitrary", "arbitrary"),
        name="ada",
    )(c_all, w_ada, b_ada.reshape(DEPTH, 1, 6 * D_MODEL))


def _inproj_kernel(x_ref, g_ref, sc_ref, sh_ref, w_ref, wdt_ref, y_ref, dtr_ref, h_scr):
    @pl.when(pl.program_id(1) == 0)
    def _():
        h = _modnorm(x_ref[...], g_ref[...], _mod_rows(sc_ref), _mod_rows(sh_ref)).astype(BF16)
        h_scr[...] = h
        dtr_ref[...] = _dot(h, wdt_ref[...])

    y_ref[...] = _dot(h_scr[...], w_ref[...]).astype(y_ref.dtype)


def _mod_spec(group, tm, piece):
    if group.per_token_mod:
        return pl.BlockSpec((tm, D_MODEL), lambda i, *_: (i, piece))
    tiles_per_seq = group.seq_rows // tm
    return pl.BlockSpec((1, 1, D_MODEL), lambda i, *_: (i // tiles_per_seq, 0, piece))


def _inproj(group, x, mod, g, w_main, w_dt):
    tm = group.tm_inproj
    rows = x.shape[0]
    tn = INPROJ_TN
    return pl.pallas_call(
        _inproj_kernel,
        grid=(rows // tm, Y_COLS // tn),
        in_specs=[
            pl.BlockSpec((tm, D_MODEL), lambda i, n: (i, 0)),
            _const_spec((1, D_MODEL)),
            _mod_spec(group, tm, 1),
            _mod_spec(group, tm, 0),
            pl.BlockSpec((D_MODEL, tn), lambda i, n: (0, n)),
            _const_spec((D_MODEL, LANES)),
        ],
        out_specs=[
            pl.BlockSpec((tm, tn), lambda i, n: (i, n)),
            pl.BlockSpec((tm, LANES), lambda i, n: (i, 0)),
        ],
        out_shape=[
            jax.ShapeDtypeStruct((rows, Y_COLS), BF16),
            jax.ShapeDtypeStruct((rows, LANES), F32),
        ],
        scratch_shapes=[pltpu.VMEM((tm, D_MODEL), BF16)],
        compiler_params=_cparams("arbitrary", "arbitrary"),
        name="inproj",
    )(x, g, mod, mod, w_main, w_dt)


def _causal_conv(xp_ref, w_ref, width, rows):
    first = CARRY_ROWS - (width - 1)
    acc = w_ref[0:1, :] * xp_ref[pl.ds(first, rows), :]
    for j in range(1, width):
        acc = acc + w_ref[j:j + 1, :] * xp_ref[pl.ds(first + j, rows), :]
    return acc


def _chunk_spec(group, width, piece):
    nc = group.chunks_per_seq
    return pl.BlockSpec((group.chunk_rows, width), lambda b, c: (b * nc + c, piece))


HEADS_PER_GROUP = SSD_HEADS // SSD_GROUPS
GROUP_WIDTH = HEADS_PER_GROUP * SSD_HEAD_DIM


def _softplus(x):
    return jnp.maximum(x, 0.0) + jnp.log1p(jnp.exp(-jnp.abs(x)))


def _ssd_intra_group(g, cs, cs_t, cmat, bmat, x_b, causal):
    T = cs.shape[0]
    gs = slice(g * SSD_STATE, (g + 1) * SSD_STATE)
    cb = _dot_nt(cmat[:, gs], bmat[:, gs])
    low_half = lax.broadcasted_iota(jnp.int32, (T, 2 * SSD_HEAD_DIM), 1) < SSD_HEAD_DIM
    pair_out = []
    for pr in range(HEADS_PER_GROUP // 2):
        h0 = g * HEADS_PER_GROUP + 2 * pr
        xp = x_b[:, h0 * SSD_HEAD_DIM:(h0 + 2) * SSD_HEAD_DIM]
        acc = None
        for k in range(2):
            h = h0 + k
            diff = cs[:, h:h + 1] - cs_t[h:h + 1, :]
            lmat = jnp.where(causal, jnp.exp(jnp.where(causal, diff, 0.0)), 0.0)
            m = (cb * lmat).astype(BF16)
            keep = low_half if k == 0 else jnp.logical_not(low_half)
            part = _dot(m, jnp.where(keep, xp, jnp.zeros_like(xp)))
            acc = part if acc is None else acc + part
        pair_out.append(acc)
    return jnp.concatenate(pair_out, axis=1)


def _ssd_finish(y, xs, z, dsk, ng):
    y = (y + xs * dsk) * _silu(z)
    gw = SSD_D_INNER // SSD_GROUPS
    normed = []
    for g in range(SSD_GROUPS):
        yg = y[:, g * gw:(g + 1) * gw]
        normed.append(yg * lax.rsqrt(jnp.mean(yg * yg, axis=-1, keepdims=True) + EPS))
    return jnp.concatenate(normed, axis=1) * ng


def _seq_pos(rows, width):
    return lax.broadcasted_iota(jnp.int32, (rows, width), 0) % SAMPLE_ROWS


def _shift_rows(x, d, head, pos):
    return jnp.where(pos >= d, pltpu.roll(x, d, 0), head)


def _ssd_s_kernel(l_valid, z_ref, xs_ref, bm_ref, cm_ref, dtr_ref, s0_ref, h1_ref, h2_ref, h3_ref,
                  cw_ref, cb_ref, dtb_ref, alog_ref, dsk_ref, ng_ref, e_ref,
                  y_ref, sfin_ref, raw_ref):
    T = CHUNK
    n_seq = T // SAMPLE_ROWS
    raw = jnp.concatenate([xs_ref[...].astype(F32), bm_ref[...].astype(F32),
                           cm_ref[...].astype(F32)], axis=1)
    raw_ref[...] = raw
    pos = _seq_pos(T, SSD_CONV_DIM)
    conv = (cb_ref[...] + cw_ref[3:4, :] * raw
            + cw_ref[2:3, :] * _shift_rows(raw, 1, h1_ref[...], pos)
            + cw_ref[1:2, :] * _shift_rows(raw, 2, h2_ref[...], pos)
            + cw_ref[0:1, :] * _shift_rows(raw, 3, h3_ref[...], pos))
    xbc = _silu(conv)
    xs = xbc[:, 0:SSD_D_INNER]
    bmat = xbc[:, SSD_D_INNER:SSD_D_INNER + SSD_BC].astype(BF16)
    cmat = xbc[:, SSD_D_INNER + SSD_BC:SSD_CONV_DIM].astype(BF16)

    dt = _softplus(dtr_ref[...] + dtb_ref[...])
    dt = jnp.where(_seq_pos(T, LANES) < l_valid, dt, 0.0)
    dta = dt * (-jnp.exp(alog_ref[...]))
    row = lax.broadcasted_iota(jnp.int32, (T, T), 0)
    col = lax.broadcasted_iota(jnp.int32, (T, T), 1)
    causal = (row >= col) & (row // SAMPLE_ROWS == col // SAMPLE_ROWS)
    tri = jnp.where(causal, 1.0, 0.0).astype(BF16)
    cs = _dot_exact_lhs(tri, dta)
    last = jnp.where(col == (row // SAMPLE_ROWS) * SAMPLE_ROWS + (SAMPLE_ROWS - 1), 1.0, 0.0).astype(BF16)
    total_rows = _dot_exact_lhs(last, cs)
    cs_t = cs.T
    e = e_ref[...]
    dt_x = _dot_exact_rhs(dt, e)
    ecs_x = _dot_exact_rhs(jnp.exp(cs), e)
    dec_x = _dot_exact_rhs(jnp.exp(total_rows - cs), e)
    x_dt = xs * dt_x
    x_b = x_dt.astype(BF16)
    x_dec_t = (x_dt * dec_x).T.astype(BF16)

    seq_of_row_wide = lax.broadcasted_iota(jnp.int32, (T, GROUP_WIDTH), 0) // SAMPLE_ROWS
    seq_of_row = lax.broadcasted_iota(jnp.int32, (T, SSD_STATE), 0) // SAMPLE_ROWS
    y_parts = []
    for g in range(SSD_GROUPS):
        gs = slice(g * SSD_STATE, (g + 1) * SSD_STATE)
        c_g, b_g = cmat[:, gs], bmat[:, gs]
        xt_g = x_dec_t[g * GROUP_WIDTH:(g + 1) * GROUP_WIDTH, :]
        y_off = jnp.zeros((T, GROUP_WIDTH), F32)
        for b in range(n_seq):
            s_bg = s0_ref[0, b, g * HEADS_PER_GROUP:(g + 1) * HEADS_PER_GROUP].reshape(GROUP_WIDTH, SSD_STATE)
            y_off = jnp.where(seq_of_row_wide == b, _dot_nt(c_g, s_bg.astype(BF16)), y_off)
            ds = _dot(xt_g, jnp.where(seq_of_row == b, b_g, jnp.zeros_like(b_g)))
            decay = jnp.exp(cs[b * SAMPLE_ROWS + SAMPLE_ROWS - 1:(b + 1) * SAMPLE_ROWS, :])
            for r in range(HEADS_PER_GROUP):
                h = g * HEADS_PER_GROUP + r
                sfin_ref[0, b, h] = (s0_ref[0, b, h] * decay[:, h:h + 1]
                                  + ds[r * SSD_HEAD_DIM:(r + 1) * SSD_HEAD_DIM, :])
        y_diag = _ssd_intra_group(g, cs, cs_t, cmat, bmat, x_b, causal)
        y_parts.append(y_diag + y_off * ecs_x[:, g * GROUP_WIDTH:(g + 1) * GROUP_WIDTH])
    y = _ssd_finish(jnp.concatenate(y_parts, axis=1), xs, z_ref[...].astype(F32), dsk_ref[...], ng_ref[...])
    y_ref[...] = y.astype(y_ref.dtype)


def _history_tiles(state, width):
    n, k, c = state.shape
    tiles = []
    for d in range(1, width):
        rows = [state[:, k + t - d] if t < d else jnp.zeros((n, c), state.dtype)
                for t in range(SAMPLE_ROWS)]
        tiles.append(jnp.stack(rows, axis=1).reshape(n * SAMPLE_ROWS, c))
    return tiles


def _ssd_s(group, y, dtr, layer, s_all, conv_state, p):
    rows = y.shape[0]
    n_seq = CHUNK // SAMPLE_ROWS
    tok = lambda width, piece: pl.BlockSpec((CHUNK, width), lambda i: (i, piece))
    state_spec = pl.BlockSpec((1, n_seq, SSD_HEADS, SSD_HEAD_DIM, SSD_STATE), lambda i: (layer, i, 0, 0, 0))
    h1, h2, h3 = _history_tiles(conv_state, SSD_CONV)
    ya, sfin, raw = pl.pallas_call(
        functools.partial(_ssd_s_kernel, group.last_chunk_valid),
        grid=(rows // CHUNK,),
        in_specs=[
            tok(D_MODEL, Y_Z), tok(D_MODEL, Y_XS), tok(SSD_BC, Y_SSDB), tok(SSD_BC, Y_SSDC),
            tok(LANES, 0), state_spec,
            tok(SSD_CONV_DIM, 0), tok(SSD_CONV_DIM, 0), tok(SSD_CONV_DIM, 0),
            _const_spec((SSD_CONV, SSD_CONV_DIM)), _const_spec((1, SSD_CONV_DIM)),
            _const_spec((1, LANES)), _const_spec((1, LANES)),
            _const_spec((1, SSD_D_INNER)), _const_spec((1, SSD_D_INNER)),
            _const_spec((LANES, SSD_D_INNER)),
        ],
        out_specs=[tok(D_MODEL, 0), state_spec, tok(SSD_CONV_DIM, 0)],
        out_shape=[
            jax.ShapeDtypeStruct((rows, D_MODEL), BF16),
            jax.ShapeDtypeStruct(s_all.shape, F32),
            jax.ShapeDtypeStruct((rows, SSD_CONV_DIM), F32),
        ],
        input_output_aliases={5: 1},
        compiler_params=_cparams("arbitrary"),
        name="ssd_s",
    )(y, y, y, y, dtr, s_all, h1, h2, h3,
      p["ssd_conv_w"], p["ssd_conv_b"], p["ssd_dt_bias"], p["ssd_a_log"], p["ssd_d"],
      p["ssd_norm_g"], p["ssd_expand"])
    l = group.last_chunk_valid
    conv_new = raw.reshape(group.n_seq, SAMPLE_ROWS, SSD_CONV_DIM)[:, l - (SSD_CONV - 1):l]
    return ya, sfin, conv_new


def _ssd_kernel(l_valid, n_chunks,
                z_ref, xs_ref, bm_ref, cm_ref, dtr_ref, s0_ref, cp_ref,
                cw_ref, cb_ref, dtb_ref, alog_ref, dsk_ref, ng_ref, e_ref,
                y_ref, sfin_ref, cst_ref,
                s_scr, xp_scr):
    T = CHUNK
    c = pl.program_id(1)

    @pl.when(c == 0)
    def _():
        s_scr[...] = s0_ref[0]
        xp_scr[0:CARRY_ROWS, :] = cp_ref[0]

    xp_scr[CARRY_ROWS:CARRY_ROWS + T, 0:SSD_D_INNER] = _chunk_rows(xs_ref, T)
    xp_scr[CARRY_ROWS:CARRY_ROWS + T, SSD_D_INNER:SSD_D_INNER + SSD_BC] = _chunk_rows(bm_ref, T)
    xp_scr[CARRY_ROWS:CARRY_ROWS + T, SSD_D_INNER + SSD_BC:SSD_CONV_DIM] = _chunk_rows(cm_ref, T)
    xbc = _silu(_causal_conv(xp_scr, cw_ref, SSD_CONV, T) + cb_ref[...])
    xs = xbc[:, 0:SSD_D_INNER]
    bmat = xbc[:, SSD_D_INNER:SSD_D_INNER + SSD_BC].astype(BF16)
    cmat = xbc[:, SSD_D_INNER + SSD_BC:SSD_CONV_DIM].astype(BF16)

    dt = _softplus(_chunk_rows(dtr_ref, T) + dtb_ref[...])
    row = lax.broadcasted_iota(jnp.int32, (T, T), 0)
    col = lax.broadcasted_iota(jnp.int32, (T, T), 1)
    if l_valid < T:
        rvalid = lax.broadcasted_iota(jnp.int32, (T, LANES), 0) < l_valid
        dt = jnp.where(rvalid, dt, 0.0)
    dta = dt * (-jnp.exp(alog_ref[...]))
    causal = row >= col
    tri = jnp.where(causal, 1.0, 0.0).astype(BF16)
    cs = _dot_exact_lhs(tri, dta)
    cs_t = cs.T
    total = cs[T - 1:T, :]
    e = e_ref[...]
    dt_x = _dot_exact_rhs(dt, e)
    ecs_x = _dot_exact_rhs(jnp.exp(cs), e)
    dec_x = _dot_exact_rhs(jnp.exp(total - cs), e)
    chunk_decay = jnp.exp(total)

    x_dt = xs * dt_x
    x_b = x_dt.astype(BF16)
    x_dec = (x_dt * dec_x).astype(BF16)

    y_parts = []
    for g in range(SSD_GROUPS):
        gs = slice(g * SSD_STATE, (g + 1) * SSD_STATE)
        s_g = s_scr[g * HEADS_PER_GROUP:(g + 1) * HEADS_PER_GROUP].reshape(GROUP_WIDTH, SSD_STATE)
        y_diag = _ssd_intra_group(g, cs, cs_t, cmat, bmat, x_b, causal)
        y_off = _dot_nt(cmat[:, gs], s_g.astype(BF16))
        ds_g = _dot_tn(x_dec[:, g * GROUP_WIDTH:(g + 1) * GROUP_WIDTH], bmat[:, gs])
        y_parts.append(y_diag + y_off * ecs_x[:, g * GROUP_WIDTH:(g + 1) * GROUP_WIDTH])
        for r in range(HEADS_PER_GROUP):
            h = g * HEADS_PER_GROUP + r
            s_scr[h] = (s_scr[h] * chunk_decay[:, h:h + 1]
                        + ds_g[r * SSD_HEAD_DIM:(r + 1) * SSD_HEAD_DIM, :])

    y = _ssd_finish(jnp.concatenate(y_parts, axis=1), xs, _chunk_rows(z_ref, T), dsk_ref[...], ng_ref[...])
    y_ref[...] = y[0:y_ref.shape[0], :].astype(y_ref.dtype)

    @pl.when(c == n_chunks - 1)
    def _():
        sfin_ref[0] = s_scr[...]
        cst_ref[0] = xp_scr[pl.ds(CARRY_ROWS + l_valid - (SSD_CONV - 1), SSD_CONV - 1), :]

    if n_chunks > 1:
        xp_scr[0:CARRY_ROWS, :] = xp_scr[T:T + CARRY_ROWS, :]


def _ssd(group, y, dtr, s0, conv_prev8, p):
    nb, nc = group.n_seq, group.chunks_per_seq
    rows = y.shape[0]
    kern = functools.partial(_ssd_kernel, group.last_chunk_valid, nc)
    return pl.pallas_call(
        kern,
        grid=(nb, nc),
        in_specs=[
            _chunk_spec(group, D_MODEL, Y_Z),
            _chunk_spec(group, D_MODEL, Y_XS),
            _chunk_spec(group, SSD_BC, Y_SSDB),
            _chunk_spec(group, SSD_BC, Y_SSDC),
            _chunk_spec(group, LANES, 0),
            pl.BlockSpec((1, SSD_HEADS, SSD_HEAD_DIM, SSD_STATE), lambda b, c: (b, 0, 0, 0)),
            pl.BlockSpec((1, CARRY_ROWS, SSD_CONV_DIM), lambda b, c: (b, 0, 0)),
            _const_spec((SSD_CONV, SSD_CONV_DIM)),
            _const_spec((1, SSD_CONV_DIM)),
            _const_spec((1, LANES)),
            _const_spec((1, LANES)),
            _const_spec((1, SSD_D_INNER)),
            _const_spec((1, SSD_D_INNER)),
            _const_spec((LANES, SSD_D_INNER)),
        ],
        out_specs=[
            _chunk_spec(group, D_MODEL, 0),
            pl.BlockSpec((1, SSD_HEADS, SSD_HEAD_DIM, SSD_STATE), lambda b, c: (b, 0, 0, 0)),
            pl.BlockSpec((1, SSD_CONV - 1, SSD_CONV_DIM), lambda b, c: (b, 0, 0)),
        ],
        out_shape=[
            jax.ShapeDtypeStruct((rows, D_MODEL), BF16),
            jax.ShapeDtypeStruct((nb, SSD_HEADS, SSD_HEAD_DIM, SSD_STATE), F32),
            jax.ShapeDtypeStruct((nb, SSD_CONV - 1, SSD_CONV_DIM), F32),
        ],
        scratch_shapes=[
            pltpu.VMEM((SSD_HEADS, SSD_HEAD_DIM, SSD_STATE), F32),
            pltpu.VMEM((CARRY_ROWS + CHUNK, SSD_CONV_DIM), F32),
        ],
        compiler_params=_cparams("arbitrary", "arbitrary"),
        name="ssd",
    )(y, y, y, y, dtr, s0, conv_prev8,
      p["ssd_conv_w"], p["ssd_conv_b"], p["ssd_dt_bias"], p["ssd_a_log"], p["ssd_d"],
      p["ssd_norm_g"], p["ssd_expand"])


def _sconv_kernel(n_tiles, bg_ref, cg_ref, xs_ref, cp_ref, cw_ref, y_ref, cst_ref, xp_scr):
    T = bg_ref.shape[0]
    c = pl.program_id(1)

    @pl.when(c == 0)
    def _():
        xp_scr[0:CARRY_ROWS, :] = cp_ref[0]

    xp_scr[CARRY_ROWS:CARRY_ROWS + T, :] = cg_ref[...].astype(F32) * xs_ref[...].astype(F32)
    y = bg_ref[...].astype(F32) * _causal_conv(xp_scr, cw_ref, SC_CONV, T)
    y_ref[...] = y.astype(y_ref.dtype)

    @pl.when(c == n_tiles - 1)
    def _():
        cst_ref[0] = xp_scr[pl.ds(CARRY_ROWS + T - (SC_CONV - 1), SC_CONV - 1), :]

    xp_scr[0:CARRY_ROWS, :] = xp_scr[T:T + CARRY_ROWS, :]


def _sconv(group, y, conv_prev8, p):
    nb = group.n_seq
    tm = group.tm_rowwise
    nt = group.seq_rows // tm
    rows = y.shape[0]
    tok = lambda piece: pl.BlockSpec((tm, D_MODEL), lambda b, c: (b * nt + c, piece))
    return pl.pallas_call(
        functools.partial(_sconv_kernel, nt),
        grid=(nb, nt),
        in_specs=[
            tok(Y_BG), tok(Y_CG), tok(Y_XSC),
            pl.BlockSpec((1, CARRY_ROWS, D_MODEL), lambda b, c: (b, 0, 0)),
            _const_spec((SC_CONV, D_MODEL)),
        ],
        out_specs=[tok(0), pl.BlockSpec((1, SC_CONV - 1, D_MODEL), lambda b, c: (b, 0, 0))],
        out_shape=[
            jax.ShapeDtypeStruct((rows, D_MODEL), BF16),
            jax.ShapeDtypeStruct((nb, SC_CONV - 1, D_MODEL), F32),
        ],
        scratch_shapes=[pltpu.VMEM((CARRY_ROWS + tm, D_MODEL), F32)],
        compiler_params=_cparams("arbitrary", "arbitrary"),
        name="sconv",
    )(y, y, y, conv_prev8, p["sc_conv_w"])


def _sconv_s_kernel(bg_ref, cg_ref, xs_ref, h1_ref, h2_ref, cw_ref, y_ref, u_ref):
    u = cg_ref[...].astype(F32) * xs_ref[...].astype(F32)
    u_ref[...] = u
    pos = _seq_pos(*u.shape)
    conv = (cw_ref[2:3, :] * u + cw_ref[1:2, :] * _shift_rows(u, 1, h1_ref[...], pos)
            + cw_ref[0:1, :] * _shift_rows(u, 2, h2_ref[...], pos))
    y_ref[...] = (bg_ref[...].astype(F32) * conv).astype(y_ref.dtype)


def _sconv_s(group, y, conv_state, p):
    rows = y.shape[0]
    tm = group.tm_rowwise
    tok = lambda piece: pl.BlockSpec((tm, D_MODEL), lambda i: (i, piece))
    h1, h2 = _history_tiles(conv_state, SC_CONV)
    yb, u = pl.pallas_call(
        _sconv_s_kernel,
        grid=(rows // tm,),
        in_specs=[tok(Y_BG), tok(Y_CG), tok(Y_XSC), tok(0), tok(0), _const_spec((SC_CONV, D_MODEL))],
        out_specs=[tok(0), tok(0)],
        out_shape=[jax.ShapeDtypeStruct((rows, D_MODEL), BF16), jax.ShapeDtypeStruct((rows, D_MODEL), F32)],
        compiler_params=_cparams("arbitrary"),
        name="sconv_s",
    )(y, y, y, h1, h2, p["sc_conv_w"])
    l = group.last_chunk_valid
    return yb, u.reshape(group.n_seq, SAMPLE_ROWS, D_MODEL)[:, l - (SC_CONV - 1):l]


def _spread_kv(k_rows, v_rows):
    kf = k_rows.astype(F32)
    vf = v_rows.astype(F32)
    blk = lax.broadcasted_iota(jnp.int32, kf.shape, 1) // ATTN_HEAD_DIM
    kspread, vplaced = [], []
    for g in range(ATTN_KV_HEADS):
        tk = jnp.where(blk == g, kf, 0.0)
        uk = tk + pltpu.roll(tk, ATTN_HEAD_DIM, 1)
        kspread.append((uk + pltpu.roll(uk, 2 * ATTN_HEAD_DIM, 1)).astype(BF16))
        tv = jnp.where(blk == g, vf, 0.0)
        tv1 = pltpu.roll(tv, ATTN_HEAD_DIM, 1)
        shifted = (tv, tv1, pltpu.roll(tv, 2 * ATTN_HEAD_DIM, 1), pltpu.roll(tv1, 2 * ATTN_HEAD_DIM, 1))
        vplaced.append([shifted[(r - g) % ATTN_KV_HEADS].astype(BF16) for r in range(ATTN_GROUP)])
    return kspread, vplaced


def _attn_kernel(tq_rows, carry, n_new, *refs):
    if carry:
        q_ref, k_ref, v_ref, sink_ref, bias_ref, y_ref, kprev_scr, vprev_scr = refs
    else:
        q_ref, k_ref, v_ref, kp_ref, vp_ref, sink_ref, bias_ref, y_ref, kout_ref, vout_ref = refs
    T = tq_rows
    c = pl.program_id(1)
    scale = ATTN_HEAD_DIM ** -0.5
    if q_ref.shape[0] == T:
        q = q_ref[...] * jnp.asarray(scale, q_ref.dtype)
    else:
        q = (_chunk_rows(q_ref, T) * scale).astype(BF16)
    if k_ref.shape[0] == WINDOW:
        kc, vc = k_ref[...], v_ref[...]
    else:
        kc = _chunk_rows(k_ref, WINDOW).astype(BF16)
        vc = _chunk_rows(v_ref, WINDOW).astype(BF16)
    ks_cur, vp_cur = _spread_kv(kc, vc)
    if carry:
        read_slot = (c + 1) % 2
        write_slot = c % 2

        @pl.when(c == 0)
        def _():
            kprev_scr[read_slot] = jnp.zeros(kprev_scr.shape[1:], BF16)
            vprev_scr[read_slot] = jnp.zeros(vprev_scr.shape[1:], BF16)
        ks_prev = [kprev_scr[read_slot, g] for g in range(ATTN_KV_HEADS)]
        vp_prev = [[vprev_scr[read_slot, g * ATTN_GROUP + r] for r in range(ATTN_GROUP)]
                   for g in range(ATTN_KV_HEADS)]
    else:
        ks_prev, vp_prev = _spread_kv(kp_ref[0, 0].astype(BF16), vp_ref[0, 0].astype(BF16))
        keep = WINDOW - n_new
        for src, new, dst in ((kp_ref, k_ref, kout_ref), (vp_ref, v_ref, vout_ref)):
            dst[0, 0, pl.ds(0, keep), :] = src[0, 0, pl.ds(n_new, keep), :]
            dst[0, 0, pl.ds(keep, n_new), :] = new[...].astype(F32)[0:n_new, :]

    q_blk = lax.broadcasted_iota(jnp.int32, (T, ATTN_KV), 1) // ATTN_HEAD_DIM
    scores = []
    for g in range(ATTN_KV_HEADS):
        qg = q[:, g * ATTN_KV:(g + 1) * ATTN_KV]
        q4 = jnp.concatenate([jnp.where(q_blk == r, qg, jnp.zeros_like(qg))
                              for r in range(ATTN_GROUP)], axis=0)
        keys = jnp.concatenate([ks_prev[g], ks_cur[g]], axis=0)
        scores.append(_dot_nt(q4, keys))
    groups = range(ATTN_KV_HEADS)
    bias = [bias_ref[0, g * ATTN_GROUP:(g + 1) * ATTN_GROUP].reshape(ATTN_GROUP * T, 2 * WINDOW)
            for g in groups]
    sinks = [jnp.concatenate([jnp.broadcast_to(sink_ref[:, h:h + 1], (T, 1))
                              for h in range(g * ATTN_GROUP, (g + 1) * ATTN_GROUP)], axis=0)
             for g in groups]
    s = [scores[g] - bias[g] for g in groups]
    m = [jnp.maximum(jnp.max(s[g], axis=-1, keepdims=True), sinks[g]) for g in groups]
    e = [jnp.exp(s[g] - m[g]) for g in groups]
    den = [jnp.sum(e[g], axis=-1, keepdims=True) + jnp.exp(sinks[g] - m[g]) for g in groups]
    pn = [(e[g] * (1.0 / den[g])).astype(BF16) for g in groups]
    probs = [jnp.concatenate([pn[g][r * T:(r + 1) * T] for r in range(ATTN_GROUP)], axis=1)
             for g in groups]
    outs = []
    for g in range(ATTN_KV_HEADS):
        vals = jnp.concatenate([blk for r in range(ATTN_GROUP)
                                for blk in (vp_prev[g][r], vp_cur[g][r])], axis=0)
        outs.append(_dot(probs[g], vals))
    y = jnp.concatenate(outs, axis=1)
    y_ref[...] = y[0:y_ref.shape[0], :].astype(y_ref.dtype)
    if carry:
        for g in range(ATTN_KV_HEADS):
            kprev_scr[write_slot, g] = ks_cur[g]
            for r in range(ATTN_GROUP):
                vprev_scr[write_slot, g * ATTN_GROUP + r] = vp_cur[g][r]


def _attn_bias_table(q_rows):
    t = jnp.arange(q_rows)[:, None]
    j = jnp.arange(2 * WINDOW)[None, :]
    dist = WINDOW + t - j
    band = (dist >= 0) & (dist <= WINDOW)
    slopes = 2.0 ** (-8.0 * jnp.arange(1, ATTN_HEADS + 1, dtype=F32) / ATTN_HEADS)
    bias = slopes[:, None, None] * dist.astype(F32)[None]
    first = jnp.where(band & (j >= WINDOW), bias, 1e30)
    later = jnp.where(band, bias, 1e30)
    return jnp.stack([first, later], axis=0)


def _attn(group, y, layer, k_cache, v_cache, p):
    nb, nc = group.n_seq, group.chunks_per_seq
    rows = y.shape[0]
    tq = group.attn_q_rows
    carry = group.is_prompt
    in_specs = [
        _chunk_spec(group, D_MODEL, Y_Q),
        _chunk_spec(group, ATTN_KV, Y_K),
        _chunk_spec(group, ATTN_KV, Y_V),
    ]
    args = [y, y, y]
    out_specs = [_chunk_spec(group, D_MODEL, 0)]
    out_shape = [jax.ShapeDtypeStruct((rows, D_MODEL), BF16)]
    scratch, aliases = [], {}
    if carry:
        bias_spec = pl.BlockSpec((1, ATTN_HEADS, tq, 2 * WINDOW),
                                 lambda b, c: (jnp.minimum(c, 1), 0, 0, 0))
        scratch = [pltpu.VMEM((2, ATTN_KV_HEADS, WINDOW, ATTN_KV), BF16),
                   pltpu.VMEM((2, ATTN_HEADS, WINDOW, ATTN_KV), BF16)]
    else:
        bias_spec = pl.BlockSpec((1, ATTN_HEADS, tq, 2 * WINDOW), lambda b, c: (1, 0, 0, 0))
        cache_spec = pl.BlockSpec((1, 1, WINDOW, ATTN_KV), lambda b, c: (layer, b, 0, 0))
        aliases = {len(args): 1, len(args) + 1: 2}
        in_specs += [cache_spec, cache_spec]
        args += [k_cache, v_cache]
        out_specs += [cache_spec, cache_spec]
        out_shape += [jax.ShapeDtypeStruct(k_cache.shape, F32), jax.ShapeDtypeStruct(v_cache.shape, F32)]
    in_specs += [_const_spec((1, LANES)), bias_spec]
    args += [p["attn_sinks"], _attn_bias_table(tq)]
    res = pl.pallas_call(
        functools.partial(_attn_kernel, tq, carry, group.last_chunk_valid),
        grid=(nb, nc),
        in_specs=in_specs,
        out_specs=out_specs,
        out_shape=out_shape,
        scratch_shapes=scratch,
        input_output_aliases=aliases,
        compiler_params=_cparams("arbitrary", "arbitrary"),
        name="attn_p" if group.is_prompt else "attn_s",
    )(*args)
    return res[0] if carry else res


def _gelu_tanh(x):
    return 0.5 * x * (1.0 + jnp.tanh(math.sqrt(2.0 / math.pi) * (x + 0.044715 * (x * x * x))))


def _gmlp_kernel(u_ref, v_ref, lg_ref, lb_ref, ws_ref, bs_ref, y_ref):
    T = CHUNK
    n_chunks = u_ref.shape[0] // T
    u = _gelu_tanh(u_ref[...].astype(F32))
    v = _gelu_tanh(v_ref[...].astype(F32))
    vc = v - jnp.mean(v, axis=-1, keepdims=True)
    v = vc * lax.rsqrt(jnp.mean(vc * vc, axis=-1, keepdims=True) + EPS) * lg_ref[...] + lb_ref[...]
    row = lax.broadcasted_iota(jnp.int32, (T, T), 0)
    col = lax.broadcasted_iota(jnp.int32, (T, T), 1)
    causal = row >= col
    vb = v.astype(BF16)
    mixed = [[None] * GM_GROUPS for _ in range(n_chunks)]
    for g in range(GM_GROUPS):
        w = jnp.where(causal, ws_ref[g], 0.0).astype(BF16)
        gs = slice(g * GM_GROUP_WIDTH, (g + 1) * GM_GROUP_WIDTH)
        v_g = jnp.concatenate([vb[c * T:(c + 1) * T, gs] for c in range(n_chunks)], axis=1)
        m_g = _dot(w, v_g)
        for c in range(n_chunks):
            mixed[c][g] = m_g[:, c * GM_GROUP_WIDTH:(c + 1) * GM_GROUP_WIDTH] + bs_ref[:, gs]
    mixed = jnp.concatenate([jnp.concatenate(m, axis=1) for m in mixed], axis=0)
    y_ref[...] = (u * mixed).astype(y_ref.dtype)


def _gmlp(group, y, p):
    tm = group.tm_rowwise
    rows = y.shape[0]
    tok = lambda piece: pl.BlockSpec((tm, D_MODEL), lambda i: (i, piece))
    yd = pl.pallas_call(
        _gmlp_kernel,
        grid=(rows // tm,),
        in_specs=[
            tok(Y_U), tok(Y_VGM),
            _const_spec((1, D_MODEL)),
            _const_spec((1, D_MODEL)),
            _const_spec((GM_GROUPS, CHUNK, CHUNK)),
            _const_spec((CHUNK, D_MODEL)),
        ],
        out_specs=tok(0),
        out_shape=jax.ShapeDtypeStruct((rows, D_MODEL), BF16),
        compiler_params=_cparams("arbitrary"),
        name="gmlp",
    )(y, y, p["gm_ln_g"], p["gm_ln_b"], p["gm_w_s"], p["gm_b_exp"])
    return yd, None


def _gmlp_s_kernel(n_diag, u_ref, v_ref, lg_ref, lb_ref, wd_ref, bs_ref, y_ref, v_out):
    u = _gelu_tanh(u_ref[...].astype(F32))
    v = _gelu_tanh(v_ref[...].astype(F32))
    vc = v - jnp.mean(v, axis=-1, keepdims=True)
    v = vc * lax.rsqrt(jnp.mean(vc * vc, axis=-1, keepdims=True) + EPS) * lg_ref[...] + lb_ref[...]
    mixed = bs_ref[...] + wd_ref[0] * v
    for d in range(1, n_diag):
        mixed = mixed + wd_ref[d] * pltpu.roll(v, d, 0)
    y_ref[...] = (u * mixed).astype(y_ref.dtype)
    v_out[...] = v


def _gmlp_s_tables(group, w_s, b_s):
    l = group.last_chunk_valid
    reps = group.tm_rowwise // SAMPLE_ROWS
    t = jnp.arange(SAMPLE_ROWS)
    expand = lambda per_group: jnp.tile(jnp.repeat(per_group.T, GM_GROUP_WIDTH, axis=1), (reps, 1))
    diags = []
    for d in range(l):
        src = t - d
        w = w_s[:, t, jnp.clip(src, 0, None)]
        diags.append(expand(jnp.where((src >= 0) & (t < l), w, 0.0)))
    return jnp.stack(diags, axis=0), expand(b_s[:, :SAMPLE_ROWS])


def _gmlp_s(group, y, p):
    rows = y.shape[0]
    tm = group.tm_rowwise
    l = group.last_chunk_valid
    tok = lambda piece: pl.BlockSpec((tm, D_MODEL), lambda i: (i, piece))
    wd, bs = _gmlp_s_tables(group, p["gm_w_s"], p["gm_b_s"])
    return pl.pallas_call(
        functools.partial(_gmlp_s_kernel, l),
        grid=(rows // tm,),
        in_specs=[tok(Y_U), tok(Y_VGM), _const_spec((1, D_MODEL)), _const_spec((1, D_MODEL)),
                  _const_spec((l, tm, D_MODEL)), _const_spec((tm, D_MODEL))],
        out_specs=[tok(0), tok(0)],
        out_shape=[jax.ShapeDtypeStruct((rows, D_MODEL), BF16), jax.ShapeDtypeStruct((rows, D_MODEL), F32)],
        compiler_params=_cparams("arbitrary"),
        name="gmlp_s",
    )(y, y, p["gm_ln_g"], p["gm_ln_b"], wd, bs)


def _merge_kernel(x_ref, ga_ref, ya_ref, yb_ref, yc_ref, yd_ref, g0_ref, g1_ref, g2_ref, g3_ref,
                  wb_ref, wo_ref, o_ref):
    merged = None
    for i, (y_ref, g_ref) in enumerate(((ya_ref, g0_ref), (yb_ref, g1_ref), (yc_ref, g2_ref),
                                        (yd_ref, g3_ref))):
        gate = jax.nn.sigmoid(g_ref[...].astype(F32))
        term = gate * _dot(y_ref[...], wb_ref[i])
        merged = term if merged is None else merged + term
    o = _dot(merged.astype(BF16), wo_ref[...])
    o_ref[...] = x_ref[...] + _mod_rows(ga_ref) * o


def _merge(group, x, mod, branches, y, p):
    tm = group.tm_merge
    rows = x.shape[0]
    tok = lambda piece: pl.BlockSpec((tm, D_MODEL), lambda i: (i, piece))
    return pl.pallas_call(
        _merge_kernel,
        grid=(rows // tm,),
        in_specs=[tok(0), _mod_spec(group, tm, 2)] + [tok(0)] * 4 + [tok(Y_G0 + i) for i in range(4)] + [
            pl.BlockSpec((N_BRANCH, D_MODEL, D_MODEL), lambda i: (0, 0, 0), pipeline_mode=pl.Buffered(1)),
            pl.BlockSpec((D_MODEL, D_MODEL), lambda i: (0, 0), pipeline_mode=pl.Buffered(1)),
        ],
        out_specs=tok(0),
        out_shape=jax.ShapeDtypeStruct((rows, D_MODEL), F32),
        compiler_params=_cparams("arbitrary"),
        name="merge",
    )(x, mod, *branches, y, y, y, y, p["w_branch"], p["w_o"])


def _ffn_kernel(seg8, l_valid, tiles_per_seq, final_norm,
                x_ref, g_ref, sc_ref, sh_ref, ga_ref, cp_ref, wa_ref, wg_ref, cw_ref, cb_ref, wd_ref,
                gf_ref, o_ref, cst_ref, carry_scr, xp_scr, act_scr):
    tm = x_ref.shape[0]
    i = pl.program_id(0)
    x = x_ref[...]
    h = _modnorm(x, g_ref[...], _mod_rows(sc_ref), _mod_rows(sh_ref)).astype(BF16)
    first = CARRY_ROWS - (FFN_CONV - 1)

    if not seg8:
        @pl.when(i % tiles_per_seq == 0)
        def _():
            carry_scr[...] = cp_ref[0]

    def conv_piece(w_ref, half, cs):
        off = half * D_FF
        up = _dot(h, w_ref[:, cs])
        col = slice(off + cs.start, off + cs.stop)
        if seg8:
            tb = tm // SAMPLE_ROWS
            xp_scr[:, 0:CARRY_ROWS, :] = cp_ref[:, :, col]
            xp_scr[:, CARRY_ROWS:CARRY_ROWS + SAMPLE_ROWS, :] = up.reshape(tb, SAMPLE_ROWS, MXU_WIDTH)
            acc = None
            for j in range(FFN_CONV):
                term = cw_ref[j:j + 1, col] * xp_scr[:, pl.ds(first + j, SAMPLE_ROWS), :]
                acc = term if acc is None else acc + term
            cst_ref[:, :, col] = xp_scr[:, pl.ds(CARRY_ROWS + l_valid - (FFN_CONV - 1), FFN_CONV - 1), :]
            return acc.reshape(tm, MXU_WIDTH) + cb_ref[:, col]
        xp_scr[0:CARRY_ROWS, :] = carry_scr[:, col]
        xp_scr[CARRY_ROWS:CARRY_ROWS + tm, :] = up
        acc = None
        for j in range(FFN_CONV):
            term = cw_ref[j:j + 1, col] * xp_scr[pl.ds(first + j, tm), :]
            acc = term if acc is None else acc + term
        carry_scr[:, col] = xp_scr[tm:tm + CARRY_ROWS, :]
        return acc + cb_ref[:, col]

    for k in range(D_FF // MXU_WIDTH):
        cs = slice(k * MXU_WIDTH, (k + 1) * MXU_WIDTH)
        a = conv_piece(wa_ref, 0, cs)
        g = conv_piece(wg_ref, 1, cs)
        act_scr[:, cs] = (_silu(a) * g).astype(BF16)

    if not seg8:
        @pl.when(i % tiles_per_seq == tiles_per_seq - 1)
        def _():
            cst_ref[0] = carry_scr[CARRY_ROWS - (FFN_CONV - 1):CARRY_ROWS, :]

    out = x + _mod_rows(ga_ref) * _dot(act_scr[...], wd_ref[...])
    if final_norm:
        out = out * lax.rsqrt(jnp.mean(out * out, axis=-1, keepdims=True) + EPS) * gf_ref[...]
    o_ref[...] = out


def _ffn(group, x, mod, conv_prev8, p, g_final, final_norm):
    tm = group.tm_ffn
    rows = x.shape[0]
    seg8 = not group.is_prompt
    tiles_per_seq = 1 if seg8 else group.seq_rows // tm
    single = dict(pipeline_mode=pl.Buffered(1))
    if seg8:
        tb = tm // SAMPLE_ROWS
        cp_spec = pl.BlockSpec((tb, CARRY_ROWS, 2 * D_FF), lambda i: (i, 0, 0))
        cst_spec = pl.BlockSpec((tb, FFN_CONV - 1, 2 * D_FF), lambda i: (i, 0, 0))
        xp_shape = (tb, CARRY_ROWS + SAMPLE_ROWS, MXU_WIDTH)
    else:
        cp_spec = pl.BlockSpec((1, CARRY_ROWS, 2 * D_FF), lambda i: (i // tiles_per_seq, 0, 0))
        cst_spec = pl.BlockSpec((1, FFN_CONV - 1, 2 * D_FF), lambda i: (i // tiles_per_seq, 0, 0))
        xp_shape = (CARRY_ROWS + tm, MXU_WIDTH)
    kern = functools.partial(_ffn_kernel, seg8, group.last_chunk_valid, tiles_per_seq, final_norm)
    return pl.pallas_call(
        kern,
        grid=(rows // tm,),
        in_specs=[
            pl.BlockSpec((tm, D_MODEL), lambda i: (i, 0)),
            _const_spec((1, D_MODEL)),
            _mod_spec(group, tm, 4),
            _mod_spec(group, tm, 3),
            _mod_spec(group, tm, 5),
            cp_spec,
            pl.BlockSpec((D_MODEL, D_FF), lambda i: (0, 0), **single),
            pl.BlockSpec((D_MODEL, D_FF), lambda i: (0, 0), **single),
            _const_spec((FFN_CONV, 2 * D_FF)),
            _const_spec((1, 2 * D_FF)),
            pl.BlockSpec((D_FF, D_MODEL), lambda i: (0, 0), **single),
            _const_spec((1, D_MODEL)),
        ],
        out_specs=[pl.BlockSpec((tm, D_MODEL), lambda i: (i, 0)), cst_spec],
        out_shape=[
            jax.ShapeDtypeStruct((rows, D_MODEL), F32),
            jax.ShapeDtypeStruct((group.n_seq, FFN_CONV - 1, 2 * D_FF), F32),
        ],
        scratch_shapes=[
            pltpu.VMEM((CARRY_ROWS, 2 * D_FF), F32),
            pltpu.VMEM(xp_shape, F32),
            pltpu.VMEM((tm, D_FF), BF16),
        ],
        compiler_params=_cparams("arbitrary"),
        name="ffn",
    )(x, p["g_norm_ffn"], mod, mod, mod, conv_prev8, p["w_up_a"], p["w_up_g"], p["ffn_conv_w"],
      p["ffn_conv_b"], p["w_down"], g_final)


class _Group:
    def __init__(self, is_prompt, n_seq, seq_len):
        self.is_prompt = is_prompt
        self.n_seq = n_seq
        if is_prompt:
            self.seq_rows = seq_len
            self.chunk_rows = CHUNK
            self.chunks_per_seq = seq_len // CHUNK
            self.last_chunk_valid = CHUNK
            self.per_token_mod = False
            self.tm_inproj = 1024
            self.tm_merge = 512
            self.tm_ffn = 512
            self.attn_q_rows = CHUNK
            self.tm_rowwise = 512
        else:
            self.attn_q_rows = BF16_SUBLANES
            self.tm_rowwise = 256
            self.seq_rows = SAMPLE_ROWS
            self.chunk_rows = SAMPLE_ROWS
            self.chunks_per_seq = 1
            self.last_chunk_valid = seq_len
            self.per_token_mod = True
            self.tm_inproj = 512
            self.tm_merge = 512
            self.tm_ffn = 128


def _pad_front(state, rows):
    return jnp.pad(state, ((0, 0), (rows - state.shape[1], 0), (0, 0)))


def _layer_params(i, w):
    pad_heads = lambda v: jnp.pad(v, (0, LANES - SSD_HEADS)).reshape(1, LANES)
    w_in = w["w_in"][i]
    head_of_channel = jnp.arange(SSD_D_INNER) // SSD_HEAD_DIM
    expand = (jnp.arange(LANES)[:, None] == head_of_channel[None, :]).astype(BF16)
    return {
        "g_norm_mix": w["g_norm_mix"][i].reshape(1, D_MODEL),
        "w_main": jnp.concatenate([w_in[:, a:b] for a, b in _W_IN_PIECES], axis=1).astype(BF16),
        "w_dt": jnp.pad(w_in[:, _IN_OFF_DT:_IN_OFF_DT + SSD_HEADS],
                        ((0, 0), (0, LANES - SSD_HEADS))).astype(BF16),
        "ssd_conv_w": w["ssd_conv_w"][i],
        "ssd_conv_b": w["ssd_conv_b"][i].reshape(1, SSD_CONV_DIM),
        "ssd_dt_bias": pad_heads(w["ssd_dt_bias"][i]),
        "ssd_a_log": pad_heads(w["ssd_a_log"][i]),
        "ssd_d": jnp.repeat(w["ssd_d"][i], SSD_HEAD_DIM).reshape(1, SSD_D_INNER),
        "ssd_norm_g": w["ssd_norm_g"][i].reshape(1, SSD_D_INNER),
        "ssd_expand": expand,
        "sc_conv_w": w["sc_conv_w"][i],
        "attn_sinks": jnp.pad(w["attn_sinks"][i], (0, LANES - ATTN_HEADS)).reshape(1, LANES),
        "gm_ln_g": w["gm_ln_g"][i].reshape(1, D_MODEL),
        "gm_ln_b": w["gm_ln_b"][i].reshape(1, D_MODEL),
        "gm_w_s": w["gm_w_s"][i],
        "gm_b_s": w["gm_b_s"][i],
        "gm_b_exp": jnp.repeat(w["gm_b_s"][i].T, GM_GROUP_WIDTH, axis=1),
        "w_branch": w["w_branch"][i].astype(BF16),
        "w_o": w["w_o"][i].astype(BF16),
        "g_norm_ffn": w["g_norm_ffn"][i].reshape(1, D_MODEL),
        "w_up_a": w["ffn_w_up"][i][:, :D_FF].astype(BF16),
        "w_up_g": w["ffn_w_up"][i][:, D_FF:].astype(BF16),
        "ffn_conv_w": w["ffn_conv_w"][i],
        "ffn_conv_b": w["ffn_conv_b"][i].reshape(1, 2 * D_FF),
        "w_down": w["ffn_w_down"][i].astype(BF16),
    }


def _run_group(group, x, mod_all, states, params, g_final):
    n = group.n_seq
    outs = []
    if not group.is_prompt:
        ssm_all = states[0]
        k_all = states[3].reshape(DEPTH, n, WINDOW, ATTN_KV)
        v_all = states[4].reshape(DEPTH, n, WINDOW, ATTN_KV)
    for i in range(DEPTH):
        p = params[i]
        mod = mod_all[i]
        y, dtr = _inproj(group, x, mod, p["g_norm_mix"], p["w_main"], p["w_dt"])
        if group.is_prompt:
            ssm0 = jnp.zeros((n, SSD_HEADS, SSD_HEAD_DIM, SSD_STATE), F32)
            ssdc0 = jnp.zeros((n, CARRY_ROWS, SSD_CONV_DIM), F32)
            scc0 = jnp.zeros((n, CARRY_ROWS, D_MODEL), F32)
            ffc0 = jnp.zeros((n, FFN_CONV - 1, 2 * D_FF), F32)
            ya, ssm1, ssdc1 = _ssd(group, y, dtr, ssm0, ssdc0, p)
            yb, scc1 = _sconv(group, y, scc0, p)
            yd, v_rows = _gmlp(group, y, p)
            yc = _attn(group, y, i, None, None, p)
        else:
            ffc0 = states[5][i]
            ya, ssm_all, ssdc1 = _ssd_s(group, y, dtr, i, ssm_all, states[1][i], p)
            yb, scc1 = _sconv_s(group, y, states[2][i], p)
            yd, v_rows = _gmlp_s(group, y, p)
            yc, k_all, v_all = _attn(group, y, i, k_all, v_all, p)
        x = _merge(group, x, mod, (ya, yb, yc, yd), y, p)
        x, ffc1 = _ffn(group, x, mod, _pad_front(ffc0, CARRY_ROWS), p, g_final, i == DEPTH - 1)

        kcol = slice(Y_K * SSD_BC, (Y_K + 1) * SSD_BC)
        vcol = slice(Y_V * SSD_BC, (Y_V + 1) * SSD_BC)
        if group.is_prompt:
            y3 = y.reshape(n, group.seq_rows, Y_COLS)
            k1 = y3[:, group.seq_rows - WINDOW:, kcol].astype(F32)
            v1 = y3[:, group.seq_rows - WINDOW:, vcol].astype(F32)
            k1 = k1.reshape(n, WINDOW, ATTN_KV_HEADS, ATTN_HEAD_DIM)
            v1 = v1.reshape(n, WINDOW, ATTN_KV_HEADS, ATTN_HEAD_DIM)
            outs.append((ssm1, ssdc1, scc1, k1, v1, ffc1))
        else:
            gv = v_rows.reshape(n, SAMPLE_ROWS, D_MODEL)[:, :group.last_chunk_valid]
            outs.append((ssdc1, scc1, ffc1, gv))
    stacked = tuple(jnp.stack([o[j] for o in outs], axis=0) for j in range(len(outs[0])))
    if not group.is_prompt:
        ssdc, scc, ffc, gv = stacked
        cache_shape = (DEPTH, n, WINDOW, ATTN_KV_HEADS, ATTN_HEAD_DIM)
        stacked = (ssm_all, ssdc, scc, k_all.reshape(cache_shape), v_all.reshape(cache_shape), ffc, gv)
    return x, stacked


def kernel(x_prompt, x_sample, c_prompt, c_sample, state_ssm, state_ssd_conv, state_sc_conv, cache_k,
           cache_v, state_ffn_conv, w_ada, b_ada, g_norm_mix, w_in, ssd_conv_w, ssd_conv_b, ssd_dt_bias,
           ssd_a_log, ssd_d, ssd_norm_g, sc_conv_w, attn_sinks, gm_ln_g, gm_ln_b, gm_w_s, gm_b_s,
           w_branch, w_o, g_norm_ffn, ffn_w_up, ffn_conv_w, ffn_conv_b, ffn_w_down, g_final):
    weights = dict(g_norm_mix=g_norm_mix, w_in=w_in, ssd_conv_w=ssd_conv_w, ssd_conv_b=ssd_conv_b,
                   ssd_dt_bias=ssd_dt_bias, ssd_a_log=ssd_a_log, ssd_d=ssd_d, ssd_norm_g=ssd_norm_g,
                   sc_conv_w=sc_conv_w, attn_sinks=attn_sinks, gm_ln_g=gm_ln_g, gm_ln_b=gm_ln_b,
                   gm_w_s=gm_w_s, gm_b_s=gm_b_s, w_branch=w_branch, w_o=w_o, g_norm_ffn=g_norm_ffn,
                   ffn_w_up=ffn_w_up, ffn_conv_w=ffn_conv_w, ffn_conv_b=ffn_conv_b, ffn_w_down=ffn_w_down)
    params = [_layer_params(i, weights) for i in range(DEPTH)]
    gf = g_final.reshape(1, D_MODEL)

    nb_p, len_p, _ = x_prompt.shape
    nb_s, len_s, _ = x_sample.shape
    assert len_p % CHUNK == 0 and len_s <= SAMPLE_ROWS and len_s >= SSD_CONV - 1
    prompt = _Group(True, nb_p, len_p)
    sample = _Group(False, nb_s, len_s)

    c_rows = nb_p + nb_s
    c_pad = -c_rows % SAMPLE_ROWS
    c_all = jnp.pad(jnp.concatenate([c_prompt, c_sample], axis=0), ((0, c_pad), (0, 0)))
    mod = _ada(c_all, w_ada, b_ada)
    mod_p = mod[:, :nb_p].reshape(DEPTH, nb_p, 1, 6 * D_MODEL)
    mod_s = jnp.repeat(mod[:, nb_p:nb_p + nb_s], SAMPLE_ROWS, axis=1)

    xp = x_prompt.reshape(nb_p * len_p, D_MODEL)
    xs = jnp.pad(x_sample, ((0, 0), (0, SAMPLE_ROWS - len_s), (0, 0))).reshape(nb_s * SAMPLE_ROWS, D_MODEL)

    y_p, st_p = _run_group(prompt, xp, mod_p, (None,) * 6, params, gf)
    y_s, st_s = _run_group(sample, xs, mod_s,
                           (state_ssm, state_ssd_conv, state_sc_conv, cache_k, cache_v, state_ffn_conv),
                           params, gf)
    y_prompt = y_p.reshape(nb_p, len_p, D_MODEL)
    y_sample = y_s.reshape(nb_s, SAMPLE_ROWS, D_MODEL)[:, :len_s]
    return (y_prompt, y_sample) + st_p + st_s
```

```python
import functools
import math

import jax
import jax.numpy as jnp
from jax import lax
from jax.experimental import pallas as pl
from jax.experimental.pallas import tpu as pltpu

F32 = jnp.float32
BF16 = jnp.bfloat16

D_MODEL = 1024
DEPTH = 4
N_BRANCH = 4
SSD_HEADS = 16
SSD_HEAD_DIM = 64
SSD_GROUPS = 4
SSD_STATE = 64
SSD_CONV = 4
SSD_D_INNER = 1024
SSD_BC = SSD_GROUPS * SSD_STATE
SSD_CONV_DIM = SSD_D_INNER + 2 * SSD_BC
SC_CONV = 3
ATTN_HEADS = 16
ATTN_KV_HEADS = 4
ATTN_GROUP = 4
ATTN_HEAD_DIM = 64
ATTN_KV = ATTN_KV_HEADS * ATTN_HEAD_DIM
WINDOW = 128
GM_GROUPS = 8
GM_GROUP_WIDTH = 128
D_FF = 2816
FFN_CONV = 3
EPS = 1e-6

CHUNK = 128
SAMPLE_ROWS = 8
CARRY_ROWS = 8
LANES = 128
BF16_SUBLANES = 16
MXU_WIDTH = 256
VMEM_LIMIT = 56 * 1024 * 1024

(Y_Z, Y_XS, Y_BG, Y_CG, Y_XSC, Y_Q, Y_U, Y_VGM, Y_G0) = range(9)
Y_WIDE = 12
Y_SSDB, Y_SSDC, Y_K, Y_V = (Y_WIDE * 4 + i for i in range(4))
Y_COLS = Y_WIDE * D_MODEL + 4 * SSD_BC
INPROJ_TN = Y_COLS // 4

_IN_OFF_XBC = 1024
_IN_OFF_DT = 2560
_IN_OFF_BCX = 2576
_IN_OFF_Q = 5648
_IN_OFF_K = 6672
_IN_OFF_V = 6928
_IN_OFF_UV = 7184
_IN_OFF_GATES = 9232
_W_IN_PIECES = (
    (0, 1024),
    (_IN_OFF_XBC, _IN_OFF_XBC + 1024),
    (_IN_OFF_BCX, _IN_OFF_BCX + 1024),
    (_IN_OFF_BCX + 1024, _IN_OFF_BCX + 2048),
    (_IN_OFF_BCX + 2048, _IN_OFF_BCX + 3072),
    (_IN_OFF_Q, _IN_OFF_Q + 1024),
    (_IN_OFF_UV, _IN_OFF_UV + 1024),
    (_IN_OFF_UV + 1024, _IN_OFF_UV + 2048),
    (_IN_OFF_GATES, _IN_OFF_GATES + 4096),
    (_IN_OFF_XBC + 1024, _IN_OFF_XBC + 1280),
    (_IN_OFF_XBC + 1280, _IN_OFF_XBC + 1536),
    (_IN_OFF_K, _IN_OFF_K + 256),
    (_IN_OFF_V, _IN_OFF_V + 256),
)


def _cparams(*sem):
    return pltpu.CompilerParams(dimension_semantics=sem, vmem_limit_bytes=VMEM_LIMIT)


def _const_spec(shape):
    zeros = (0,) * len(shape)
    return pl.BlockSpec(shape, lambda *_: zeros)


def _silu(x):
    return x * jax.nn.sigmoid(x)


def _mod_rows(ref):
    return ref[0] if len(ref.shape) == 3 else ref[...]


def _modnorm(x, g, sc, sh):
    r = lax.rsqrt(jnp.mean(x * x, axis=-1, keepdims=True) + EPS)
    return (x * r * g) * (1.0 + sc) + sh


def _split3(x):
    hi = x.astype(BF16)
    r1 = x - hi.astype(F32)
    mid = r1.astype(BF16)
    lo = (r1 - mid.astype(F32)).astype(BF16)
    return hi, mid, lo


def _dot(a, b):
    return jnp.dot(a, b, preferred_element_type=F32)


def _dot_nt(a, b):
    return lax.dot_general(a, b, (((1,), (1,)), ((), ())), preferred_element_type=F32)


def _dot_tn(a, b):
    return lax.dot_general(a, b, (((0,), (0,)), ((), ())), preferred_element_type=F32)


def _dot_exact_lhs(a_bf16, x):
    hi, mid, lo = _split3(x)
    return _dot(a_bf16, hi) + _dot(a_bf16, mid) + _dot(a_bf16, lo)


def _dot_exact_rhs(x, e_bf16):
    hi, mid, lo = _split3(x)
    return _dot(hi, e_bf16) + _dot(mid, e_bf16) + _dot(lo, e_bf16)


def _chunk_rows(ref, rows):
    x = ref[...].astype(F32)
    if x.shape[0] == rows:
        return x
    pad = jnp.zeros((rows - x.shape[0], x.shape[1]), F32)
    return jnp.concatenate([x, pad], axis=0)


def _ada_kernel(c_ref, w_ref, b_ref, o_ref):
    s = _silu(c_ref[...]).astype(BF16)
    o_ref[0] = _dot(s, w_ref[0].astype(BF16)) + b_ref[0]


def _ada(c_all, w_ada, b_ada):
    rows = c_all.shape[0]
    n_tiles = w_ada.shape[-1] // D_MODEL
    return pl.pallas_call(
        _ada_kernel,
        grid=(DEPTH, n_tiles),
        in_specs=[
            _const_spec((rows, D_MODEL)),
            pl.BlockSpec((1, D_MODEL, D_MODEL), lambda l, n: (l, 0, n)),
            pl.BlockSpec((1, 1, D_MODEL), lambda l, n: (l, 0, n)),
        ],
        out_specs=pl.BlockSpec((1, rows, D_MODEL), lambda l, n: (l, 0, n)),
        out_shape=jax.ShapeDtypeStruct((DEPTH, rows, 6 * D_MODEL), F32),
        compiler_params=_cparams("arbitrary", "arbitrary"),
        name="ada",
    )(c_all, w_ada, b_ada.reshape(DEPTH, 1, 6 * D_MODEL))


def _inproj_kernel(x_ref, g_ref, sc_ref, sh_ref, w_ref, wdt_ref, y_ref, dtr_ref, h_scr):
    @pl.when(pl.program_id(1) == 0)
    def _():
        h = _modnorm(x_ref[...], g_ref[...], _mod_rows(sc_ref), _mod_rows(sh_ref)).astype(BF16)
        h_scr[...] = h
        dtr_ref[...] = _dot(h, wdt_ref[...])

    y_ref[...] = _dot(h_scr[...], w_ref[...]).astype(y_ref.dtype)


def _mod_spec(group, tm, piece):
    if group.per_token_mod:
        return pl.BlockSpec((tm, D_MODEL), lambda i, *_: (i, piece))
    tiles_per_seq = group.seq_rows // tm
    return pl.BlockSpec((1, 1, D_MODEL), lambda i, *_: (i // tiles_per_seq, 0, piece))


def _inproj(group, x, mod, g, w_main, w_dt):
    tm = group.tm_inproj
    rows = x.shape[0]
    tn = INPROJ_TN
    return pl.pallas_call(
        _inproj_kernel,
        grid=(rows // tm, Y_COLS // tn),
        in_specs=[
            pl.BlockSpec((tm, D_MODEL), lambda i, n: (i, 0)),
            _const_spec((1, D_MODEL)),
            _mod_spec(group, tm, 1),
            _mod_spec(group, tm, 0),
            pl.BlockSpec((D_MODEL, tn), lambda i, n: (0, n)),
            _const_spec((D_MODEL, LANES)),
        ],
        out_specs=[
            pl.BlockSpec((tm, tn), lambda i, n: (i, n)),
            pl.BlockSpec((tm, LANES), lambda i, n: (i, 0)),
        ],
        out_shape=[
            jax.ShapeDtypeStruct((rows, Y_COLS), BF16),
            jax.ShapeDtypeStruct((rows, LANES), F32),
        ],
        scratch_shapes=[pltpu.VMEM((tm, D_MODEL), BF16)],
        compiler_params=_cparams("arbitrary", "arbitrary"),
        name="inproj",
    )(x, g, mod, mod, w_main, w_dt)


def _causal_conv(xp_ref, w_ref, width, rows):
    first = CARRY_ROWS - (width - 1)
    acc = w_ref[0:1, :] * xp_ref[pl.ds(first, rows), :]
    for j in range(1, width):
        acc = acc + w_ref[j:j + 1, :] * xp_ref[pl.ds(first + j, rows), :]
    return acc


def _chunk_spec(group, width, piece):
    nc = group.chunks_per_seq
    return pl.BlockSpec((group.chunk_rows, width), lambda b, c: (b * nc + c, piece))


HEADS_PER_GROUP = SSD_HEADS // SSD_GROUPS
GROUP_WIDTH = HEADS_PER_GROUP * SSD_HEAD_DIM


def _softplus(x):
    return jnp.maximum(x, 0.0) + jnp.log1p(jnp.exp(-jnp.abs(x)))


def _ssd_intra_group(g, cs, cs_t, cmat, bmat, x_b, causal):
    T = cs.shape[0]
    gs = slice(g * SSD_STATE, (g + 1) * SSD_STATE)
    cb = _dot_nt(cmat[:, gs], bmat[:, gs])
    low_half = lax.broadcasted_iota(jnp.int32, (T, 2 * SSD_HEAD_DIM), 1) < SSD_HEAD_DIM
    pair_out = []
    for pr in range(HEADS_PER_GROUP // 2):
        h0 = g * HEADS_PER_GROUP + 2 * pr
        xp = x_b[:, h0 * SSD_HEAD_DIM:(h0 + 2) * SSD_HEAD_DIM]
        acc = None
        for k in range(2):
            h = h0 + k
            diff = cs[:, h:h + 1] - cs_t[h:h + 1, :]
            lmat = jnp.where(causal, jnp.exp(jnp.where(causal, diff, 0.0)), 0.0)
            m = (cb * lmat).astype(BF16)
            keep = low_half if k == 0 else jnp.logical_not(low_half)
            part = _dot(m, jnp.where(keep, xp, jnp.zeros_like(xp)))
            acc = part if acc is None else acc + part
        pair_out.append(acc)
    return jnp.concatenate(pair_out, axis=1)


def _ssd_finish(y, xs, z, dsk, ng):
    y = (y + xs * dsk) * _silu(z)
    gw = SSD_D_INNER // SSD_GROUPS
    normed = []
    for g in range(SSD_GROUPS):
        yg = y[:, g * gw:(g + 1) * gw]
        normed.append(yg * lax.rsqrt(jnp.mean(yg * yg, axis=-1, keepdims=True) + EPS))
    return jnp.concatenate(normed, axis=1) * ng


def _seq_pos(rows, width):
    return lax.broadcasted_iota(jnp.int32, (rows, width), 0) % SAMPLE_ROWS


def _shift_rows(x, d, head, pos):
    return jnp.where(pos >= d, pltpu.roll(x, d, 0), head)


def _ssd_s_kernel(l_valid, z_ref, xs_ref, bm_ref, cm_ref, dtr_ref, s0_ref, _sfin_hbm, h1_ref, h2_ref, h3_ref,
                  cw_ref, cb_ref, dtb_ref, alog_ref, dsk_ref, ng_ref, e_ref,
                  y_ref, sfin_ref, raw_ref):
    T = CHUNK
    n_seq = T // SAMPLE_ROWS
    raw = jnp.concatenate([xs_ref[...].astype(F32), bm_ref[...].astype(F32),
                           cm_ref[...].astype(F32)], axis=1)
    raw_ref[...] = raw
    pos = _seq_pos(T, SSD_CONV_DIM)
    conv = (cb_ref[...] + cw_ref[3:4, :] * raw
            + cw_ref[2:3, :] * _shift_rows(raw, 1, h1_ref[...], pos)
            + cw_ref[1:2, :] * _shift_rows(raw, 2, h2_ref[...], pos)
            + cw_ref[0:1, :] * _shift_rows(raw, 3, h3_ref[...], pos))
    xbc = _silu(conv)
    xs = xbc[:, 0:SSD_D_INNER]
    bmat = xbc[:, SSD_D_INNER:SSD_D_INNER + SSD_BC].astype(BF16)
    cmat = xbc[:, SSD_D_INNER + SSD_BC:SSD_CONV_DIM].astype(BF16)

    dt = _softplus(dtr_ref[...] + dtb_ref[...])
    dt = jnp.where(_seq_pos(T, LANES) < l_valid, dt, 0.0)
    dta = dt * (-jnp.exp(alog_ref[...]))
    row = lax.broadcasted_iota(jnp.int32, (T, T), 0)
    col = lax.broadcasted_iota(jnp.int32, (T, T), 1)
    causal = (row >= col) & (row // SAMPLE_ROWS == col // SAMPLE_ROWS)
    tri = jnp.where(causal, 1.0, 0.0).astype(BF16)
    cs = _dot_exact_lhs(tri, dta)
    last = jnp.where(col == (row // SAMPLE_ROWS) * SAMPLE_ROWS + (SAMPLE_ROWS - 1), 1.0, 0.0).astype(BF16)
    total_rows = _dot_exact_lhs(last, cs)
    cs_t = cs.T
    e = e_ref[...]
    dt_x = _dot_exact_rhs(dt, e)
    ecs_x = _dot_exact_rhs(jnp.exp(cs), e)
    dec_x = _dot_exact_rhs(jnp.exp(total_rows - cs), e)
    x_dt = xs * dt_x
    x_b = x_dt.astype(BF16)
    x_dec_t = (x_dt * dec_x).T.astype(BF16)

    seq_of_row_wide = lax.broadcasted_iota(jnp.int32, (T, GROUP_WIDTH), 0) // SAMPLE_ROWS
    seq_of_row = lax.broadcasted_iota(jnp.int32, (T, SSD_STATE), 0) // SAMPLE_ROWS
    y_parts = []
    for g in range(SSD_GROUPS):
        gs = slice(g * SSD_STATE, (g + 1) * SSD_STATE)
        c_g, b_g = cmat[:, gs], bmat[:, gs]
        xt_g = x_dec_t[g * GROUP_WIDTH:(g + 1) * GROUP_WIDTH, :]
        y_off = jnp.zeros((T, GROUP_WIDTH), F32)
        for b in range(n_seq):
            s_bg = s0_ref[0, b, g * HEADS_PER_GROUP:(g + 1) * HEADS_PER_GROUP].reshape(GROUP_WIDTH, SSD_STATE)
            y_off = jnp.where(seq_of_row_wide == b, _dot_nt(c_g, s_bg.astype(BF16)), y_off)
            ds = _dot(xt_g, jnp.where(seq_of_row == b, b_g, jnp.zeros_like(b_g)))
            decay = jnp.exp(cs[b * SAMPLE_ROWS + SAMPLE_ROWS - 1:(b + 1) * SAMPLE_ROWS, :])
            for r in range(HEADS_PER_GROUP):
                h = g * HEADS_PER_GROUP + r
                sfin_ref[0, b, h] = (s0_ref[0, b, h] * decay[:, h:h + 1]
                                  + ds[r * SSD_HEAD_DIM:(r + 1) * SSD_HEAD_DIM, :])
        y_diag = _ssd_intra_group(g, cs, cs_t, cmat, bmat, x_b, causal)
        y_parts.append(y_diag + y_off * ecs_x[:, g * GROUP_WIDTH:(g + 1) * GROUP_WIDTH])
    y = _ssd_finish(jnp.concatenate(y_parts, axis=1), xs, z_ref[...].astype(F32), dsk_ref[...], ng_ref[...])
    y_ref[...] = y.astype(y_ref.dtype)


def _history_tiles(state, width):
    n, k, c = state.shape
    tiles = []
    for d in range(1, width):
        rows = [state[:, k + t - d] if t < d else jnp.zeros((n, c), state.dtype)
                for t in range(SAMPLE_ROWS)]
        tiles.append(jnp.stack(rows, axis=1).reshape(n * SAMPLE_ROWS, c))
    return tiles


def _ssd_s(group, y, dtr, layer, s_all, s_new_all, conv_state, p):
    rows = y.shape[0]
    n_seq = CHUNK // SAMPLE_ROWS
    tok = lambda width, piece: pl.BlockSpec((CHUNK, width), lambda i: (i, piece))
    state_spec = pl.BlockSpec((1, n_seq, SSD_HEADS, SSD_HEAD_DIM, SSD_STATE), lambda i: (layer, i, 0, 0, 0))
    h1, h2, h3 = _history_tiles(conv_state, SSD_CONV)
    ya, sfin, raw = pl.pallas_call(
        functools.partial(_ssd_s_kernel, group.last_chunk_valid),
        grid=(rows // CHUNK,),
        in_specs=[
            tok(D_MODEL, Y_Z), tok(D_MODEL, Y_XS), tok(SSD_BC, Y_SSDB), tok(SSD_BC, Y_SSDC),
            tok(LANES, 0), state_spec, pl.BlockSpec(memory_space=pl.ANY),
            tok(SSD_CONV_DIM, 0), tok(SSD_CONV_DIM, 0), tok(SSD_CONV_DIM, 0),
            _const_spec((SSD_CONV, SSD_CONV_DIM)), _const_spec((1, SSD_CONV_DIM)),
            _const_spec((1, LANES)), _const_spec((1, LANES)),
            _const_spec((1, SSD_D_INNER)), _const_spec((1, SSD_D_INNER)),
            _const_spec((LANES, SSD_D_INNER)),
        ],
        out_specs=[tok(D_MODEL, 0), state_spec, tok(SSD_CONV_DIM, 0)],
        out_shape=[
            jax.ShapeDtypeStruct((rows, D_MODEL), BF16),
            jax.ShapeDtypeStruct(s_all.shape, F32),
            jax.ShapeDtypeStruct((rows, SSD_CONV_DIM), F32),
        ],
        input_output_aliases={6: 1},
        compiler_params=_cparams("arbitrary"),
        name="ssd_s",
    )(y, y, y, y, dtr, s_all, s_new_all, h1, h2, h3,
      p["ssd_conv_w"], p["ssd_conv_b"], p["ssd_dt_bias"], p["ssd_a_log"], p["ssd_d"],
      p["ssd_norm_g"], p["ssd_expand"])
    l = group.last_chunk_valid
    conv_new = raw.reshape(group.n_seq, SAMPLE_ROWS, SSD_CONV_DIM)[:, l - (SSD_CONV - 1):l]
    return ya, sfin, conv_new


def _ssd_kernel(l_valid, n_chunks,
                z_ref, xs_ref, bm_ref, cm_ref, dtr_ref, s0_ref, cp_ref,
                cw_ref, cb_ref, dtb_ref, alog_ref, dsk_ref, ng_ref, e_ref,
                y_ref, sfin_ref, cst_ref,
                s_scr, xp_scr):
    T = CHUNK
    c = pl.program_id(1)

    @pl.when(c == 0)
    def _():
        s_scr[...] = s0_ref[0]
        xp_scr[0:CARRY_ROWS, :] = cp_ref[0]

    xp_scr[CARRY_ROWS:CARRY_ROWS + T, 0:SSD_D_INNER] = _chunk_rows(xs_ref, T)
    xp_scr[CARRY_ROWS:CARRY_ROWS + T, SSD_D_INNER:SSD_D_INNER + SSD_BC] = _chunk_rows(bm_ref, T)
    xp_scr[CARRY_ROWS:CARRY_ROWS + T, SSD_D_INNER + SSD_BC:SSD_CONV_DIM] = _chunk_rows(cm_ref, T)
    xbc = _silu(_causal_conv(xp_scr, cw_ref, SSD_CONV, T) + cb_ref[...])
    xs = xbc[:, 0:SSD_D_INNER]
    bmat = xbc[:, SSD_D_INNER:SSD_D_INNER + SSD_BC].astype(BF16)
    cmat = xbc[:, SSD_D_INNER + SSD_BC:SSD_CONV_DIM].astype(BF16)

    dt = _softplus(_chunk_rows(dtr_ref, T) + dtb_ref[...])
    row = lax.broadcasted_iota(jnp.int32, (T, T), 0)
    col = lax.broadcasted_iota(jnp.int32, (T, T), 1)
    if l_valid < T:
        rvalid = lax.broadcasted_iota(jnp.int32, (T, LANES), 0) < l_valid
        dt = jnp.where(rvalid, dt, 0.0)
    dta = dt * (-jnp.exp(alog_ref[...]))
    causal = row >= col
    tri = jnp.where(causal, 1.0, 0.0).astype(BF16)
    cs = _dot_exact_lhs(tri, dta)
    cs_t = cs.T
    total = cs[T - 1:T, :]
    e = e_ref[...]
    dt_x = _dot_exact_rhs(dt, e)
    ecs_x = _dot_exact_rhs(jnp.exp(cs), e)
    dec_x = _dot_exact_rhs(jnp.exp(total - cs), e)
    chunk_decay = jnp.exp(total)

    x_dt = xs * dt_x
    x_b = x_dt.astype(BF16)
    x_dec = (x_dt * dec_x).astype(BF16)

    y_parts = []
    for g in range(SSD_GROUPS):
        gs = slice(g * SSD_STATE, (g + 1) * SSD_STATE)
        s_g = s_scr[g * HEADS_PER_GROUP:(g + 1) * HEADS_PER_GROUP].reshape(GROUP_WIDTH, SSD_STATE)
        y_diag = _ssd_intra_group(g, cs, cs_t, cmat, bmat, x_b, causal)
        y_off = _dot_nt(cmat[:, gs], s_g.astype(BF16))
        ds_g = _dot_tn(x_dec[:, g * GROUP_WIDTH:(g + 1) * GROUP_WIDTH], bmat[:, gs])
        y_parts.append(y_diag + y_off * ecs_x[:, g * GROUP_WIDTH:(g + 1) * GROUP_WIDTH])
        for r in range(HEADS_PER_GROUP):
            h = g * HEADS_PER_GROUP + r
            s_scr[h] = (s_scr[h] * chunk_decay[:, h:h + 1]
                        + ds_g[r * SSD_HEAD_DIM:(r + 1) * SSD_HEAD_DIM, :])

    y = _ssd_finish(jnp.concatenate(y_parts, axis=1), xs, _chunk_rows(z_ref, T), dsk_ref[...], ng_ref[...])
    y_ref[...] = y[0:y_ref.shape[0], :].astype(y_ref.dtype)

    @pl.when(c == n_chunks - 1)
    def _():
        sfin_ref[0] = s_scr[...]
        cst_ref[0] = xp_scr[pl.ds(CARRY_ROWS + l_valid - (SSD_CONV - 1), SSD_CONV - 1), :]

    if n_chunks > 1:
        xp_scr[0:CARRY_ROWS, :] = xp_scr[T:T + CARRY_ROWS, :]


def _ssd(group, y, dtr, s0, conv_prev8, p):
    nb, nc = group.n_seq, group.chunks_per_seq
    rows = y.shape[0]
    kern = functools.partial(_ssd_kernel, group.last_chunk_valid, nc)
    return pl.pallas_call(
        kern,
        grid=(nb, nc),
        in_specs=[
            _chunk_spec(group, D_MODEL, Y_Z),
            _chunk_spec(group, D_MODEL, Y_XS),
            _chunk_spec(group, SSD_BC, Y_SSDB),
            _chunk_spec(group, SSD_BC, Y_SSDC),
            _chunk_spec(group, LANES, 0),
            pl.BlockSpec((1, SSD_HEADS, SSD_HEAD_DIM, SSD_STATE), lambda b, c: (b, 0, 0, 0)),
            pl.BlockSpec((1, CARRY_ROWS, SSD_CONV_DIM), lambda b, c: (b, 0, 0)),
            _const_spec((SSD_CONV, SSD_CONV_DIM)),
            _const_spec((1, SSD_CONV_DIM)),
            _const_spec((1, LANES)),
            _const_spec((1, LANES)),
            _const_spec((1, SSD_D_INNER)),
            _const_spec((1, SSD_D_INNER)),
            _const_spec((LANES, SSD_D_INNER)),
        ],
        out_specs=[
            _chunk_spec(group, D_MODEL, 0),
            pl.BlockSpec((1, SSD_HEADS, SSD_HEAD_DIM, SSD_STATE), lambda b, c: (b, 0, 0, 0)),
            pl.BlockSpec((1, SSD_CONV - 1, SSD_CONV_DIM), lambda b, c: (b, 0, 0)),
        ],
        out_shape=[
            jax.ShapeDtypeStruct((rows, D_MODEL), BF16),
            jax.ShapeDtypeStruct((nb, SSD_HEADS, SSD_HEAD_DIM, SSD_STATE), F32),
            jax.ShapeDtypeStruct((nb, SSD_CONV - 1, SSD_CONV_DIM), F32),
        ],
        scratch_shapes=[
            pltpu.VMEM((SSD_HEADS, SSD_HEAD_DIM, SSD_STATE), F32),
            pltpu.VMEM((CARRY_ROWS + CHUNK, SSD_CONV_DIM), F32),
        ],
        compiler_params=_cparams("arbitrary", "arbitrary"),
        name="ssd",
    )(y, y, y, y, dtr, s0, conv_prev8,
      p["ssd_conv_w"], p["ssd_conv_b"], p["ssd_dt_bias"], p["ssd_a_log"], p["ssd_d"],
      p["ssd_norm_g"], p["ssd_expand"])


def _sconv_kernel(n_tiles, bg_ref, cg_ref, xs_ref, cp_ref, cw_ref, y_ref, cst_ref, xp_scr):
    T = bg_ref.shape[0]
    c = pl.program_id(1)

    @pl.when(c == 0)
    def _():
        xp_scr[0:CARRY_ROWS, :] = cp_ref[0]

    xp_scr[CARRY_ROWS:CARRY_ROWS + T, :] = cg_ref[...].astype(F32) * xs_ref[...].astype(F32)
    y = bg_ref[...].astype(F32) * _causal_conv(xp_scr, cw_ref, SC_CONV, T)
    y_ref[...] = y.astype(y_ref.dtype)

    @pl.when(c == n_tiles - 1)
    def _():
        cst_ref[0] = xp_scr[pl.ds(CARRY_ROWS + T - (SC_CONV - 1), SC_CONV - 1), :]

    xp_scr[0:CARRY_ROWS, :] = xp_scr[T:T + CARRY_ROWS, :]


def _sconv(group, y, conv_prev8, p):
    nb = group.n_seq
    tm = group.tm_rowwise
    nt = group.seq_rows // tm
    rows = y.shape[0]
    tok = lambda piece: pl.BlockSpec((tm, D_MODEL), lambda b, c: (b * nt + c, piece))
    return pl.pallas_call(
        functools.partial(_sconv_kernel, nt),
        grid=(nb, nt),
        in_specs=[
            tok(Y_BG), tok(Y_CG), tok(Y_XSC),
            pl.BlockSpec((1, CARRY_ROWS, D_MODEL), lambda b, c: (b, 0, 0)),
            _const_spec((SC_CONV, D_MODEL)),
        ],
        out_specs=[tok(0), pl.BlockSpec((1, SC_CONV - 1, D_MODEL), lambda b, c: (b, 0, 0))],
        out_shape=[
            jax.ShapeDtypeStruct((rows, D_MODEL), BF16),
            jax.ShapeDtypeStruct((nb, SC_CONV - 1, D_MODEL), F32),
        ],
        scratch_shapes=[pltpu.VMEM((CARRY_ROWS + tm, D_MODEL), F32)],
        compiler_params=_cparams("arbitrary", "arbitrary"),
        name="sconv",
    )(y, y, y, conv_prev8, p["sc_conv_w"])


def _sconv_s_kernel(bg_ref, cg_ref, xs_ref, h1_ref, h2_ref, cw_ref, y_ref, u_ref):
    u = cg_ref[...].astype(F32) * xs_ref[...].astype(F32)
    u_ref[...] = u
    pos = _seq_pos(*u.shape)
    conv = (cw_ref[2:3, :] * u + cw_ref[1:2, :] * _shift_rows(u, 1, h1_ref[...], pos)
            + cw_ref[0:1, :] * _shift_rows(u, 2, h2_ref[...], pos))
    y_ref[...] = (bg_ref[...].astype(F32) * conv).astype(y_ref.dtype)


def _sconv_s(group, y, conv_state, p):
    rows = y.shape[0]
    tm = group.tm_rowwise
    tok = lambda piece: pl.BlockSpec((tm, D_MODEL), lambda i: (i, piece))
    h1, h2 = _history_tiles(conv_state, SC_CONV)
    yb, u = pl.pallas_call(
        _sconv_s_kernel,
        grid=(rows // tm,),
        in_specs=[tok(Y_BG), tok(Y_CG), tok(Y_XSC), tok(0), tok(0), _const_spec((SC_CONV, D_MODEL))],
        out_specs=[tok(0), tok(0)],
        out_shape=[jax.ShapeDtypeStruct((rows, D_MODEL), BF16), jax.ShapeDtypeStruct((rows, D_MODEL), F32)],
        compiler_params=_cparams("arbitrary"),
        name="sconv_s",
    )(y, y, y, h1, h2, p["sc_conv_w"])
    l = group.last_chunk_valid
    return yb, u.reshape(group.n_seq, SAMPLE_ROWS, D_MODEL)[:, l - (SC_CONV - 1):l]


def _spread_kv(k_rows, v_rows):
    kf = k_rows.astype(F32)
    vf = v_rows.astype(F32)
    blk = lax.broadcasted_iota(jnp.int32, kf.shape, 1) // ATTN_HEAD_DIM
    kspread, vplaced = [], []
    for g in range(ATTN_KV_HEADS):
        tk = jnp.where(blk == g, kf, 0.0)
        uk = tk + pltpu.roll(tk, ATTN_HEAD_DIM, 1)
        kspread.append((uk + pltpu.roll(uk, 2 * ATTN_HEAD_DIM, 1)).astype(BF16))
        tv = jnp.where(blk == g, vf, 0.0)
        tv1 = pltpu.roll(tv, ATTN_HEAD_DIM, 1)
        shifted = (tv, tv1, pltpu.roll(tv, 2 * ATTN_HEAD_DIM, 1), pltpu.roll(tv1, 2 * ATTN_HEAD_DIM, 1))
        vplaced.append([shifted[(r - g) % ATTN_KV_HEADS].astype(BF16) for r in range(ATTN_GROUP)])
    return kspread, vplaced


def _attn_kernel(tq_rows, carry, n_new, *refs):
    if carry:
        q_ref, k_ref, v_ref, sink_ref, bias_ref, y_ref, kprev_scr, vprev_scr = refs
    else:
        q_ref, k_ref, v_ref, kp_ref, vp_ref, sink_ref, bias_ref, y_ref, kout_ref, vout_ref = refs
    T = tq_rows
    c = pl.program_id(1)
    scale = ATTN_HEAD_DIM ** -0.5
    blocks = []
    if carry:
        ks_cur, vp_cur = _spread_kv(k_ref[...], v_ref[...])
        read_slot = (c + 1) % 2
        write_slot = c % 2

        @pl.when(c == 0)
        def _():
            kprev_scr[read_slot] = jnp.zeros(kprev_scr.shape[1:], BF16)
            vprev_scr[read_slot] = jnp.zeros(vprev_scr.shape[1:], BF16)
        ks_prev = [kprev_scr[read_slot, g] for g in range(ATTN_KV_HEADS)]
        vp_prev = [[vprev_scr[read_slot, g * ATTN_GROUP + r] for r in range(ATTN_GROUP)]
                   for g in range(ATTN_KV_HEADS)]
        blocks.append((q_ref[...] * jnp.asarray(scale, q_ref.dtype), ks_prev, ks_cur, vp_prev, vp_cur))
    else:
        n_sub = q_ref.shape[0] // SAMPLE_ROWS
        q_all, k_all, v_all = (r[...].astype(F32) for r in (q_ref, k_ref, v_ref))
        keep = WINDOW - n_new
        pad_to = lambda x, rows: jnp.concatenate([x, jnp.zeros((rows - x.shape[0], x.shape[1]), F32)], axis=0)
        for sb in range(n_sub):
            rows = slice(sb * SAMPLE_ROWS, (sb + 1) * SAMPLE_ROWS)
            ks_cur, vp_cur = _spread_kv(pad_to(k_all[rows], WINDOW).astype(BF16),
                                        pad_to(v_all[rows], WINDOW).astype(BF16))
            ks_prev, vp_prev = _spread_kv(kp_ref[0, sb].astype(BF16), vp_ref[0, sb].astype(BF16))
            blocks.append(((pad_to(q_all[rows], T) * scale).astype(BF16), ks_prev, ks_cur, vp_prev, vp_cur))
            for src, new, dst in ((kp_ref, k_all, kout_ref), (vp_ref, v_all, vout_ref)):
                dst[0, sb, pl.ds(0, keep), :] = src[0, sb, pl.ds(n_new, keep), :]
                dst[0, sb, pl.ds(keep, n_new), :] = new[sb * SAMPLE_ROWS:sb * SAMPLE_ROWS + n_new, :]

    q_blk = lax.broadcasted_iota(jnp.int32, (T, ATTN_KV), 1) // ATTN_HEAD_DIM
    pairs = [(i, g) for i in range(len(blocks)) for g in range(ATTN_KV_HEADS)]
    scores = []
    for i, g in pairs:
        q, ks_prev, ks_cur = blocks[i][0], blocks[i][1], blocks[i][2]
        qg = q[:, g * ATTN_KV:(g + 1) * ATTN_KV]
        q4 = jnp.concatenate([jnp.where(q_blk == r, qg, jnp.zeros_like(qg))
                              for r in range(ATTN_GROUP)], axis=0)
        keys = jnp.concatenate([ks_prev[g], ks_cur[g]], axis=0)
        scores.append(_dot_nt(q4, keys))
    bias = [bias_ref[0, g * ATTN_GROUP:(g + 1) * ATTN_GROUP].reshape(ATTN_GROUP * T, 2 * WINDOW)
            for g in range(ATTN_KV_HEADS)]
    sinks = [jnp.concatenate([jnp.broadcast_to(sink_ref[:, h:h + 1], (T, 1))
                              for h in range(g * ATTN_GROUP, (g + 1) * ATTN_GROUP)], axis=0)
             for g in range(ATTN_KV_HEADS)]
    n = range(len(pairs))
    s = [scores[j] - bias[pairs[j][1]] for j in n]
    m = [jnp.maximum(jnp.max(s[j], axis=-1, keepdims=True), sinks[pairs[j][1]]) for j in n]
    e = [jnp.exp(s[j] - m[j]) for j in n]
    den = [jnp.sum(e[j], axis=-1, keepdims=True) + jnp.exp(sinks[pairs[j][1]] - m[j]) for j in n]
    pn = [(e[j] * (1.0 / den[j])).astype(BF16) for j in n]
    probs = [jnp.concatenate([pn[j][r * T:(r + 1) * T] for r in range(ATTN_GROUP)], axis=1)
             for j in n]
    outs = []
    for j, (i, g) in enumerate(pairs):
        vp_prev, vp_cur = blocks[i][3], blocks[i][4]
        vals = jnp.concatenate([blk for r in range(ATTN_GROUP)
                                for blk in (vp_prev[g][r], vp_cur[g][r])], axis=0)
        outs.append(_dot(probs[j], vals))
    per_block = [jnp.concatenate(outs[i * ATTN_KV_HEADS:(i + 1) * ATTN_KV_HEADS], axis=1)
                 for i in range(len(blocks))]
    if carry:
        y_ref[...] = per_block[0].astype(y_ref.dtype)
    else:
        y_ref[...] = jnp.concatenate([y[0:SAMPLE_ROWS] for y in per_block], axis=0).astype(y_ref.dtype)
    if carry:
        ks_cur, vp_cur = blocks[0][2], blocks[0][4]
        for g in range(ATTN_KV_HEADS):
            kprev_scr[write_slot, g] = ks_cur[g]
            for r in range(ATTN_GROUP):
                vprev_scr[write_slot, g * ATTN_GROUP + r] = vp_cur[g][r]


def _attn_bias_table(q_rows):
    t = jnp.arange(q_rows)[:, None]
    j = jnp.arange(2 * WINDOW)[None, :]
    dist = WINDOW + t - j
    band = (dist >= 0) & (dist <= WINDOW)
    slopes = 2.0 ** (-8.0 * jnp.arange(1, ATTN_HEADS + 1, dtype=F32) / ATTN_HEADS)
    bias = slopes[:, None, None] * dist.astype(F32)[None]
    first = jnp.where(band & (j >= WINDOW), bias, 1e30)
    later = jnp.where(band, bias, 1e30)
    return jnp.stack([first, later], axis=0)


def _attn(group, y, layer, k_cache, v_cache, p):
    nb, nc = group.n_seq, group.chunks_per_seq
    rows = y.shape[0]
    tq = group.attn_q_rows
    carry = group.is_prompt
    n_sub = 1 if carry else group.attn_seqs_per_step
    nb = nb // n_sub
    blk_rows = group.chunk_rows * n_sub
    blk = lambda width, piece: pl.BlockSpec((blk_rows, width), lambda b, c: (b * nc + c, piece))
    in_specs = [blk(D_MODEL, Y_Q), blk(ATTN_KV, Y_K), blk(ATTN_KV, Y_V)]
    args = [y, y, y]
    out_specs = [blk(D_MODEL, 0)]
    out_shape = [jax.ShapeDtypeStruct((rows, D_MODEL), BF16)]
    scratch, aliases = [], {}
    if carry:
        bias_spec = pl.BlockSpec((1, ATTN_HEADS, tq, 2 * WINDOW),
                                 lambda b, c: (jnp.minimum(c, 1), 0, 0, 0))
        scratch = [pltpu.VMEM((2, ATTN_KV_HEADS, WINDOW, ATTN_KV), BF16),
                   pltpu.VMEM((2, ATTN_HEADS, WINDOW, ATTN_KV), BF16)]
    else:
        bias_spec = pl.BlockSpec((1, ATTN_HEADS, tq, 2 * WINDOW), lambda b, c: (1, 0, 0, 0))
        cache_spec = pl.BlockSpec((1, n_sub, WINDOW, ATTN_KV), lambda b, c: (layer, b, 0, 0))
        aliases = {len(args): 1, len(args) + 1: 2}
        in_specs += [cache_spec, cache_spec]
        args += [k_cache, v_cache]
        out_specs += [cache_spec, cache_spec]
        out_shape += [jax.ShapeDtypeStruct(k_cache.shape, F32), jax.ShapeDtypeStruct(v_cache.shape, F32)]
    in_specs += [_const_spec((1, LANES)), bias_spec]
    args += [p["attn_sinks"], _attn_bias_table(tq)]
    res = pl.pallas_call(
        functools.partial(_attn_kernel, tq, carry, group.last_chunk_valid),
        grid=(nb, nc),
        in_specs=in_specs,
        out_specs=out_specs,
        out_shape=out_shape,
        scratch_shapes=scratch,
        input_output_aliases=aliases,
        compiler_params=_cparams("arbitrary", "arbitrary"),
        name="attn_p" if group.is_prompt else "attn_s",
    )(*args)
    return res[0] if carry else res


def _gelu_tanh(x):
    return 0.5 * x * (1.0 + jnp.tanh(math.sqrt(2.0 / math.pi) * (x + 0.044715 * (x * x * x))))


def _gmlp_kernel(u_ref, v_ref, lg_ref, lb_ref, ws_ref, bs_ref, y_ref):
    T = CHUNK
    n_chunks = u_ref.shape[0] // T
    u = _gelu_tanh(u_ref[...].astype(F32))
    v = _gelu_tanh(v_ref[...].astype(F32))
    vc = v - jnp.mean(v, axis=-1, keepdims=True)
    v = vc * lax.rsqrt(jnp.mean(vc * vc, axis=-1, keepdims=True) + EPS) * lg_ref[...] + lb_ref[...]
    row = lax.broadcasted_iota(jnp.int32, (T, T), 0)
    col = lax.broadcasted_iota(jnp.int32, (T, T), 1)
    causal = row >= col
    vb = v.astype(BF16)
    mixed = [[None] * GM_GROUPS for _ in range(n_chunks)]
    for g in range(GM_GROUPS):
        w = jnp.where(causal, ws_ref[g], 0.0).astype(BF16)
        gs = slice(g * GM_GROUP_WIDTH, (g + 1) * GM_GROUP_WIDTH)
        v_g = jnp.concatenate([vb[c * T:(c + 1) * T, gs] for c in range(n_chunks)], axis=1)
        m_g = _dot(w, v_g)
        for c in range(n_chunks):
            mixed[c][g] = m_g[:, c * GM_GROUP_WIDTH:(c + 1) * GM_GROUP_WIDTH] + bs_ref[:, gs]
    mixed = jnp.concatenate([jnp.concatenate(m, axis=1) for m in mixed], axis=0)
    y_ref[...] = (u * mixed).astype(y_ref.dtype)


def _gmlp(group, y, p):
    tm = group.tm_rowwise
    rows = y.shape[0]
    tok = lambda piece: pl.BlockSpec((tm, D_MODEL), lambda i: (i, piece))
    yd = pl.pallas_call(
        _gmlp_kernel,
        grid=(rows // tm,),
        in_specs=[
            tok(Y_U), tok(Y_VGM),
            _const_spec((1, D_MODEL)),
            _const_spec((1, D_MODEL)),
            _const_spec((GM_GROUPS, CHUNK, CHUNK)),
            _const_spec((CHUNK, D_MODEL)),
        ],
        out_specs=tok(0),
        out_shape=jax.ShapeDtypeStruct((rows, D_MODEL), BF16),
        compiler_params=_cparams("arbitrary"),
        name="gmlp",
    )(y, y, p["gm_ln_g"], p["gm_ln_b"], p["gm_w_s"], p["gm_b_exp"])
    return yd, None


def _gmlp_s_kernel(n_diag, u_ref, v_ref, lg_ref, lb_ref, wd_ref, bs_ref, y_ref, v_out):
    u = _gelu_tanh(u_ref[...].astype(F32))
    v = _gelu_tanh(v_ref[...].astype(F32))
    vc = v - jnp.mean(v, axis=-1, keepdims=True)
    v = vc * lax.rsqrt(jnp.mean(vc * vc, axis=-1, keepdims=True) + EPS) * lg_ref[...] + lb_ref[...]
    mixed = bs_ref[...] + wd_ref[0] * v
    for d in range(1, n_diag):
        mixed = mixed + wd_ref[d] * pltpu.roll(v, d, 0)
    y_ref[...] = (u * mixed).astype(y_ref.dtype)
    v_out[...] = v


def _gmlp_s_tables(group, w_s, b_s):
    l = group.last_chunk_valid
    reps = group.tm_rowwise // SAMPLE_ROWS
    t = jnp.arange(SAMPLE_ROWS)
    expand = lambda per_group: jnp.tile(jnp.repeat(per_group.T, GM_GROUP_WIDTH, axis=1), (reps, 1))
    diags = []
    for d in range(l):
        src = t - d
        w = w_s[:, t, jnp.clip(src, 0, None)]
        diags.append(expand(jnp.where((src >= 0) & (t < l), w, 0.0)))
    return jnp.stack(diags, axis=0), expand(b_s[:, :SAMPLE_ROWS])


def _gmlp_s(group, y, p):
    rows = y.shape[0]
    tm = group.tm_rowwise
    l = group.last_chunk_valid
    tok = lambda piece: pl.BlockSpec((tm, D_MODEL), lambda i: (i, piece))
    wd, bs = _gmlp_s_tables(group, p["gm_w_s"], p["gm_b_s"])
    return pl.pallas_call(
        functools.partial(_gmlp_s_kernel, l),
        grid=(rows // tm,),
        in_specs=[tok(Y_U), tok(Y_VGM), _const_spec((1, D_MODEL)), _const_spec((1, D_MODEL)),
                  _const_spec((l, tm, D_MODEL)), _const_spec((tm, D_MODEL))],
        out_specs=[tok(0), tok(0)],
        out_shape=[jax.ShapeDtypeStruct((rows, D_MODEL), BF16), jax.ShapeDtypeStruct((rows, D_MODEL), F32)],
        compiler_params=_cparams("arbitrary"),
        name="gmlp_s",
    )(y, y, p["gm_ln_g"], p["gm_ln_b"], wd, bs)


def _merge_kernel(x_ref, ga_ref, ya_ref, yb_ref, yc_ref, yd_ref, g0_ref, g1_ref, g2_ref, g3_ref,
                  wb_ref, wo_ref, o_ref):
    merged = None
    for i, (y_ref, g_ref) in enumerate(((ya_ref, g0_ref), (yb_ref, g1_ref), (yc_ref, g2_ref),
                                        (yd_ref, g3_ref))):
        gate = jax.nn.sigmoid(g_ref[...].astype(F32))
        term = gate * _dot(y_ref[...], wb_ref[i])
        merged = term if merged is None else merged + term
    o = _dot(merged.astype(BF16), wo_ref[...])
    o_ref[...] = x_ref[...] + _mod_rows(ga_ref) * o


def _merge(group, x, mod, branches, y, p):
    tm = group.tm_merge
    rows = x.shape[0]
    tok = lambda piece: pl.BlockSpec((tm, D_MODEL), lambda i: (i, piece))
    return pl.pallas_call(
        _merge_kernel,
        grid=(rows // tm,),
        in_specs=[tok(0), _mod_spec(group, tm, 2)] + [tok(0)] * 4 + [tok(Y_G0 + i) for i in range(4)] + [
            pl.BlockSpec((N_BRANCH, D_MODEL, D_MODEL), lambda i: (0, 0, 0), pipeline_mode=pl.Buffered(1)),
            pl.BlockSpec((D_MODEL, D_MODEL), lambda i: (0, 0), pipeline_mode=pl.Buffered(1)),
        ],
        out_specs=tok(0),
        out_shape=jax.ShapeDtypeStruct((rows, D_MODEL), F32),
        compiler_params=_cparams("arbitrary"),
        name="merge",
    )(x, mod, *branches, y, y, y, y, p["w_branch"], p["w_o"])


def _ffn_kernel(seg8, l_valid, tiles_per_seq, final_norm,
                x_ref, g_ref, sc_ref, sh_ref, ga_ref, cp_ref, wa_ref, wg_ref, cw_ref, cb_ref, wd_ref,
                gf_ref, o_ref, cst_ref, carry_scr, xp_scr, act_scr):
    tm = x_ref.shape[0]
    i = pl.program_id(0)
    x = x_ref[...]
    h = _modnorm(x, g_ref[...], _mod_rows(sc_ref), _mod_rows(sh_ref)).astype(BF16)
    first = CARRY_ROWS - (FFN_CONV - 1)

    if not seg8:
        @pl.when(i % tiles_per_seq == 0)
        def _():
            carry_scr[...] = cp_ref[0]

    def conv_piece(up, half, cs, xp):
        off = half * D_FF
        col = slice(off + cs.start, off + cs.stop)
        if seg8:
            tb = tm // SAMPLE_ROWS
            xp[:, 0:CARRY_ROWS, :] = cp_ref[:, :, col]
            xp[:, CARRY_ROWS:CARRY_ROWS + SAMPLE_ROWS, :] = up.reshape(tb, SAMPLE_ROWS, MXU_WIDTH)
            acc = None
            for j in range(FFN_CONV):
                term = cw_ref[j:j + 1, col] * xp[:, pl.ds(first + j, SAMPLE_ROWS), :]
                acc = term if acc is None else acc + term
            cst_ref[:, :, col] = xp[:, pl.ds(CARRY_ROWS + l_valid - (FFN_CONV - 1), FFN_CONV - 1), :]
            return acc.reshape(tm, MXU_WIDTH) + cb_ref[:, col]
        xp[0:CARRY_ROWS, :] = carry_scr[:, col]
        xp[CARRY_ROWS:CARRY_ROWS + tm, :] = up
        acc = None
        for j in range(FFN_CONV):
            term = cw_ref[j:j + 1, col] * xp[pl.ds(first + j, tm), :]
            acc = term if acc is None else acc + term
        carry_scr[:, col] = xp[tm:tm + CARRY_ROWS, :]
        return acc + cb_ref[:, col]

    n_pieces = D_FF // MXU_WIDTH
    pieces = [slice(k * MXU_WIDTH, (k + 1) * MXU_WIDTH) for k in range(n_pieces)]
    ups = (_dot(h, wa_ref[:, pieces[0]]), _dot(h, wg_ref[:, pieces[0]]))
    for k in range(n_pieces):
        up_a, up_g = ups
        if k + 1 < n_pieces:
            ups = (_dot(h, wa_ref[:, pieces[k + 1]]), _dot(h, wg_ref[:, pieces[k + 1]]))
        a = conv_piece(up_a, 0, pieces[k], xp_scr.at[2 * (k % 2)])
        g = conv_piece(up_g, 1, pieces[k], xp_scr.at[2 * (k % 2) + 1])
        act_scr[:, pieces[k]] = (_silu(a) * g).astype(BF16)

    if not seg8:
        @pl.when(i % tiles_per_seq == tiles_per_seq - 1)
        def _():
            cst_ref[0] = carry_scr[CARRY_ROWS - (FFN_CONV - 1):CARRY_ROWS, :]

    out = x + _mod_rows(ga_ref) * _dot(act_scr[...], wd_ref[...])
    if final_norm:
        out = out * lax.rsqrt(jnp.mean(out * out, axis=-1, keepdims=True) + EPS) * gf_ref[...]
    o_ref[...] = out


def _ffn(group, x, mod, conv_prev8, p, g_final, final_norm):
    tm = group.tm_ffn
    rows = x.shape[0]
    seg8 = not group.is_prompt
    tiles_per_seq = 1 if seg8 else group.seq_rows // tm
    single = dict(pipeline_mode=pl.Buffered(1))
    if seg8:
        tb = tm // SAMPLE_ROWS
        cp_spec = pl.BlockSpec((tb, CARRY_ROWS, 2 * D_FF), lambda i: (i, 0, 0))
        cst_spec = pl.BlockSpec((tb, FFN_CONV - 1, 2 * D_FF), lambda i: (i, 0, 0))
        xp_shape = (tb, CARRY_ROWS + SAMPLE_ROWS, MXU_WIDTH)
    else:
        cp_spec = pl.BlockSpec((1, CARRY_ROWS, 2 * D_FF), lambda i: (i // tiles_per_seq, 0, 0))
        cst_spec = pl.BlockSpec((1, FFN_CONV - 1, 2 * D_FF), lambda i: (i // tiles_per_seq, 0, 0))
        xp_shape = (CARRY_ROWS + tm, MXU_WIDTH)
    kern = functools.partial(_ffn_kernel, seg8, group.last_chunk_valid, tiles_per_seq, final_norm)
    return pl.pallas_call(
        kern,
        grid=(rows // tm,),
        in_specs=[
            pl.BlockSpec((tm, D_MODEL), lambda i: (i, 0)),
            _const_spec((1, D_MODEL)),
            _mod_spec(group, tm, 4),
            _mod_spec(group, tm, 3),
            _mod_spec(group, tm, 5),
            cp_spec,
            pl.BlockSpec((D_MODEL, D_FF), lambda i: (0, 0), **single),
            pl.BlockSpec((D_MODEL, D_FF), lambda i: (0, 0), **single),
            _const_spec((FFN_CONV, 2 * D_FF)),
            _const_spec((1, 2 * D_FF)),
            pl.BlockSpec((D_FF, D_MODEL), lambda i: (0, 0), **single),
            _const_spec((1, D_MODEL)),
        ],
        out_specs=[pl.BlockSpec((tm, D_MODEL), lambda i: (i, 0)), cst_spec],
        out_shape=[
            jax.ShapeDtypeStruct((rows, D_MODEL), F32),
            jax.ShapeDtypeStruct((group.n_seq, FFN_CONV - 1, 2 * D_FF), F32),
        ],
        scratch_shapes=[
            pltpu.VMEM((CARRY_ROWS, 2 * D_FF), F32),
            pltpu.VMEM((4,) + xp_shape, F32),
            pltpu.VMEM((tm, D_FF), BF16),
        ],
        compiler_params=_cparams("arbitrary"),
        name="ffn",
    )(x, p["g_norm_ffn"], mod, mod, mod, conv_prev8, p["w_up_a"], p["w_up_g"], p["ffn_conv_w"],
      p["ffn_conv_b"], p["w_down"], g_final)


class _Group:
    def __init__(self, is_prompt, n_seq, seq_len):
        self.is_prompt = is_prompt
        self.n_seq = n_seq
        if is_prompt:
            self.seq_rows = seq_len
            self.chunk_rows = CHUNK
            self.chunks_per_seq = seq_len // CHUNK
            self.last_chunk_valid = CHUNK
            self.per_token_mod = False
            self.tm_inproj = 1024
            self.tm_merge = 512
            self.tm_ffn = 512
            self.attn_q_rows = CHUNK
            self.tm_rowwise = 512
        else:
            self.attn_q_rows = BF16_SUBLANES
            self.tm_rowwise = 256
            self.attn_seqs_per_step = 4
            self.seq_rows = SAMPLE_ROWS
            self.chunk_rows = SAMPLE_ROWS
            self.chunks_per_seq = 1
            self.last_chunk_valid = seq_len
            self.per_token_mod = True
            self.tm_inproj = 512
            self.tm_merge = 512
            self.tm_ffn = 128


def _pad_front(state, rows):
    return jnp.pad(state, ((0, 0), (rows - state.shape[1], 0), (0, 0)))


def _layer_params(i, w):
    pad_heads = lambda v: jnp.pad(v, (0, LANES - SSD_HEADS)).reshape(1, LANES)
    w_in = w["w_in"][i]
    head_of_channel = jnp.arange(SSD_D_INNER) // SSD_HEAD_DIM
    expand = (jnp.arange(LANES)[:, None] == head_of_channel[None, :]).astype(BF16)
    return {
        "g_norm_mix": w["g_norm_mix"][i].reshape(1, D_MODEL),
        "w_main": jnp.concatenate([w_in[:, a:b] for a, b in _W_IN_PIECES], axis=1).astype(BF16),
        "w_dt": jnp.pad(w_in[:, _IN_OFF_DT:_IN_OFF_DT + SSD_HEADS],
                        ((0, 0), (0, LANES - SSD_HEADS))).astype(BF16),
        "ssd_conv_w": w["ssd_conv_w"][i],
        "ssd_conv_b": w["ssd_conv_b"][i].reshape(1, SSD_CONV_DIM),
        "ssd_dt_bias": pad_heads(w["ssd_dt_bias"][i]),
        "ssd_a_log": pad_heads(w["ssd_a_log"][i]),
        "ssd_d": jnp.repeat(w["ssd_d"][i], SSD_HEAD_DIM).reshape(1, SSD_D_INNER),
        "ssd_norm_g": w["ssd_norm_g"][i].reshape(1, SSD_D_INNER),
        "ssd_expand": expand,
        "sc_conv_w": w["sc_conv_w"][i],
        "attn_sinks": jnp.pad(w["attn_sinks"][i], (0, LANES - ATTN_HEADS)).reshape(1, LANES),
        "gm_ln_g": w["gm_ln_g"][i].reshape(1, D_MODEL),
        "gm_ln_b": w["gm_ln_b"][i].reshape(1, D_MODEL),
        "gm_w_s": w["gm_w_s"][i],
        "gm_b_s": w["gm_b_s"][i],
        "gm_b_exp": jnp.repeat(w["gm_b_s"][i].T, GM_GROUP_WIDTH, axis=1),
        "w_branch": w["w_branch"][i].astype(BF16),
        "w_o": w["w_o"][i].astype(BF16),
        "g_norm_ffn": w["g_norm_ffn"][i].reshape(1, D_MODEL),
        "w_up_a": w["ffn_w_up"][i][:, :D_FF].astype(BF16),
        "w_up_g": w["ffn_w_up"][i][:, D_FF:].astype(BF16),
        "ffn_conv_w": w["ffn_conv_w"][i],
        "ffn_conv_b": w["ffn_conv_b"][i].reshape(1, 2 * D_FF),
        "w_down": w["ffn_w_down"][i].astype(BF16),
    }


def _run_group(group, x, mod_all, states, params, g_final):
    n = group.n_seq
    outs = []
    if not group.is_prompt:
        ssm_all = jnp.zeros(states[0].shape, F32)
        k_all = states[3].reshape(DEPTH, n, WINDOW, ATTN_KV)
        v_all = states[4].reshape(DEPTH, n, WINDOW, ATTN_KV)
    for i in range(DEPTH):
        p = params[i]
        mod = mod_all[i]
        y, dtr = _inproj(group, x, mod, p["g_norm_mix"], p["w_main"], p["w_dt"])
        if group.is_prompt:
            ssm0 = jnp.zeros((n, SSD_HEADS, SSD_HEAD_DIM, SSD_STATE), F32)
            ssdc0 = jnp.zeros((n, CARRY_ROWS, SSD_CONV_DIM), F32)
            scc0 = jnp.zeros((n, CARRY_ROWS, D_MODEL), F32)
            ffc0 = jnp.zeros((n, FFN_CONV - 1, 2 * D_FF), F32)
            ya, ssm1, ssdc1 = _ssd(group, y, dtr, ssm0, ssdc0, p)
            yb, scc1 = _sconv(group, y, scc0, p)
            yd, v_rows = _gmlp(group, y, p)
            yc = _attn(group, y, i, None, None, p)
        else:
            ffc0 = states[5][i]
            ya, ssm_all, ssdc1 = _ssd_s(group, y, dtr, i, states[0], ssm_all, states[1][i], p)
            yb, scc1 = _sconv_s(group, y, states[2][i], p)
            yd, v_rows = _gmlp_s(group, y, p)
            yc, k_all, v_all = _attn(group, y, i, k_all, v_all, p)
        x = _merge(group, x, mod, (ya, yb, yc, yd), y, p)
        x, ffc1 = _ffn(group, x, mod, _pad_front(ffc0, CARRY_ROWS), p, g_final, i == DEPTH - 1)

        kcol = slice(Y_K * SSD_BC, (Y_K + 1) * SSD_BC)
        vcol = slice(Y_V * SSD_BC, (Y_V + 1) * SSD_BC)
        if group.is_prompt:
            y3 = y.reshape(n, group.seq_rows, Y_COLS)
            k1 = y3[:, group.seq_rows - WINDOW:, kcol].astype(F32)
            v1 = y3[:, group.seq_rows - WINDOW:, vcol].astype(F32)
            k1 = k1.reshape(n, WINDOW, ATTN_KV_HEADS, ATTN_HEAD_DIM)
            v1 = v1.reshape(n, WINDOW, ATTN_KV_HEADS, ATTN_HEAD_DIM)
            outs.append((ssm1, ssdc1, scc1, k1, v1, ffc1))
        else:
            gv = v_rows.reshape(n, SAMPLE_ROWS, D_MODEL)[:, :group.last_chunk_valid]
            outs.append((ssdc1, scc1, ffc1, gv))
    stacked = tuple(jnp.stack([o[j] for o in outs], axis=0) for j in range(len(outs[0])))
    if not group.is_prompt:
        ssdc, scc, ffc, gv = stacked
        cache_shape = (DEPTH, n, WINDOW, ATTN_KV_HEADS, ATTN_HEAD_DIM)
        stacked = (ssm_all, ssdc, scc, k_all.reshape(cache_shape), v_all.reshape(cache_shape), ffc, gv)
    return x, stacked


def kernel(x_prompt, x_sample, c_prompt, c_sample, state_ssm, state_ssd_conv, state_sc_conv, cache_k,
           cache_v, state_ffn_conv, w_ada, b_ada, g_norm_mix, w_in, ssd_conv_w, ssd_conv_b, ssd_dt_bias,
           ssd_a_log, ssd_d, ssd_norm_g, sc_conv_w, attn_sinks, gm_ln_g, gm_ln_b, gm_w_s, gm_b_s,
           w_branch, w_o, g_norm_ffn, ffn_w_up, ffn_conv_w, ffn_conv_b, ffn_w_down, g_final):
    weights = dict(g_norm_mix=g_norm_mix, w_in=w_in, ssd_conv_w=ssd_conv_w, ssd_conv_b=ssd_conv_b,
                   ssd_dt_bias=ssd_dt_bias, ssd_a_log=ssd_a_log, ssd_d=ssd_d, ssd_norm_g=ssd_norm_g,
                   sc_conv_w=sc_conv_w, attn_sinks=attn_sinks, gm_ln_g=gm_ln_g, gm_ln_b=gm_ln_b,
                   gm_w_s=gm_w_s, gm_b_s=gm_b_s, w_branch=w_branch, w_o=w_o, g_norm_ffn=g_norm_ffn,
                   ffn_w_up=ffn_w_up, ffn_conv_w=ffn_conv_w, ffn_conv_b=ffn_conv_b, ffn_w_down=ffn_w_down)
    params = [_layer_params(i, weights) for i in range(DEPTH)]
    gf = g_final.reshape(1, D_MODEL)

    nb_p, len_p, _ = x_prompt.shape
    nb_s, len_s, _ = x_sample.shape
    assert len_p % CHUNK == 0 and len_s <= SAMPLE_ROWS and len_s >= SSD_CONV - 1
    prompt = _Group(True, nb_p, len_p)
    sample = _Group(False, nb_s, len_s)

    c_rows = nb_p + nb_s
    c_pad = -c_rows % SAMPLE_ROWS
    c_all = jnp.pad(jnp.concatenate([c_prompt, c_sample], axis=0), ((0, c_pad), (0, 0)))
    mod = _ada(c_all, w_ada, b_ada)
    mod_p = mod[:, :nb_p].reshape(DEPTH, nb_p, 1, 6 * D_MODEL)
    mod_s = jnp.repeat(mod[:, nb_p:nb_p + nb_s], SAMPLE_ROWS, axis=1)

    xp = x_prompt.reshape(nb_p * len_p, D_MODEL)
    xs = jnp.pad(x_sample, ((0, 0), (0, SAMPLE_ROWS - len_s), (0, 0))).reshape(nb_s * SAMPLE_ROWS, D_MODEL)

    y_p, st_p = _run_group(prompt, xp, mod_p, (None,) * 6, params, gf)
    y_s, st_s = _run_group(sample, xs, mod_s,
                           (state_ssm, state_ssd_conv, state_sc_conv, cache_k, cache_v, state_ffn_conv),
                           params, gf)
    y_prompt = y_p.reshape(nb_p, len_p, D_MODEL)
    y_sample = y_s.reshape(nb_s, SAMPLE_ROWS, D_MODEL)[:, :len_s]
    return (y_prompt, y_sample) + st_p + st_s
```

```python
import functools
import math

import jax
import jax.numpy as jnp
from jax import lax
from jax.experimental import pallas as pl
from jax.experimental.pallas import tpu as pltpu

F32 = jnp.float32
BF16 = jnp.bfloat16

D_MODEL = 1024
DEPTH = 4
N_BRANCH = 4
SSD_HEADS = 16
SSD_HEAD_DIM = 64
SSD_GROUPS = 4
SSD_STATE = 64
SSD_CONV = 4
SSD_D_INNER = 1024
SSD_BC = SSD_GROUPS * SSD_STATE
SSD_CONV_DIM = SSD_D_INNER + 2 * SSD_BC
SC_CONV = 3
ATTN_HEADS = 16
ATTN_KV_HEADS = 4
ATTN_GROUP = 4
ATTN_HEAD_DIM = 64
ATTN_KV = ATTN_KV_HEADS * ATTN_HEAD_DIM
WINDOW = 128
GM_GROUPS = 8
GM_GROUP_WIDTH = 128
D_FF = 2816
FFN_CONV = 3
EPS = 1e-6

CHUNK = 128
SAMPLE_ROWS = 8
CARRY_ROWS = 8
LANES = 128
BF16_SUBLANES = 16
MXU_WIDTH = 256
VMEM_LIMIT = 56 * 1024 * 1024

(Y_Z, Y_XS, Y_BG, Y_CG, Y_XSC, Y_Q, Y_U, Y_VGM, Y_G0) = range(9)
Y_WIDE = 12
Y_SSDB, Y_SSDC, Y_K, Y_V = (Y_WIDE * 4 + i for i in range(4))
Y_COLS = Y_WIDE * D_MODEL + 4 * SSD_BC
INPROJ_TN = Y_COLS // 4

_IN_OFF_XBC = 1024
_IN_OFF_DT = 2560
_IN_OFF_BCX = 2576
_IN_OFF_Q = 5648
_IN_OFF_K = 6672
_IN_OFF_V = 6928
_IN_OFF_UV = 7184
_IN_OFF_GATES = 9232
_W_IN_PIECES = (
    (0, 1024),
    (_IN_OFF_XBC, _IN_OFF_XBC + 1024),
    (_IN_OFF_BCX, _IN_OFF_BCX + 1024),
    (_IN_OFF_BCX + 1024, _IN_OFF_BCX + 2048),
    (_IN_OFF_BCX + 2048, _IN_OFF_BCX + 3072),
    (_IN_OFF_Q, _IN_OFF_Q + 1024),
    (_IN_OFF_UV, _IN_OFF_UV + 1024),
    (_IN_OFF_UV + 1024, _IN_OFF_UV + 2048),
    (_IN_OFF_GATES, _IN_OFF_GATES + 4096),
    (_IN_OFF_XBC + 1024, _IN_OFF_XBC + 1280),
    (_IN_OFF_XBC + 1280, _IN_OFF_XBC + 1536),
    (_IN_OFF_K, _IN_OFF_K + 256),
    (_IN_OFF_V, _IN_OFF_V + 256),
)


def _cparams(*sem):
    return pltpu.CompilerParams(dimension_semantics=sem, vmem_limit_bytes=VMEM_LIMIT)


def _const_spec(shape):
    zeros = (0,) * len(shape)
    return pl.BlockSpec(shape, lambda *_: zeros)


def _silu(x):
    return x * jax.nn.sigmoid(x)


def _mod_rows(ref):
    return ref[0] if len(ref.shape) == 3 else ref[...]


def _modnorm(x, g, sc, sh):
    r = lax.rsqrt(jnp.mean(x * x, axis=-1, keepdims=True) + EPS)
    return (x * r * g) * (1.0 + sc) + sh


def _split3(x):
    hi = x.astype(BF16)
    r1 = x - hi.astype(F32)
    mid = r1.astype(BF16)
    lo = (r1 - mid.astype(F32)).astype(BF16)
    return hi, mid, lo


def _dot(a, b):
    return jnp.dot(a, b, preferred_element_type=F32)


def _dot_nt(a, b):
    return lax.dot_general(a, b, (((1,), (1,)), ((), ())), preferred_element_type=F32)


def _dot_tn(a, b):
    return lax.dot_general(a, b, (((0,), (0,)), ((), ())), preferred_element_type=F32)


def _dot_exact_lhs(a_bf16, x):
    hi, mid, lo = _split3(x)
    return _dot(a_bf16, hi) + _dot(a_bf16, mid) + _dot(a_bf16, lo)


def _dot_exact_rhs(x, e_bf16):
    hi, mid, lo = _split3(x)
    return _dot(hi, e_bf16) + _dot(mid, e_bf16) + _dot(lo, e_bf16)


def _chunk_rows(ref, rows):
    x = ref[...].astype(F32)
    if x.shape[0] == rows:
        return x
    pad = jnp.zeros((rows - x.shape[0], x.shape[1]), F32)
    return jnp.concatenate([x, pad], axis=0)


def _ada_kernel(c_ref, w_ref, b_ref, o_ref):
    s = _silu(c_ref[...]).astype(BF16)
    o_ref[0] = _dot(s, w_ref[0].astype(BF16)) + b_ref[0]


def _ada(c_all, w_ada, b_ada):
    rows = c_all.shape[0]
    n_tiles = w_ada.shape[-1] // D_MODEL
    return pl.pallas_call(
        _ada_kernel,
        grid=(DEPTH, n_tiles),
        in_specs=[
            _const_spec((rows, D_MODEL)),
            pl.BlockSpec((1, D_MODEL, D_MODEL), lambda l, n: (l, 0, n)),
            pl.BlockSpec((1, 1, D_MODEL), lambda l, n: (l, 0, n)),
        ],
        out_specs=pl.BlockSpec((1, rows, D_MODEL), lambda l, n: (l, 0, n)),
        out_shape=jax.ShapeDtypeStruct((DEPTH, rows, 6 * D_MODEL), F32),
        compiler_params=_cparams("arbitrary", "arbitrary"),
        name="ada",
    )(c_all, w_ada, b_ada.reshape(DEPTH, 1, 6 * D_MODEL))


def _inproj_kernel(x_ref, g_ref, sc_ref, sh_ref, w_ref, wdt_ref, y_ref, dtr_ref, h_scr):
    @pl.when(pl.program_id(1) == 0)
    def _():
        h = _modnorm(x_ref[...], g_ref[...], _mod_rows(sc_ref), _mod_rows(sh_ref)).astype(BF16)
        h_scr[...] = h
        dtr_ref[...] = _dot(h, wdt_ref[...])

    y_ref[...] = _dot(h_scr[...], w_ref[...]).astype(y_ref.dtype)


def _mod_spec(group, tm, piece):
    if group.per_token_mod:
        return pl.BlockSpec((tm, D_MODEL), lambda i, *_: (i, piece))
    tiles_per_seq = group.seq_rows // tm
    return pl.BlockSpec((1, 1, D_MODEL), lambda i, *_: (i // tiles_per_seq, 0, piece))


def _inproj(group, x, mod, g, w_main, w_dt):
    tm = group.tm_inproj
    rows = x.shape[0]
    tn = INPROJ_TN
    return pl.pallas_call(
        _inproj_kernel,
        grid=(rows // tm, Y_COLS // tn),
        in_specs=[
            pl.BlockSpec((tm, D_MODEL), lambda i, n: (i, 0)),
            _const_spec((1, D_MODEL)),
            _mod_spec(group, tm, 1),
            _mod_spec(group, tm, 0),
            pl.BlockSpec((D_MODEL, tn), lambda i, n: (0, n)),
            _const_spec((D_MODEL, LANES)),
        ],
        out_specs=[
            pl.BlockSpec((tm, tn), lambda i, n: (i, n)),
            pl.BlockSpec((tm, LANES), lambda i, n: (i, 0)),
        ],
        out_shape=[
            jax.ShapeDtypeStruct((rows, Y_COLS), BF16),
            jax.ShapeDtypeStruct((rows, LANES), F32),
        ],
        scratch_shapes=[pltpu.VMEM((tm, D_MODEL), BF16)],
        compiler_params=_cparams("arbitrary", "arbitrary"),
        name="inproj",
    )(x, g, mod, mod, w_main, w_dt)


def _causal_conv(xp_ref, w_ref, width, rows):
    first = CARRY_ROWS - (width - 1)
    acc = w_ref[0:1, :] * xp_ref[pl.ds(first, rows), :]
    for j in range(1, width):
        acc = acc + w_ref[j:j + 1, :] * xp_ref[pl.ds(first + j, rows), :]
    return acc


def _chunk_spec(group, width, piece):
    nc = group.chunks_per_seq
    return pl.BlockSpec((group.chunk_rows, width), lambda b, c: (b * nc + c, piece))


HEADS_PER_GROUP = SSD_HEADS // SSD_GROUPS
GROUP_WIDTH = HEADS_PER_GROUP * SSD_HEAD_DIM


def _softplus(x):
    return jnp.maximum(x, 0.0) + jnp.log1p(jnp.exp(-jnp.abs(x)))


def _ssd_intra_group(g, cs, cs_t, cmat, bmat, x_b, causal):
    T = cs.shape[0]
    gs = slice(g * SSD_STATE, (g + 1) * SSD_STATE)
    cb = _dot_nt(cmat[:, gs], bmat[:, gs])
    low_half = lax.broadcasted_iota(jnp.int32, (T, 2 * SSD_HEAD_DIM), 1) < SSD_HEAD_DIM
    pair_out = []
    for pr in range(HEADS_PER_GROUP // 2):
        h0 = g * HEADS_PER_GROUP + 2 * pr
        xp = x_b[:, h0 * SSD_HEAD_DIM:(h0 + 2) * SSD_HEAD_DIM]
        acc = None
        for k in range(2):
            h = h0 + k
            diff = cs[:, h:h + 1] - cs_t[h:h + 1, :]
            lmat = jnp.where(causal, jnp.exp(jnp.where(causal, diff, 0.0)), 0.0)
            m = (cb * lmat).astype(BF16)
            keep = low_half if k == 0 else jnp.logical_not(low_half)
            part = _dot(m, jnp.where(keep, xp, jnp.zeros_like(xp)))
            acc = part if acc is None else acc + part
        pair_out.append(acc)
    return jnp.concatenate(pair_out, axis=1)


def _ssd_finish(y, xs, z, dsk, ng):
    y = (y + xs * dsk) * _silu(z)
    gw = SSD_D_INNER // SSD_GROUPS
    normed = []
    for g in range(SSD_GROUPS):
        yg = y[:, g * gw:(g + 1) * gw]
        normed.append(yg * lax.rsqrt(jnp.mean(yg * yg, axis=-1, keepdims=True) + EPS))
    return jnp.concatenate(normed, axis=1) * ng


def _seq_pos(rows, width):
    return lax.broadcasted_iota(jnp.int32, (rows, width), 0) % SAMPLE_ROWS


def _shift_rows(x, d, head, pos):
    return jnp.where(pos >= d, pltpu.roll(x, d, 0), head)


def _ssd_s_kernel(l_valid, z_ref, xs_ref, bm_ref, cm_ref, dtr_ref, s0_ref, h1_ref, h2_ref, h3_ref,
                  cw_ref, cb_ref, dtb_ref, alog_ref, dsk_ref, ng_ref, e_ref,
                  y_ref, sfin_ref, raw_ref):
    T = CHUNK
    n_seq = T // SAMPLE_ROWS
    raw = jnp.concatenate([xs_ref[...].astype(F32), bm_ref[...].astype(F32),
                           cm_ref[...].astype(F32)], axis=1)
    raw_ref[...] = raw
    pos = _seq_pos(T, SSD_CONV_DIM)
    conv = (cb_ref[...] + cw_ref[3:4, :] * raw
            + cw_ref[2:3, :] * _shift_rows(raw, 1, h1_ref[...], pos)
            + cw_ref[1:2, :] * _shift_rows(raw, 2, h2_ref[...], pos)
            + cw_ref[0:1, :] * _shift_rows(raw, 3, h3_ref[...], pos))
    xbc = _silu(conv)
    xs = xbc[:, 0:SSD_D_INNER]
    bmat = xbc[:, SSD_D_INNER:SSD_D_INNER + SSD_BC].astype(BF16)
    cmat = xbc[:, SSD_D_INNER + SSD_BC:SSD_CONV_DIM].astype(BF16)

    dt = _softplus(dtr_ref[...] + dtb_ref[...])
    dt = jnp.where(_seq_pos(T, LANES) < l_valid, dt, 0.0)
    dta = dt * (-jnp.exp(alog_ref[...]))
    row = lax.broadcasted_iota(jnp.int32, (T, T), 0)
    col = lax.broadcasted_iota(jnp.int32, (T, T), 1)
    causal = (row >= col) & (row // SAMPLE_ROWS == col // SAMPLE_ROWS)
    tri = jnp.where(causal, 1.0, 0.0).astype(BF16)
    cs = _dot_exact_lhs(tri, dta)
    last = jnp.where(col == (row // SAMPLE_ROWS) * SAMPLE_ROWS + (SAMPLE_ROWS - 1), 1.0, 0.0).astype(BF16)
    total_rows = _dot_exact_lhs(last, cs)
    cs_t = cs.T
    e = e_ref[...]
    dt_x = _dot_exact_rhs(dt, e)
    ecs_x = _dot_exact_rhs(jnp.exp(cs), e)
    dec_x = _dot_exact_rhs(jnp.exp(total_rows - cs), e)
    x_dt = xs * dt_x
    x_b = x_dt.astype(BF16)
    x_dec_t = (x_dt * dec_x).T.astype(BF16)

    seq_of_row_wide = lax.broadcasted_iota(jnp.int32, (T, GROUP_WIDTH), 0) // SAMPLE_ROWS
    seq_of_row = lax.broadcasted_iota(jnp.int32, (T, SSD_STATE), 0) // SAMPLE_ROWS
    y_parts = []
    for g in range(SSD_GROUPS):
        gs = slice(g * SSD_STATE, (g + 1) * SSD_STATE)
        c_g, b_g = cmat[:, gs], bmat[:, gs]
        xt_g = x_dec_t[g * GROUP_WIDTH:(g + 1) * GROUP_WIDTH, :]
        y_off = jnp.zeros((T, GROUP_WIDTH), F32)
        for b in range(n_seq):
            s_bg = s0_ref[0, b, g * HEADS_PER_GROUP:(g + 1) * HEADS_PER_GROUP].reshape(GROUP_WIDTH, SSD_STATE)
            y_off = jnp.where(seq_of_row_wide == b, _dot_nt(c_g, s_bg.astype(BF16)), y_off)
            ds = _dot(xt_g, jnp.where(seq_of_row == b, b_g, jnp.zeros_like(b_g)))
            decay = jnp.exp(cs[b * SAMPLE_ROWS + SAMPLE_ROWS - 1:(b + 1) * SAMPLE_ROWS, :])
            for r in range(HEADS_PER_GROUP):
                h = g * HEADS_PER_GROUP + r
                sfin_ref[0, b, h] = (s0_ref[0, b, h] * decay[:, h:h + 1]
                                  + ds[r * SSD_HEAD_DIM:(r + 1) * SSD_HEAD_DIM, :])
        y_diag = _ssd_intra_group(g, cs, cs_t, cmat, bmat, x_b, causal)
        y_parts.append(y_diag + y_off * ecs_x[:, g * GROUP_WIDTH:(g + 1) * GROUP_WIDTH])
    y = _ssd_finish(jnp.concatenate(y_parts, axis=1), xs, z_ref[...].astype(F32), dsk_ref[...], ng_ref[...])
    y_ref[...] = y.astype(y_ref.dtype)


def _history_tiles(state, width):
    n, k, c = state.shape
    tiles = []
    for d in range(1, width):
        rows = [state[:, k + t - d] if t < d else jnp.zeros((n, c), state.dtype)
                for t in range(SAMPLE_ROWS)]
        tiles.append(jnp.stack(rows, axis=1).reshape(n * SAMPLE_ROWS, c))
    return tiles


def _ssd_s(group, y, dtr, layer, s_all, conv_state, p):
    rows = y.shape[0]
    n_seq = CHUNK // SAMPLE_ROWS
    tok = lambda width, piece: pl.BlockSpec((CHUNK, width), lambda i: (i, piece))
    state_spec = pl.BlockSpec((1, n_seq, SSD_HEADS, SSD_HEAD_DIM, SSD_STATE), lambda i: (layer, i, 0, 0, 0))
    h1, h2, h3 = _history_tiles(conv_state, SSD_CONV)
    ya, sfin, raw = pl.pallas_call(
        functools.partial(_ssd_s_kernel, group.last_chunk_valid),
        grid=(rows // CHUNK,),
        in_specs=[
            tok(D_MODEL, Y_Z), tok(D_MODEL, Y_XS), tok(SSD_BC, Y_SSDB), tok(SSD_BC, Y_SSDC),
            tok(LANES, 0), state_spec,
            tok(SSD_CONV_DIM, 0), tok(SSD_CONV_DIM, 0), tok(SSD_CONV_DIM, 0),
            _const_spec((SSD_CONV, SSD_CONV_DIM)), _const_spec((1, SSD_CONV_DIM)),
            _const_spec((1, LANES)), _const_spec((1, LANES)),
            _const_spec((1, SSD_D_INNER)), _const_spec((1, SSD_D_INNER)),
            _const_spec((LANES, SSD_D_INNER)),
        ],
        out_specs=[tok(D_MODEL, 0), state_spec, tok(SSD_CONV_DIM, 0)],
        out_shape=[
            jax.ShapeDtypeStruct((rows, D_MODEL), BF16),
            jax.ShapeDtypeStruct(s_all.shape, F32),
            jax.ShapeDtypeStruct((rows, SSD_CONV_DIM), F32),
        ],
        input_output_aliases={5: 1},
        compiler_params=_cparams("arbitrary"),
        name="ssd_s",
    )(y, y, y, y, dtr, s_all, h1, h2, h3,
      p["ssd_conv_w"], p["ssd_conv_b"], p["ssd_dt_bias"], p["ssd_a_log"], p["ssd_d"],
      p["ssd_norm_g"], p["ssd_expand"])
    l = group.last_chunk_valid
    conv_new = raw.reshape(group.n_seq, SAMPLE_ROWS, SSD_CONV_DIM)[:, l - (SSD_CONV - 1):l]
    return ya, sfin, conv_new


def _ssd_kernel(l_valid, n_chunks,
                z_ref, xs_ref, bm_ref, cm_ref, dtr_ref, s0_ref, cp_ref,
                cw_ref, cb_ref, dtb_ref, alog_ref, dsk_ref, ng_ref, e_ref,
                y_ref, sfin_ref, cst_ref,
                s_scr, xp_scr):
    T = CHUNK
    c = pl.program_id(1)

    @pl.when(c == 0)
    def _():
        s_scr[...] = s0_ref[0]
        xp_scr[0:CARRY_ROWS, :] = cp_ref[0]

    xp_scr[CARRY_ROWS:CARRY_ROWS + T, 0:SSD_D_INNER] = _chunk_rows(xs_ref, T)
    xp_scr[CARRY_ROWS:CARRY_ROWS + T, SSD_D_INNER:SSD_D_INNER + SSD_BC] = _chunk_rows(bm_ref, T)
    xp_scr[CARRY_ROWS:CARRY_ROWS + T, SSD_D_INNER + SSD_BC:SSD_CONV_DIM] = _chunk_rows(cm_ref, T)
    xbc = _silu(_causal_conv(xp_scr, cw_ref, SSD_CONV, T) + cb_ref[...])
    xs = xbc[:, 0:SSD_D_INNER]
    bmat = xbc[:, SSD_D_INNER:SSD_D_INNER + SSD_BC].astype(BF16)
    cmat = xbc[:, SSD_D_INNER + SSD_BC:SSD_CONV_DIM].astype(BF16)

    dt = _softplus(_chunk_rows(dtr_ref, T) + dtb_ref[...])
    row = lax.broadcasted_iota(jnp.int32, (T, T), 0)
    col = lax.broadcasted_iota(jnp.int32, (T, T), 1)
    if l_valid < T:
        rvalid = lax.broadcasted_iota(jnp.int32, (T, LANES), 0) < l_valid
        dt = jnp.where(rvalid, dt, 0.0)
    dta = dt * (-jnp.exp(alog_ref[...]))
    causal = row >= col
    tri = jnp.where(causal, 1.0, 0.0).astype(BF16)
    cs = _dot_exact_lhs(tri, dta)
    cs_t = cs.T
    total = cs[T - 1:T, :]
    e = e_ref[...]
    dt_x = _dot_exact_rhs(dt, e)
    ecs_x = _dot_exact_rhs(jnp.exp(cs), e)
    dec_x = _dot_exact_rhs(jnp.exp(total - cs), e)
    chunk_decay = jnp.exp(total)

    x_dt = xs * dt_x
    x_b = x_dt.astype(BF16)
    x_dec = (x_dt * dec_x).astype(BF16)

    y_parts = []
    for g in range(SSD_GROUPS):
        gs = slice(g * SSD_STATE, (g + 1) * SSD_STATE)
        s_g = s_scr[g * HEADS_PER_GROUP:(g + 1) * HEADS_PER_GROUP].reshape(GROUP_WIDTH, SSD_STATE)
        y_diag = _ssd_intra_group(g, cs, cs_t, cmat, bmat, x_b, causal)
        y_off = _dot_nt(cmat[:, gs], s_g.astype(BF16))
        ds_g = _dot_tn(x_dec[:, g * GROUP_WIDTH:(g + 1) * GROUP_WIDTH], bmat[:, gs])
        y_parts.append(y_diag + y_off * ecs_x[:, g * GROUP_WIDTH:(g + 1) * GROUP_WIDTH])
        for r in range(HEADS_PER_GROUP):
            h = g * HEADS_PER_GROUP + r
            s_scr[h] = (s_scr[h] * chunk_decay[:, h:h + 1]
                        + ds_g[r * SSD_HEAD_DIM:(r + 1) * SSD_HEAD_DIM, :])

    y = _ssd_finish(jnp.concatenate(y_parts, axis=1), xs, _chunk_rows(z_ref, T), dsk_ref[...], ng_ref[...])
    y_ref[...] = y[0:y_ref.shape[0], :].astype(y_ref.dtype)

    @pl.when(c == n_chunks - 1)
    def _():
        sfin_ref[0] = s_scr[...]
        cst_ref[0] = xp_scr[pl.ds(CARRY_ROWS + l_valid - (SSD_CONV - 1), SSD_CONV - 1), :]

    if n_chunks > 1:
        xp_scr[0:CARRY_ROWS, :] = xp_scr[T:T + CARRY_ROWS, :]


def _ssd(group, y, dtr, s0, conv_prev8, p):
    nb, nc = group.n_seq, group.chunks_per_seq
    rows = y.shape[0]
    kern = functools.partial(_ssd_kernel, group.last_chunk_valid, nc)
    return pl.pallas_call(
        kern,
        grid=(nb, nc),
        in_specs=[
            _chunk_spec(group, D_MODEL, Y_Z),
            _chunk_spec(group, D_MODEL, Y_XS),
            _chunk_spec(group, SSD_BC, Y_SSDB),
            _chunk_spec(group, SSD_BC, Y_SSDC),
            _chunk_spec(group, LANES, 0),
            pl.BlockSpec((1, SSD_HEADS, SSD_HEAD_DIM, SSD_STATE), lambda b, c: (b, 0, 0, 0)),
            pl.BlockSpec((1, CARRY_ROWS, SSD_CONV_DIM), lambda b, c: (b, 0, 0)),
            _const_spec((SSD_CONV, SSD_CONV_DIM)),
            _const_spec((1, SSD_CONV_DIM)),
            _const_spec((1, LANES)),
            _const_spec((1, LANES)),
            _const_spec((1, SSD_D_INNER)),
            _const_spec((1, SSD_D_INNER)),
            _const_spec((LANES, SSD_D_INNER)),
        ],
        out_specs=[
            _chunk_spec(group, D_MODEL, 0),
            pl.BlockSpec((1, SSD_HEADS, SSD_HEAD_DIM, SSD_STATE), lambda b, c: (b, 0, 0, 0)),
            pl.BlockSpec((1, SSD_CONV - 1, SSD_CONV_DIM), lambda b, c: (b, 0, 0)),
        ],
        out_shape=[
            jax.ShapeDtypeStruct((rows, D_MODEL), BF16),
            jax.ShapeDtypeStruct((nb, SSD_HEADS, SSD_HEAD_DIM, SSD_STATE), F32),
            jax.ShapeDtypeStruct((nb, SSD_CONV - 1, SSD_CONV_DIM), F32),
        ],
        scratch_shapes=[
            pltpu.VMEM((SSD_HEADS, SSD_HEAD_DIM, SSD_STATE), F32),
            pltpu.VMEM((CARRY_ROWS + CHUNK, SSD_CONV_DIM), F32),
        ],
        compiler_params=_cparams("arbitrary", "arbitrary"),
        name="ssd",
    )(y, y, y, y, dtr, s0, conv_prev8,
      p["ssd_conv_w"], p["ssd_conv_b"], p["ssd_dt_bias"], p["ssd_a_log"], p["ssd_d"],
      p["ssd_norm_g"], p["ssd_expand"])


def _sconv_kernel(n_tiles, bg_ref, cg_ref, xs_ref, cp_ref, cw_ref, y_ref, cst_ref, xp_scr):
    T = bg_ref.shape[0]
    c = pl.program_id(1)

    @pl.when(c == 0)
    def _():
        xp_scr[0:CARRY_ROWS, :] = cp_ref[0]

    xp_scr[CARRY_ROWS:CARRY_ROWS + T, :] = cg_ref[...].astype(F32) * xs_ref[...].astype(F32)
    y = bg_ref[...].astype(F32) * _causal_conv(xp_scr, cw_ref, SC_CONV, T)
    y_ref[...] = y.astype(y_ref.dtype)

    @pl.when(c == n_tiles - 1)
    def _():
        cst_ref[0] = xp_scr[pl.ds(CARRY_ROWS + T - (SC_CONV - 1), SC_CONV - 1), :]

    xp_scr[0:CARRY_ROWS, :] = xp_scr[T:T + CARRY_ROWS, :]


def _sconv(group, y, conv_prev8, p):
    nb = group.n_seq
    tm = group.tm_rowwise
    nt = group.seq_rows // tm
    rows = y.shape[0]
    tok = lambda piece: pl.BlockSpec((tm, D_MODEL), lambda b, c: (b * nt + c, piece))
    return pl.pallas_call(
        functools.partial(_sconv_kernel, nt),
        grid=(nb, nt),
        in_specs=[
            tok(Y_BG), tok(Y_CG), tok(Y_XSC),
            pl.BlockSpec((1, CARRY_ROWS, D_MODEL), lambda b, c: (b, 0, 0)),
            _const_spec((SC_CONV, D_MODEL)),
        ],
        out_specs=[tok(0), pl.BlockSpec((1, SC_CONV - 1, D_MODEL), lambda b, c: (b, 0, 0))],
        out_shape=[
            jax.ShapeDtypeStruct((rows, D_MODEL), BF16),
            jax.ShapeDtypeStruct((nb, SC_CONV - 1, D_MODEL), F32),
        ],
        scratch_shapes=[pltpu.VMEM((CARRY_ROWS + tm, D_MODEL), F32)],
        compiler_params=_cparams("arbitrary", "arbitrary"),
        name="sconv",
    )(y, y, y, conv_prev8, p["sc_conv_w"])


def _sconv_s_kernel(bg_ref, cg_ref, xs_ref, h1_ref, h2_ref, cw_ref, y_ref, u_ref):
    u = cg_ref[...].astype(F32) * xs_ref[...].astype(F32)
    u_ref[...] = u
    pos = _seq_pos(*u.shape)
    conv = (cw_ref[2:3, :] * u + cw_ref[1:2, :] * _shift_rows(u, 1, h1_ref[...], pos)
            + cw_ref[0:1, :] * _shift_rows(u, 2, h2_ref[...], pos))
    y_ref[...] = (bg_ref[...].astype(F32) * conv).astype(y_ref.dtype)


def _sconv_s(group, y, conv_state, p):
    rows = y.shape[0]
    tm = group.tm_rowwise
    tok = lambda piece: pl.BlockSpec((tm, D_MODEL), lambda i: (i, piece))
    h1, h2 = _history_tiles(conv_state, SC_CONV)
    yb, u = pl.pallas_call(
        _sconv_s_kernel,
        grid=(rows // tm,),
        in_specs=[tok(Y_BG), tok(Y_CG), tok(Y_XSC), tok(0), tok(0), _const_spec((SC_CONV, D_MODEL))],
        out_specs=[tok(0), tok(0)],
        out_shape=[jax.ShapeDtypeStruct((rows, D_MODEL), BF16), jax.ShapeDtypeStruct((rows, D_MODEL), F32)],
        compiler_params=_cparams("arbitrary"),
        name="sconv_s",
    )(y, y, y, h1, h2, p["sc_conv_w"])
    l = group.last_chunk_valid
    return yb, u.reshape(group.n_seq, SAMPLE_ROWS, D_MODEL)[:, l - (SC_CONV - 1):l]


def _spread_kv(k_rows, v_rows):
    kf = k_rows.astype(F32)
    vf = v_rows.astype(F32)
    blk = lax.broadcasted_iota(jnp.int32, kf.shape, 1) // ATTN_HEAD_DIM
    kspread, vplaced = [], []
    for g in range(ATTN_KV_HEADS):
        tk = jnp.where(blk == g, kf, 0.0)
        uk = tk + pltpu.roll(tk, ATTN_HEAD_DIM, 1)
        kspread.append((uk + pltpu.roll(uk, 2 * ATTN_HEAD_DIM, 1)).astype(BF16))
        tv = jnp.where(blk == g, vf, 0.0)
        tv1 = pltpu.roll(tv, ATTN_HEAD_DIM, 1)
        shifted = (tv, tv1, pltpu.roll(tv, 2 * ATTN_HEAD_DIM, 1), pltpu.roll(tv1, 2 * ATTN_HEAD_DIM, 1))
        vplaced.append([shifted[(r - g) % ATTN_KV_HEADS].astype(BF16) for r in range(ATTN_GROUP)])
    return kspread, vplaced


def _attn_s_kernel(tq_rows, n_new, q_ref, k_ref, v_ref, kp_ref, vp_ref, sink_ref, bias_ref,
                   y_ref, kout_ref, vout_ref):
    T = tq_rows
    scale = ATTN_HEAD_DIM ** -0.5
    blocks = []
    n_sub = q_ref.shape[0] // SAMPLE_ROWS
    q_all, k_all, v_all = (r[...].astype(F32) for r in (q_ref, k_ref, v_ref))
    keep = WINDOW - n_new
    pad_to = lambda x, rows: jnp.concatenate([x, jnp.zeros((rows - x.shape[0], x.shape[1]), F32)], axis=0)
    for sb in range(n_sub):
        rows = slice(sb * SAMPLE_ROWS, (sb + 1) * SAMPLE_ROWS)
        ks_cur, vp_cur = _spread_kv(pad_to(k_all[rows], WINDOW).astype(BF16),
                                    pad_to(v_all[rows], WINDOW).astype(BF16))
        ks_prev, vp_prev = _spread_kv(kp_ref[0, sb].astype(BF16), vp_ref[0, sb].astype(BF16))
        blocks.append(((pad_to(q_all[rows], T) * scale).astype(BF16), ks_prev, ks_cur, vp_prev, vp_cur))
        for src, new, dst in ((kp_ref, k_all, kout_ref), (vp_ref, v_all, vout_ref)):
            dst[0, sb, pl.ds(0, keep), :] = src[0, sb, pl.ds(n_new, keep), :]
            dst[0, sb, pl.ds(keep, n_new), :] = new[sb * SAMPLE_ROWS:sb * SAMPLE_ROWS + n_new, :]

    q_blk = lax.broadcasted_iota(jnp.int32, (T, ATTN_KV), 1) // ATTN_HEAD_DIM
    pairs = [(i, g) for i in range(len(blocks)) for g in range(ATTN_KV_HEADS)]
    scores = []
    for i, g in pairs:
        q, ks_prev, ks_cur = blocks[i][0], blocks[i][1], blocks[i][2]
        qg = q[:, g * ATTN_KV:(g + 1) * ATTN_KV]
        q4 = jnp.concatenate([jnp.where(q_blk == r, qg, jnp.zeros_like(qg))
                              for r in range(ATTN_GROUP)], axis=0)
        keys = jnp.concatenate([ks_prev[g], ks_cur[g]], axis=0)
        scores.append(_dot_nt(q4, keys))
    bias = [bias_ref[0, g * ATTN_GROUP:(g + 1) * ATTN_GROUP].reshape(ATTN_GROUP * T, 2 * WINDOW)
            for g in range(ATTN_KV_HEADS)]
    sinks = [jnp.concatenate([jnp.broadcast_to(sink_ref[:, h:h + 1], (T, 1))
                              for h in range(g * ATTN_GROUP, (g + 1) * ATTN_GROUP)], axis=0)
             for g in range(ATTN_KV_HEADS)]
    n = range(len(pairs))
    s = [scores[j] - bias[pairs[j][1]] for j in n]
    m = [jnp.maximum(jnp.max(s[j], axis=-1, keepdims=True), sinks[pairs[j][1]]) for j in n]
    e = [jnp.exp(s[j] - m[j]) for j in n]
    den = [jnp.sum(e[j], axis=-1, keepdims=True) + jnp.exp(sinks[pairs[j][1]] - m[j]) for j in n]
    pn = [(e[j] * (1.0 / den[j])).astype(BF16) for j in n]
    probs = [jnp.concatenate([pn[j][r * T:(r + 1) * T] for r in range(ATTN_GROUP)], axis=1)
             for j in n]
    outs = []
    for j, (i, g) in enumerate(pairs):
        vp_prev, vp_cur = blocks[i][3], blocks[i][4]
        vals = jnp.concatenate([blk for r in range(ATTN_GROUP)
                                for blk in (vp_prev[g][r], vp_cur[g][r])], axis=0)
        outs.append(_dot(probs[j], vals))
    per_block = [jnp.concatenate(outs[i * ATTN_KV_HEADS:(i + 1) * ATTN_KV_HEADS], axis=1)
                 for i in range(len(blocks))]
    y_ref[...] = jnp.concatenate([y[0:SAMPLE_ROWS] for y in per_block], axis=0).astype(y_ref.dtype)


def _attn_t_kernel(q_ref, k_ref, v_ref, sink_ref, bias_ref, y_ref, kprev_scr, vtprev_scr):
    T = CHUNK
    c = pl.program_id(1)
    read_slot = (c + 1) % 2
    write_slot = c % 2

    @pl.when(c == 0)
    def _():
        kprev_scr[read_slot] = jnp.zeros(kprev_scr.shape[1:], BF16)
        vtprev_scr[read_slot] = jnp.zeros(vtprev_scr.shape[1:], BF16)

    scale = ATTN_HEAD_DIM ** -0.5
    q_t = (q_ref[...].astype(F32) * scale).T.astype(BF16)
    k_cur = k_ref[...]
    vt_cur = v_ref[...].astype(F32).T.astype(BF16)
    k_all = jnp.concatenate([kprev_scr[read_slot], k_cur], axis=0)
    vt_all = jnp.concatenate([vtprev_scr[read_slot], vt_cur], axis=1)

    groups = range(ATTN_KV_HEADS)
    zero_rows = jnp.zeros((ATTN_HEAD_DIM, ATTN_GROUP * T), BF16)
    scores = []
    for g in groups:
        heads = jnp.concatenate([q_t[(g * ATTN_GROUP + r) * ATTN_HEAD_DIM:(g * ATTN_GROUP + r + 1) * ATTN_HEAD_DIM]
                                 for r in range(ATTN_GROUP)], axis=1)
        q4_t = jnp.concatenate([heads if gg == g else zero_rows for gg in groups], axis=0)
        scores.append(_dot(k_all, q4_t))
    sinks = [jnp.concatenate([jnp.broadcast_to(sink_ref[:, h:h + 1], (1, T))
                              for h in range(g * ATTN_GROUP, (g + 1) * ATTN_GROUP)], axis=1)
             for g in groups]
    s = [scores[g] - bias_ref[0, g] for g in groups]
    m = [jnp.maximum(jnp.max(s[g], axis=0, keepdims=True), sinks[g]) for g in groups]
    e = [jnp.exp(s[g] - m[g]) for g in groups]
    den = [jnp.sum(e[g], axis=0, keepdims=True) + jnp.exp(sinks[g] - m[g]) for g in groups]
    p_t = [(e[g] * (1.0 / den[g])).astype(BF16) for g in groups]
    o_t = [_dot(vt_all[g * ATTN_HEAD_DIM:(g + 1) * ATTN_HEAD_DIM, :], p_t[g]) for g in groups]
    y_t = jnp.concatenate([o_t[g][:, r * T:(r + 1) * T] for g in groups for r in range(ATTN_GROUP)],
                          axis=0)
    y_ref[...] = y_t.T.astype(y_ref.dtype)
    kprev_scr[write_slot] = k_cur
    vtprev_scr[write_slot] = vt_cur


def _attn_t(group, y, p):
    nb, nc = group.n_seq, group.chunks_per_seq
    rows = y.shape[0]
    T = CHUNK
    table = _attn_bias_table(T).reshape(2, ATTN_KV_HEADS, ATTN_GROUP, T, 2 * WINDOW)
    table = table.transpose(0, 1, 4, 2, 3).reshape(2, ATTN_KV_HEADS, 2 * WINDOW, ATTN_GROUP * T)
    return pl.pallas_call(
        _attn_t_kernel,
        grid=(nb, nc),
        in_specs=[
            _chunk_spec(group, D_MODEL, Y_Q), _chunk_spec(group, ATTN_KV, Y_K), _chunk_spec(group, ATTN_KV, Y_V),
            _const_spec((1, LANES)),
            pl.BlockSpec((1, ATTN_KV_HEADS, 2 * WINDOW, ATTN_GROUP * T), lambda b, c: (jnp.minimum(c, 1), 0, 0, 0)),
        ],
        out_specs=_chunk_spec(group, D_MODEL, 0),
        out_shape=jax.ShapeDtypeStruct((rows, D_MODEL), BF16),
        scratch_shapes=[pltpu.VMEM((2, WINDOW, ATTN_KV), BF16), pltpu.VMEM((2, ATTN_KV, WINDOW), BF16)],
        compiler_params=_cparams("arbitrary", "arbitrary"),
        name="attn_p",
    )(y, y, y, p["attn_sinks"], table)


def _attn_bias_table(q_rows):
    t = jnp.arange(q_rows)[:, None]
    j = jnp.arange(2 * WINDOW)[None, :]
    dist = WINDOW + t - j
    band = (dist >= 0) & (dist <= WINDOW)
    slopes = 2.0 ** (-8.0 * jnp.arange(1, ATTN_HEADS + 1, dtype=F32) / ATTN_HEADS)
    bias = slopes[:, None, None] * dist.astype(F32)[None]
    first = jnp.where(band & (j >= WINDOW), bias, 1e30)
    later = jnp.where(band, bias, 1e30)
    return jnp.stack([first, later], axis=0)


def _attn_s(group, y, layer, k_cache, v_cache, p):
    rows = y.shape[0]
    tq = group.attn_q_rows
    n_sub = group.attn_seqs_per_step
    blk_rows = SAMPLE_ROWS * n_sub
    blk = lambda width, piece: pl.BlockSpec((blk_rows, width), lambda b: (b, piece))
    cache_spec = pl.BlockSpec((1, n_sub, WINDOW, ATTN_KV), lambda b: (layer, b, 0, 0))
    return pl.pallas_call(
        functools.partial(_attn_s_kernel, tq, group.last_chunk_valid),
        grid=(group.n_seq // n_sub,),
        in_specs=[
            blk(D_MODEL, Y_Q), blk(ATTN_KV, Y_K), blk(ATTN_KV, Y_V), cache_spec, cache_spec,
            _const_spec((1, LANES)),
            pl.BlockSpec((1, ATTN_HEADS, tq, 2 * WINDOW), lambda b: (1, 0, 0, 0)),
        ],
        out_specs=[blk(D_MODEL, 0), cache_spec, cache_spec],
        out_shape=[jax.ShapeDtypeStruct((rows, D_MODEL), BF16),
                   jax.ShapeDtypeStruct(k_cache.shape, F32), jax.ShapeDtypeStruct(v_cache.shape, F32)],
        input_output_aliases={3: 1, 4: 2},
        compiler_params=_cparams("arbitrary"),
        name="attn_s",
    )(y, y, y, k_cache, v_cache, p["attn_sinks"], _attn_bias_table(tq))


def _gelu_tanh(x):
    return 0.5 * x * (1.0 + jnp.tanh(math.sqrt(2.0 / math.pi) * (x + 0.044715 * (x * x * x))))


def _gmlp_kernel(u_ref, v_ref, lg_ref, lb_ref, ws_ref, bs_ref, y_ref):
    T = CHUNK
    n_chunks = u_ref.shape[0] // T
    u = _gelu_tanh(u_ref[...].astype(F32))
    v = _gelu_tanh(v_ref[...].astype(F32))
    vc = v - jnp.mean(v, axis=-1, keepdims=True)
    v = vc * lax.rsqrt(jnp.mean(vc * vc, axis=-1, keepdims=True) + EPS) * lg_ref[...] + lb_ref[...]
    row = lax.broadcasted_iota(jnp.int32, (T, T), 0)
    col = lax.broadcasted_iota(jnp.int32, (T, T), 1)
    causal = row >= col
    vb = v.astype(BF16)
    mixed = [[None] * GM_GROUPS for _ in range(n_chunks)]
    for g in range(GM_GROUPS):
        w = jnp.where(causal, ws_ref[g], 0.0).astype(BF16)
        gs = slice(g * GM_GROUP_WIDTH, (g + 1) * GM_GROUP_WIDTH)
        v_g = jnp.concatenate([vb[c * T:(c + 1) * T, gs] for c in range(n_chunks)], axis=1)
        m_g = _dot(w, v_g)
        for c in range(n_chunks):
            mixed[c][g] = m_g[:, c * GM_GROUP_WIDTH:(c + 1) * GM_GROUP_WIDTH] + bs_ref[:, gs]
    mixed = jnp.concatenate([jnp.concatenate(m, axis=1) for m in mixed], axis=0)
    y_ref[...] = (u * mixed).astype(y_ref.dtype)


def _gmlp(group, y, p):
    tm = group.tm_rowwise
    rows = y.shape[0]
    tok = lambda piece: pl.BlockSpec((tm, D_MODEL), lambda i: (i, piece))
    yd = pl.pallas_call(
        _gmlp_kernel,
        grid=(rows // tm,),
        in_specs=[
            tok(Y_U), tok(Y_VGM),
            _const_spec((1, D_MODEL)),
            _const_spec((1, D_MODEL)),
            _const_spec((GM_GROUPS, CHUNK, CHUNK)),
            _const_spec((CHUNK, D_MODEL)),
        ],
        out_specs=tok(0),
        out_shape=jax.ShapeDtypeStruct((rows, D_MODEL), BF16),
        compiler_params=_cparams("arbitrary"),
        name="gmlp",
    )(y, y, p["gm_ln_g"], p["gm_ln_b"], p["gm_w_s"], p["gm_b_exp"])
    return yd, None


def _gmlp_s_kernel(n_diag, u_ref, v_ref, lg_ref, lb_ref, wd_ref, bs_ref, y_ref, v_out):
    u = _gelu_tanh(u_ref[...].astype(F32))
    v = _gelu_tanh(v_ref[...].astype(F32))
    vc = v - jnp.mean(v, axis=-1, keepdims=True)
    v = vc * lax.rsqrt(jnp.mean(vc * vc, axis=-1, keepdims=True) + EPS) * lg_ref[...] + lb_ref[...]
    mixed = bs_ref[...] + wd_ref[0] * v
    for d in range(1, n_diag):
        mixed = mixed + wd_ref[d] * pltpu.roll(v, d, 0)
    y_ref[...] = (u * mixed).astype(y_ref.dtype)
    v_out[...] = v


def _gmlp_s_tables(group, w_s, b_s):
    l = group.last_chunk_valid
    reps = group.tm_rowwise // SAMPLE_ROWS
    t = jnp.arange(SAMPLE_ROWS)
    expand = lambda per_group: jnp.tile(jnp.repeat(per_group.T, GM_GROUP_WIDTH, axis=1), (reps, 1))
    diags = []
    for d in range(l):
        src = t - d
        w = w_s[:, t, jnp.clip(src, 0, None)]
        diags.append(expand(jnp.where((src >= 0) & (t < l), w, 0.0)))
    return jnp.stack(diags, axis=0), expand(b_s[:, :SAMPLE_ROWS])


def _gmlp_s(group, y, p):
    rows = y.shape[0]
    tm = group.tm_rowwise
    l = group.last_chunk_valid
    tok = lambda piece: pl.BlockSpec((tm, D_MODEL), lambda i: (i, piece))
    wd, bs = _gmlp_s_tables(group, p["gm_w_s"], p["gm_b_s"])
    return pl.pallas_call(
        functools.partial(_gmlp_s_kernel, l),
        grid=(rows // tm,),
        in_specs=[tok(Y_U), tok(Y_VGM), _const_spec((1, D_MODEL)), _const_spec((1, D_MODEL)),
                  _const_spec((l, tm, D_MODEL)), _const_spec((tm, D_MODEL))],
        out_specs=[tok(0), tok(0)],
        out_shape=[jax.ShapeDtypeStruct((rows, D_MODEL), BF16), jax.ShapeDtypeStruct((rows, D_MODEL), F32)],
        compiler_params=_cparams("arbitrary"),
        name="gmlp_s",
    )(y, y, p["gm_ln_g"], p["gm_ln_b"], wd, bs)


def _merge_kernel(x_ref, ga_ref, ya_ref, yb_ref, yc_ref, yd_ref, g0_ref, g1_ref, g2_ref, g3_ref,
                  wb_ref, wo_ref, o_ref):
    merged = None
    for i, (y_ref, g_ref) in enumerate(((ya_ref, g0_ref), (yb_ref, g1_ref), (yc_ref, g2_ref),
                                        (yd_ref, g3_ref))):
        gate = jax.nn.sigmoid(g_ref[...].astype(F32))
        term = gate * _dot(y_ref[...], wb_ref[i])
        merged = term if merged is None else merged + term
    o = _dot(merged.astype(BF16), wo_ref[...])
    o_ref[...] = x_ref[...] + _mod_rows(ga_ref) * o


def _merge(group, x, mod, branches, y, p):
    tm = group.tm_merge
    rows = x.shape[0]
    tok = lambda piece: pl.BlockSpec((tm, D_MODEL), lambda i: (i, piece))
    return pl.pallas_call(
        _merge_kernel,
        grid=(rows // tm,),
        in_specs=[tok(0), _mod_spec(group, tm, 2)] + [tok(0)] * 4 + [tok(Y_G0 + i) for i in range(4)] + [
            pl.BlockSpec((N_BRANCH, D_MODEL, D_MODEL), lambda i: (0, 0, 0), pipeline_mode=pl.Buffered(1)),
            pl.BlockSpec((D_MODEL, D_MODEL), lambda i: (0, 0), pipeline_mode=pl.Buffered(1)),
        ],
        out_specs=tok(0),
        out_shape=jax.ShapeDtypeStruct((rows, D_MODEL), F32),
        compiler_params=_cparams("arbitrary"),
        name="merge",
    )(x, mod, *branches, y, y, y, y, p["w_branch"], p["w_o"])


def _ffn_kernel(seg8, l_valid, tiles_per_seq, final_norm,
                x_ref, g_ref, sc_ref, sh_ref, ga_ref, cp_ref, wa_ref, wg_ref, cw_ref, cb_ref, wd_ref,
                gf_ref, o_ref, cst_ref, carry_scr, xp_scr, act_scr):
    tm = x_ref.shape[0]
    i = pl.program_id(0)
    x = x_ref[...]
    h = _modnorm(x, g_ref[...], _mod_rows(sc_ref), _mod_rows(sh_ref)).astype(BF16)
    first = CARRY_ROWS - (FFN_CONV - 1)

    if not seg8:
        @pl.when(i % tiles_per_seq == 0)
        def _():
            carry_scr[...] = cp_ref[0]

    def conv_piece(up, half, cs, xp):
        off = half * D_FF
        col = slice(off + cs.start, off + cs.stop)
        if seg8:
            tb = tm // SAMPLE_ROWS
            xp[:, 0:CARRY_ROWS, :] = cp_ref[:, :, col]
            xp[:, CARRY_ROWS:CARRY_ROWS + SAMPLE_ROWS, :] = up.reshape(tb, SAMPLE_ROWS, MXU_WIDTH)
            acc = None
            for j in range(FFN_CONV):
                term = cw_ref[j:j + 1, col] * xp[:, pl.ds(first + j, SAMPLE_ROWS), :]
                acc = term if acc is None else acc + term
            cst_ref[:, :, col] = xp[:, pl.ds(CARRY_ROWS + l_valid - (FFN_CONV - 1), FFN_CONV - 1), :]
            return acc.reshape(tm, MXU_WIDTH) + cb_ref[:, col]
        xp[0:CARRY_ROWS, :] = carry_scr[:, col]
        xp[CARRY_ROWS:CARRY_ROWS + tm, :] = up
        acc = None
        for j in range(FFN_CONV):
            term = cw_ref[j:j + 1, col] * xp[pl.ds(first + j, tm), :]
            acc = term if acc is None else acc + term
        carry_scr[:, col] = xp[tm:tm + CARRY_ROWS, :]
        return acc + cb_ref[:, col]

    n_pieces = D_FF // MXU_WIDTH
    pieces = [slice(k * MXU_WIDTH, (k + 1) * MXU_WIDTH) for k in range(n_pieces)]
    ups = (_dot(h, wa_ref[:, pieces[0]]), _dot(h, wg_ref[:, pieces[0]]))
    for k in range(n_pieces):
        up_a, up_g = ups
        if k + 1 < n_pieces:
            ups = (_dot(h, wa_ref[:, pieces[k + 1]]), _dot(h, wg_ref[:, pieces[k + 1]]))
        a = conv_piece(up_a, 0, pieces[k], xp_scr.at[2 * (k % 2)])
        g = conv_piece(up_g, 1, pieces[k], xp_scr.at[2 * (k % 2) + 1])
        act_scr[:, pieces[k]] = (_silu(a) * g).astype(BF16)

    if not seg8:
        @pl.when(i % tiles_per_seq == tiles_per_seq - 1)
        def _():
            cst_ref[0] = carry_scr[CARRY_ROWS - (FFN_CONV - 1):CARRY_ROWS, :]

    out = x + _mod_rows(ga_ref) * _dot(act_scr[...], wd_ref[...])
    if final_norm:
        out = out * lax.rsqrt(jnp.mean(out * out, axis=-1, keepdims=True) + EPS) * gf_ref[...]
    o_ref[...] = out


def _ffn(group, x, mod, conv_prev8, p, g_final, final_norm):
    tm = group.tm_ffn
    rows = x.shape[0]
    seg8 = not group.is_prompt
    tiles_per_seq = 1 if seg8 else group.seq_rows // tm
    single = dict(pipeline_mode=pl.Buffered(1))
    if seg8:
        tb = tm // SAMPLE_ROWS
        cp_spec = pl.BlockSpec((tb, CARRY_ROWS, 2 * D_FF), lambda i: (i, 0, 0))
        cst_spec = pl.BlockSpec((tb, FFN_CONV - 1, 2 * D_FF), lambda i: (i, 0, 0))
        xp_shape = (tb, CARRY_ROWS + SAMPLE_ROWS, MXU_WIDTH)
    else:
        cp_spec = pl.BlockSpec((1, CARRY_ROWS, 2 * D_FF), lambda i: (i // tiles_per_seq, 0, 0))
        cst_spec = pl.BlockSpec((1, FFN_CONV - 1, 2 * D_FF), lambda i: (i // tiles_per_seq, 0, 0))
        xp_shape = (CARRY_ROWS + tm, MXU_WIDTH)
    kern = functools.partial(_ffn_kernel, seg8, group.last_chunk_valid, tiles_per_seq, final_norm)
    return pl.pallas_call(
        kern,
        grid=(rows // tm,),
        in_specs=[
            pl.BlockSpec((tm, D_MODEL), lambda i: (i, 0)),
            _const_spec((1, D_MODEL)),
            _mod_spec(group, tm, 4),
            _mod_spec(group, tm, 3),
            _mod_spec(group, tm, 5),
            cp_spec,
            pl.BlockSpec((D_MODEL, D_FF), lambda i: (0, 0), **single),
            pl.BlockSpec((D_MODEL, D_FF), lambda i: (0, 0), **single),
            _const_spec((FFN_CONV, 2 * D_FF)),
            _const_spec((1, 2 * D_FF)),
            pl.BlockSpec((D_FF, D_MODEL), lambda i: (0, 0), **single),
            _const_spec((1, D_MODEL)),
        ],
        out_specs=[pl.BlockSpec((tm, D_MODEL), lambda i: (i, 0)), cst_spec],
        out_shape=[
            jax.ShapeDtypeStruct((rows, D_MODEL), F32),
            jax.ShapeDtypeStruct((group.n_seq, FFN_CONV - 1, 2 * D_FF), F32),
        ],
        scratch_shapes=[
            pltpu.VMEM((CARRY_ROWS, 2 * D_FF), F32),
            pltpu.VMEM((4,) + xp_shape, F32),
            pltpu.VMEM((tm, D_FF), BF16),
        ],
        compiler_params=_cparams("arbitrary"),
        name="ffn",
    )(x, p["g_norm_ffn"], mod, mod, mod, conv_prev8, p["w_up_a"], p["w_up_g"], p["ffn_conv_w"],
      p["ffn_conv_b"], p["w_down"], g_final)


class _Group:
    def __init__(self, is_prompt, n_seq, seq_len):
        self.is_prompt = is_prompt
        self.n_seq = n_seq
        if is_prompt:
            self.seq_rows = seq_len
            self.chunk_rows = CHUNK
            self.chunks_per_seq = seq_len // CHUNK
            self.last_chunk_valid = CHUNK
            self.per_token_mod = False
            self.tm_inproj = 1024
            self.tm_merge = 512
            self.tm_ffn = 512
            self.tm_rowwise = 512
        else:
            self.attn_q_rows = BF16_SUBLANES
            self.tm_rowwise = 256
            self.attn_seqs_per_step = 4
            self.seq_rows = SAMPLE_ROWS
            self.chunk_rows = SAMPLE_ROWS
            self.chunks_per_seq = 1
            self.last_chunk_valid = seq_len
            self.per_token_mod = True
            self.tm_inproj = 512
            self.tm_merge = 512
            self.tm_ffn = 128


def _pad_front(state, rows):
    return jnp.pad(state, ((0, 0), (rows - state.shape[1], 0), (0, 0)))


def _layer_params(i, w):
    pad_heads = lambda v: jnp.pad(v, (0, LANES - SSD_HEADS)).reshape(1, LANES)
    w_in = w["w_in"][i]
    head_of_channel = jnp.arange(SSD_D_INNER) // SSD_HEAD_DIM
    expand = (jnp.arange(LANES)[:, None] == head_of_channel[None, :]).astype(BF16)
    return {
        "g_norm_mix": w["g_norm_mix"][i].reshape(1, D_MODEL),
        "w_main": jnp.concatenate([w_in[:, a:b] for a, b in _W_IN_PIECES], axis=1).astype(BF16),
        "w_dt": jnp.pad(w_in[:, _IN_OFF_DT:_IN_OFF_DT + SSD_HEADS],
                        ((0, 0), (0, LANES - SSD_HEADS))).astype(BF16),
        "ssd_conv_w": w["ssd_conv_w"][i],
        "ssd_conv_b": w["ssd_conv_b"][i].reshape(1, SSD_CONV_DIM),
        "ssd_dt_bias": pad_heads(w["ssd_dt_bias"][i]),
        "ssd_a_log": pad_heads(w["ssd_a_log"][i]),
        "ssd_d": jnp.repeat(w["ssd_d"][i], SSD_HEAD_DIM).reshape(1, SSD_D_INNER),
        "ssd_norm_g": w["ssd_norm_g"][i].reshape(1, SSD_D_INNER),
        "ssd_expand": expand,
        "sc_conv_w": w["sc_conv_w"][i],
        "attn_sinks": jnp.pad(w["attn_sinks"][i], (0, LANES - ATTN_HEADS)).reshape(1, LANES),
        "gm_ln_g": w["gm_ln_g"][i].reshape(1, D_MODEL),
        "gm_ln_b": w["gm_ln_b"][i].reshape(1, D_MODEL),
        "gm_w_s": w["gm_w_s"][i],
        "gm_b_s": w["gm_b_s"][i],
        "gm_b_exp": jnp.repeat(w["gm_b_s"][i].T, GM_GROUP_WIDTH, axis=1),
        "w_branch": w["w_branch"][i].astype(BF16),
        "w_o": w["w_o"][i].astype(BF16),
        "g_norm_ffn": w["g_norm_ffn"][i].reshape(1, D_MODEL),
        "w_up_a": w["ffn_w_up"][i][:, :D_FF].astype(BF16),
        "w_up_g": w["ffn_w_up"][i][:, D_FF:].astype(BF16),
        "ffn_conv_w": w["ffn_conv_w"][i],
        "ffn_conv_b": w["ffn_conv_b"][i].reshape(1, 2 * D_FF),
        "w_down": w["ffn_w_down"][i].astype(BF16),
    }


def _run_group(group, x, mod_all, states, params, g_final):
    n = group.n_seq
    outs = []
    if not group.is_prompt:
        ssm_all = states[0]
        k_all = states[3].reshape(DEPTH, n, WINDOW, ATTN_KV)
        v_all = states[4].reshape(DEPTH, n, WINDOW, ATTN_KV)
    for i in range(DEPTH):
        p = params[i]
        mod = mod_all[i]
        y, dtr = _inproj(group, x, mod, p["g_norm_mix"], p["w_main"], p["w_dt"])
        if group.is_prompt:
            ssm0 = jnp.zeros((n, SSD_HEADS, SSD_HEAD_DIM, SSD_STATE), F32)
            ssdc0 = jnp.zeros((n, CARRY_ROWS, SSD_CONV_DIM), F32)
            scc0 = jnp.zeros((n, CARRY_ROWS, D_MODEL), F32)
            ffc0 = jnp.zeros((n, FFN_CONV - 1, 2 * D_FF), F32)
            ya, ssm1, ssdc1 = _ssd(group, y, dtr, ssm0, ssdc0, p)
            yb, scc1 = _sconv(group, y, scc0, p)
            yd, v_rows = _gmlp(group, y, p)
            yc = _attn_t(group, y, p)
        else:
            ffc0 = states[5][i]
            ya, ssm_all, ssdc1 = _ssd_s(group, y, dtr, i, ssm_all, states[1][i], p)
            yb, scc1 = _sconv_s(group, y, states[2][i], p)
            yd, v_rows = _gmlp_s(group, y, p)
            yc, k_all, v_all = _attn_s(group, y, i, k_all, v_all, p)
        x = _merge(group, x, mod, (ya, yb, yc, yd), y, p)
        x, ffc1 = _ffn(group, x, mod, _pad_front(ffc0, CARRY_ROWS), p, g_final, i == DEPTH - 1)

        kcol = slice(Y_K * SSD_BC, (Y_K + 1) * SSD_BC)
        vcol = slice(Y_V * SSD_BC, (Y_V + 1) * SSD_BC)
        if group.is_prompt:
            y3 = y.reshape(n, group.seq_rows, Y_COLS)
            k1 = y3[:, group.seq_rows - WINDOW:, kcol].astype(F32)
            v1 = y3[:, group.seq_rows - WINDOW:, vcol].astype(F32)
            k1 = k1.reshape(n, WINDOW, ATTN_KV_HEADS, ATTN_HEAD_DIM)
            v1 = v1.reshape(n, WINDOW, ATTN_KV_HEADS, ATTN_HEAD_DIM)
            outs.append((ssm1, ssdc1, scc1, k1, v1, ffc1))
        else:
            gv = v_rows.reshape(n, SAMPLE_ROWS, D_MODEL)[:, :group.last_chunk_valid]
            outs.append((ssdc1, scc1, ffc1, gv))
    stacked = tuple(jnp.stack([o[j] for o in outs], axis=0) for j in range(len(outs[0])))
    if not group.is_prompt:
        ssdc, scc, ffc, gv = stacked
        cache_shape = (DEPTH, n, WINDOW, ATTN_KV_HEADS, ATTN_HEAD_DIM)
        stacked = (ssm_all, ssdc, scc, k_all.reshape(cache_shape), v_all.reshape(cache_shape), ffc, gv)
    return x, stacked


def kernel(x_prompt, x_sample, c_prompt, c_sample, state_ssm, state_ssd_conv, state_sc_conv, cache_k,
           cache_v, state_ffn_conv, w_ada, b_ada, g_norm_mix, w_in, ssd_conv_w, ssd_conv_b, ssd_dt_bias,
           ssd_a_log, ssd_d, ssd_norm_g, sc_conv_w, attn_sinks, gm_ln_g, gm_ln_b, gm_w_s, gm_b_s,
           w_branch, w_o, g_norm_ffn, ffn_w_up, ffn_conv_w, ffn_conv_b, ffn_w_down, g_final):
    weights = dict(g_norm_mix=g_norm_mix, w_in=w_in, ssd_conv_w=ssd_conv_w, ssd_conv_b=ssd_conv_b,
                   ssd_dt_bias=ssd_dt_bias, ssd_a_log=ssd_a_log, ssd_d=ssd_d, ssd_norm_g=ssd_norm_g,
                   sc_conv_w=sc_conv_w, attn_sinks=attn_sinks, gm_ln_g=gm_ln_g, gm_ln_b=gm_ln_b,
                   gm_w_s=gm_w_s, gm_b_s=gm_b_s, w_branch=w_branch, w_o=w_o, g_norm_ffn=g_norm_ffn,
                   ffn_w_up=ffn_w_up, ffn_conv_w=ffn_conv_w, ffn_conv_b=ffn_conv_b, ffn_w_down=ffn_w_down)
    params = [_layer_params(i, weights) for i in range(DEPTH)]
    gf = g_final.reshape(1, D_MODEL)

    nb_p, len_p, _ = x_prompt.shape
    nb_s, len_s, _ = x_sample.shape
    assert len_p % CHUNK == 0 and len_s <= SAMPLE_ROWS and len_s >= SSD_CONV - 1
    prompt = _Group(True, nb_p, len_p)
    sample = _Group(False, nb_s, len_s)

    c_rows = nb_p + nb_s
    c_pad = -c_rows % SAMPLE_ROWS
    c_all = jnp.pad(jnp.concatenate([c_prompt, c_sample], axis=0), ((0, c_pad), (0, 0)))
    mod = _ada(c_all, w_ada, b_ada)
    mod_p = mod[:, :nb_p].reshape(DEPTH, nb_p, 1, 6 * D_MODEL)
    mod_s = jnp.repeat(mod[:, nb_p:nb_p + nb_s], SAMPLE_ROWS, axis=1)

    xp = x_prompt.reshape(nb_p * len_p, D_MODEL)
    xs = jnp.pad(x_sample, ((0, 0), (0, SAMPLE_ROWS - len_s), (0, 0))).reshape(nb_s * SAMPLE_ROWS, D_MODEL)

    y_p, st_p = _run_group(prompt, xp, mod_p, (None,) * 6, params, gf)
    y_s, st_s = _run_group(sample, xs, mod_s,
                           (state_ssm, state_ssd_conv, state_sc_conv, cache_k, cache_v, state_ffn_conv),
                           params, gf)
    y_prompt = y_p.reshape(nb_p, len_p, D_MODEL)
    y_sample = y_s.reshape(nb_s, SAMPLE_ROWS, D_MODEL)[:, :len_s]
    return (y_prompt, y_sample) + st_p + st_s
```

```python
import functools
import math

import jax
import jax.numpy as jnp
from jax import lax
from jax.experimental import pallas as pl
from jax.experimental.pallas import tpu as pltpu

F32 = jnp.float32
BF16 = jnp.bfloat16

D_MODEL = 1024
DEPTH = 4
N_BRANCH = 4
SSD_HEADS = 16
SSD_HEAD_DIM = 64
SSD_GROUPS = 4
SSD_STATE = 64
SSD_CONV = 4
SSD_D_INNER = 1024
SSD_BC = SSD_GROUPS * SSD_STATE
SSD_CONV_DIM = SSD_D_INNER + 2 * SSD_BC
SC_CONV = 3
ATTN_HEADS = 16
ATTN_KV_HEADS = 4
ATTN_GROUP = 4
ATTN_HEAD_DIM = 64
ATTN_KV = ATTN_KV_HEADS * ATTN_HEAD_DIM
WINDOW = 128
GM_GROUPS = 8
GM_GROUP_WIDTH = 128
D_FF = 2816
FFN_CONV = 3
EPS = 1e-6

CHUNK = 128
SAMPLE_ROWS = 8
CARRY_ROWS = 8
LANES = 128
BF16_SUBLANES = 16
MXU_WIDTH = 256
VMEM_LIMIT = 56 * 1024 * 1024

Y_Z, Y_XS, Y_U, Y_VGM, Y_G0 = 0, 1, 7, 8, 9
Y_SSDB, Y_SSDC, Y_K, Y_V = 8, 9, 26, 27
Y_BG, Y_CG, Y_XSC, Y_Q = 5, 7, 9, 11
HALF = D_MODEL // 2
Y_COLS = 13 * D_MODEL
INPROJ_TN = Y_COLS // 4

_IN_OFF_XBC = 1024
_IN_OFF_DT = 2560
_IN_OFF_BCX = 2576
_IN_OFF_Q = 5648
_IN_OFF_K = 6672
_IN_OFF_V = 6928
_IN_OFF_UV = 7184
_IN_OFF_GATES = 9232
_IN_COLS = 13328
_W_IN_PIECES = ((0, _IN_OFF_DT), (_IN_OFF_BCX, _IN_COLS))


def _cparams(*sem):
    return pltpu.CompilerParams(dimension_semantics=sem, vmem_limit_bytes=VMEM_LIMIT)


def _const_spec(shape):
    zeros = (0,) * len(shape)
    return pl.BlockSpec(shape, lambda *_: zeros)


def _silu(x):
    return x * jax.nn.sigmoid(x)


def _mod_rows(ref):
    return ref[0] if len(ref.shape) == 3 else ref[...]


def _modnorm(x, g, sc, sh):
    r = lax.rsqrt(jnp.mean(x * x, axis=-1, keepdims=True) + EPS)
    return (x * r * g) * (1.0 + sc) + sh


def _split3(x):
    hi = x.astype(BF16)
    r1 = x - hi.astype(F32)
    mid = r1.astype(BF16)
    lo = (r1 - mid.astype(F32)).astype(BF16)
    return hi, mid, lo


def _dot(a, b):
    return jnp.dot(a, b, preferred_element_type=F32)


def _dot_nt(a, b):
    return lax.dot_general(a, b, (((1,), (1,)), ((), ())), preferred_element_type=F32)


def _dot_tn(a, b):
    return lax.dot_general(a, b, (((0,), (0,)), ((), ())), preferred_element_type=F32)


def _dot_exact_lhs(a_bf16, x):
    hi, mid, lo = _split3(x)
    return _dot(a_bf16, hi) + _dot(a_bf16, mid) + _dot(a_bf16, lo)


def _dot_exact_rhs(x, e_bf16):
    hi, mid, lo = _split3(x)
    return _dot(hi, e_bf16) + _dot(mid, e_bf16) + _dot(lo, e_bf16)


def _chunk_rows(ref, rows):
    x = ref[...].astype(F32)
    if x.shape[0] == rows:
        return x
    pad = jnp.zeros((rows - x.shape[0], x.shape[1]), F32)
    return jnp.concatenate([x, pad], axis=0)


def _ada_kernel(c_ref, w_ref, b_ref, o_ref):
    s = _silu(c_ref[...]).astype(BF16)
    o_ref[0] = _dot(s, w_ref[0].astype(BF16)) + b_ref[0]


def _ada(c_all, w_ada, b_ada):
    rows = c_all.shape[0]
    n_tiles = w_ada.shape[-1] // D_MODEL
    return pl.pallas_call(
        _ada_kernel,
        grid=(DEPTH, n_tiles),
        in_specs=[
            _const_spec((rows, D_MODEL)),
            pl.BlockSpec((1, D_MODEL, D_MODEL), lambda l, n: (l, 0, n)),
            pl.BlockSpec((1, 1, D_MODEL), lambda l, n: (l, 0, n)),
        ],
        out_specs=pl.BlockSpec((1, rows, D_MODEL), lambda l, n: (l, 0, n)),
        out_shape=jax.ShapeDtypeStruct((DEPTH, rows, 6 * D_MODEL), F32),
        compiler_params=_cparams("arbitrary", "arbitrary"),
        name="ada",
    )(c_all, w_ada, b_ada.reshape(DEPTH, 1, 6 * D_MODEL))


def _inproj_kernel(x_ref, g_ref, sc_ref, sh_ref, w_ref, wdt_ref, y_ref, dtr_ref, h_scr):
    @pl.when(pl.program_id(1) == 0)
    def _():
        h = _modnorm(x_ref[...], g_ref[...], _mod_rows(sc_ref), _mod_rows(sh_ref)).astype(BF16)
        h_scr[...] = h
        dtr_ref[...] = _dot(h, wdt_ref[...])

    y_ref[...] = _dot(h_scr[...], w_ref[...]).astype(y_ref.dtype)


def _mod_spec(group, tm, piece):
    if group.per_token_mod:
        return pl.BlockSpec((tm, D_MODEL), lambda i, *_: (i, piece))
    tiles_per_seq = group.seq_rows // tm
    return pl.BlockSpec((1, 1, D_MODEL), lambda i, *_: (i // tiles_per_seq, 0, piece))


def _inproj(group, x, mod, g, w_main, w_dt):
    tm = group.tm_inproj
    rows = x.shape[0]
    tn = INPROJ_TN
    return pl.pallas_call(
        _inproj_kernel,
        grid=(rows // tm, Y_COLS // tn),
        in_specs=[
            pl.BlockSpec((tm, D_MODEL), lambda i, n: (i, 0)),
            _const_spec((1, D_MODEL)),
            _mod_spec(group, tm, 1),
            _mod_spec(group, tm, 0),
            pl.BlockSpec((D_MODEL, tn), lambda i, n: (0, n)),
            _const_spec((D_MODEL, LANES)),
        ],
        out_specs=[
            pl.BlockSpec((tm, tn), lambda i, n: (i, n)),
            pl.BlockSpec((tm, LANES), lambda i, n: (i, 0)),
        ],
        out_shape=[
            jax.ShapeDtypeStruct((rows, Y_COLS), BF16),
            jax.ShapeDtypeStruct((rows, LANES), F32),
        ],
        scratch_shapes=[pltpu.VMEM((tm, D_MODEL), BF16)],
        compiler_params=_cparams("arbitrary", "arbitrary"),
        name="inproj",
    )(x, g, mod, mod, w_main, w_dt)


def _causal_conv(xp_ref, taps, rows):
    x = xp_ref[CARRY_ROWS:CARRY_ROWS + rows, :]
    tail = xp_ref[0:CARRY_ROWS, :]
    pos = lax.broadcasted_iota(jnp.int32, tail.shape, 0)
    width = len(taps)
    acc = taps[width - 1] * x
    for d in range(1, width):
        rolled = pltpu.roll(x, d, 0)
        first = jnp.where(pos >= d, rolled[0:CARRY_ROWS], pltpu.roll(tail, d, 0))
        shifted = jnp.concatenate([first, rolled[CARRY_ROWS:]], axis=0)
        acc = acc + taps[width - 1 - d] * shifted
    return acc


def _chunk_spec(group, width, piece):
    nc = group.chunks_per_seq
    return pl.BlockSpec((group.chunk_rows, width), lambda b, c: (b * nc + c, piece))


HEADS_PER_GROUP = SSD_HEADS // SSD_GROUPS
GROUP_WIDTH = HEADS_PER_GROUP * SSD_HEAD_DIM


def _softplus(x):
    return jnp.maximum(x, 0.0) + jnp.log1p(jnp.exp(-jnp.abs(x)))


def _ssd_intra_group(g, cs, cs_t, cmat, bmat, x_b, causal):
    T = cs.shape[0]
    gs = slice(g * SSD_STATE, (g + 1) * SSD_STATE)
    cb = _dot_nt(cmat[:, gs], bmat[:, gs])
    low_half = lax.broadcasted_iota(jnp.int32, (T, 2 * SSD_HEAD_DIM), 1) < SSD_HEAD_DIM
    pair_out = []
    for pr in range(HEADS_PER_GROUP // 2):
        h0 = g * HEADS_PER_GROUP + 2 * pr
        xp = x_b[:, h0 * SSD_HEAD_DIM:(h0 + 2) * SSD_HEAD_DIM]
        acc = None
        for k in range(2):
            h = h0 + k
            diff = cs[:, h:h + 1] - cs_t[h:h + 1, :]
            lmat = jnp.where(causal, jnp.exp(jnp.where(causal, diff, 0.0)), 0.0)
            m = (cb * lmat).astype(BF16)
            keep = low_half if k == 0 else jnp.logical_not(low_half)
            part = _dot(m, jnp.where(keep, xp, jnp.zeros_like(xp)))
            acc = part if acc is None else acc + part
        pair_out.append(acc)
    return jnp.concatenate(pair_out, axis=1)


def _ssd_finish(y, xs, z, dsk, ng):
    y = (y + xs * dsk) * _silu(z)
    gw = SSD_D_INNER // SSD_GROUPS
    normed = []
    for g in range(SSD_GROUPS):
        yg = y[:, g * gw:(g + 1) * gw]
        normed.append(yg * lax.rsqrt(jnp.mean(yg * yg, axis=-1, keepdims=True) + EPS))
    return jnp.concatenate(normed, axis=1) * ng


def _seq_pos(rows, width):
    return lax.broadcasted_iota(jnp.int32, (rows, width), 0) % SAMPLE_ROWS


def _shift_rows(x, d, head, pos):
    return jnp.where(pos >= d, pltpu.roll(x, d, 0), head)


def _ssd_s_kernel(l_valid, z_ref, xs_ref, bm_ref, cm_ref, dtr_ref, s0_ref, h1_ref, h2_ref, h3_ref,
                  cw_ref, cb_ref, dtb_ref, alog_ref, dsk_ref, ng_ref, e_ref,
                  y_ref, sfin_ref, raw_ref):
    T = CHUNK
    n_seq = T // SAMPLE_ROWS
    raw = jnp.concatenate([xs_ref[...].astype(F32), bm_ref[...].astype(F32),
                           cm_ref[...].astype(F32)], axis=1)
    raw_ref[...] = raw
    pos = _seq_pos(T, SSD_CONV_DIM)
    conv = (cb_ref[...] + cw_ref[3:4, :] * raw
            + cw_ref[2:3, :] * _shift_rows(raw, 1, h1_ref[...], pos)
            + cw_ref[1:2, :] * _shift_rows(raw, 2, h2_ref[...], pos)
            + cw_ref[0:1, :] * _shift_rows(raw, 3, h3_ref[...], pos))
    xbc = _silu(conv)
    xs = xbc[:, 0:SSD_D_INNER]
    bmat = xbc[:, SSD_D_INNER:SSD_D_INNER + SSD_BC].astype(BF16)
    cmat = xbc[:, SSD_D_INNER + SSD_BC:SSD_CONV_DIM].astype(BF16)

    dt = _softplus(dtr_ref[...] + dtb_ref[...])
    dt = jnp.where(_seq_pos(T, LANES) < l_valid, dt, 0.0)
    dta = dt * (-jnp.exp(alog_ref[...]))
    row = lax.broadcasted_iota(jnp.int32, (T, T), 0)
    col = lax.broadcasted_iota(jnp.int32, (T, T), 1)
    causal = (row >= col) & (row // SAMPLE_ROWS == col // SAMPLE_ROWS)
    tri = jnp.where(causal, 1.0, 0.0).astype(BF16)
    cs = _dot_exact_lhs(tri, dta)
    last = jnp.where(col == (row // SAMPLE_ROWS) * SAMPLE_ROWS + (SAMPLE_ROWS - 1), 1.0, 0.0).astype(BF16)
    total_rows = _dot_exact_lhs(last, cs)
    cs_t = cs.T
    e = e_ref[...]
    dt_x = _dot_exact_rhs(dt, e)
    ecs_x = _dot_exact_rhs(jnp.exp(cs), e)
    dec_x = _dot_exact_rhs(jnp.exp(total_rows - cs), e)
    x_dt = xs * dt_x
    x_b = x_dt.astype(BF16)
    x_dec_t = (x_dt * dec_x).T.astype(BF16)

    seq_of_row_wide = lax.broadcasted_iota(jnp.int32, (T, GROUP_WIDTH), 0) // SAMPLE_ROWS
    seq_of_row = lax.broadcasted_iota(jnp.int32, (T, SSD_STATE), 0) // SAMPLE_ROWS
    y_parts = []
    for g in range(SSD_GROUPS):
        gs = slice(g * SSD_STATE, (g + 1) * SSD_STATE)
        c_g, b_g = cmat[:, gs], bmat[:, gs]
        xt_g = x_dec_t[g * GROUP_WIDTH:(g + 1) * GROUP_WIDTH, :]
        y_off = jnp.zeros((T, GROUP_WIDTH), F32)
        for b in range(n_seq):
            s_bg = s0_ref[0, b, g * HEADS_PER_GROUP:(g + 1) * HEADS_PER_GROUP].reshape(GROUP_WIDTH, SSD_STATE)
            y_off = jnp.where(seq_of_row_wide == b, _dot_nt(c_g, s_bg.astype(BF16)), y_off)
            ds = _dot(xt_g, jnp.where(seq_of_row == b, b_g, jnp.zeros_like(b_g)))
            decay = jnp.exp(cs[b * SAMPLE_ROWS + SAMPLE_ROWS - 1:(b + 1) * SAMPLE_ROWS, :])
            for r in range(HEADS_PER_GROUP):
                h = g * HEADS_PER_GROUP + r
                sfin_ref[0, b, h] = (s0_ref[0, b, h] * decay[:, h:h + 1]
                                  + ds[r * SSD_HEAD_DIM:(r + 1) * SSD_HEAD_DIM, :])
        y_diag = _ssd_intra_group(g, cs, cs_t, cmat, bmat, x_b, causal)
        y_parts.append(y_diag + y_off * ecs_x[:, g * GROUP_WIDTH:(g + 1) * GROUP_WIDTH])
    y = _ssd_finish(jnp.concatenate(y_parts, axis=1), xs, z_ref[...].astype(F32), dsk_ref[...], ng_ref[...])
    y_ref[...] = y.astype(y_ref.dtype)


def _history_tiles(state, width):
    n, k, c = state.shape
    tiles = []
    for d in range(1, width):
        rows = [state[:, k + t - d] if t < d else jnp.zeros((n, c), state.dtype)
                for t in range(SAMPLE_ROWS)]
        tiles.append(jnp.stack(rows, axis=1).reshape(n * SAMPLE_ROWS, c))
    return tiles


def _ssd_s(group, y, dtr, layer, s_all, conv_state, p):
    rows = y.shape[0]
    n_seq = CHUNK // SAMPLE_ROWS
    tok = lambda width, piece: pl.BlockSpec((CHUNK, width), lambda i: (i, piece))
    state_spec = pl.BlockSpec((1, n_seq, SSD_HEADS, SSD_HEAD_DIM, SSD_STATE), lambda i: (layer, i, 0, 0, 0))
    h1, h2, h3 = _history_tiles(conv_state, SSD_CONV)
    ya, sfin, raw = pl.pallas_call(
        functools.partial(_ssd_s_kernel, group.last_chunk_valid),
        grid=(rows // CHUNK,),
        in_specs=[
            tok(D_MODEL, Y_Z), tok(D_MODEL, Y_XS), tok(SSD_BC, Y_SSDB), tok(SSD_BC, Y_SSDC),
            tok(LANES, 0), state_spec,
            tok(SSD_CONV_DIM, 0), tok(SSD_CONV_DIM, 0), tok(SSD_CONV_DIM, 0),
            _const_spec((SSD_CONV, SSD_CONV_DIM)), _const_spec((1, SSD_CONV_DIM)),
            _const_spec((1, LANES)), _const_spec((1, LANES)),
            _const_spec((1, SSD_D_INNER)), _const_spec((1, SSD_D_INNER)),
            _const_spec((LANES, SSD_D_INNER)),
        ],
        out_specs=[tok(D_MODEL, 0), state_spec, tok(SSD_CONV_DIM, 0)],
        out_shape=[
            jax.ShapeDtypeStruct((rows, D_MODEL), BF16),
            jax.ShapeDtypeStruct(s_all.shape, F32),
            jax.ShapeDtypeStruct((rows, SSD_CONV_DIM), F32),
        ],
        input_output_aliases={5: 1},
        compiler_params=_cparams("arbitrary"),
        name="ssd_s",
    )(y, y, y, y, dtr, s_all, h1, h2, h3,
      p["ssd_conv_w"], p["ssd_conv_b"], p["ssd_dt_bias"], p["ssd_a_log"], p["ssd_d"],
      p["ssd_norm_g"], p["ssd_expand"])
    l = group.last_chunk_valid
    conv_new = raw.reshape(group.n_seq, SAMPLE_ROWS, SSD_CONV_DIM)[:, l - (SSD_CONV - 1):l]
    return ya, sfin, conv_new


def _ssd_kernel(l_valid, n_chunks,
                z_ref, xs_ref, bm_ref, cm_ref, dtr_ref, s0_ref, cp_ref,
                cw_ref, cb_ref, dtb_ref, alog_ref, dsk_ref, ng_ref, e_ref,
                y_ref, sfin_ref, cst_ref,
                s_scr, xp_scr):
    T = CHUNK
    c = pl.program_id(1)

    @pl.when(c == 0)
    def _():
        s_scr[...] = s0_ref[0]
        xp_scr[0:CARRY_ROWS, :] = cp_ref[0]

    xp_scr[CARRY_ROWS:CARRY_ROWS + T, 0:SSD_D_INNER] = _chunk_rows(xs_ref, T)
    xp_scr[CARRY_ROWS:CARRY_ROWS + T, SSD_D_INNER:SSD_D_INNER + SSD_BC] = _chunk_rows(bm_ref, T)
    xp_scr[CARRY_ROWS:CARRY_ROWS + T, SSD_D_INNER + SSD_BC:SSD_CONV_DIM] = _chunk_rows(cm_ref, T)
    xbc = _silu(_causal_conv(xp_scr, [cw_ref[j:j + 1, :] for j in range(SSD_CONV)], T) + cb_ref[...])
    xs = xbc[:, 0:SSD_D_INNER]
    bmat = xbc[:, SSD_D_INNER:SSD_D_INNER + SSD_BC].astype(BF16)
    cmat = xbc[:, SSD_D_INNER + SSD_BC:SSD_CONV_DIM].astype(BF16)

    dt = _softplus(_chunk_rows(dtr_ref, T) + dtb_ref[...])
    row = lax.broadcasted_iota(jnp.int32, (T, T), 0)
    col = lax.broadcasted_iota(jnp.int32, (T, T), 1)
    if l_valid < T:
        rvalid = lax.broadcasted_iota(jnp.int32, (T, LANES), 0) < l_valid
        dt = jnp.where(rvalid, dt, 0.0)
    dta = dt * (-jnp.exp(alog_ref[...]))
    causal = row >= col
    tri = jnp.where(causal, 1.0, 0.0).astype(BF16)
    cs = _dot_exact_lhs(tri, dta)
    cs_t = cs.T
    total = cs[T - 1:T, :]
    e = e_ref[...]
    dt_x = _dot_exact_rhs(dt, e)
    ecs_x = _dot_exact_rhs(jnp.exp(cs), e)
    dec_x = _dot_exact_rhs(jnp.exp(total - cs), e)
    chunk_decay = jnp.exp(total)

    x_dt = xs * dt_x
    x_b = x_dt.astype(BF16)
    x_dec = (x_dt * dec_x).astype(BF16)

    y_parts = []
    for g in range(SSD_GROUPS):
        gs = slice(g * SSD_STATE, (g + 1) * SSD_STATE)
        s_g = s_scr[g * HEADS_PER_GROUP:(g + 1) * HEADS_PER_GROUP].reshape(GROUP_WIDTH, SSD_STATE)
        y_diag = _ssd_intra_group(g, cs, cs_t, cmat, bmat, x_b, causal)
        y_off = _dot_nt(cmat[:, gs], s_g.astype(BF16))
        ds_g = _dot_tn(x_dec[:, g * GROUP_WIDTH:(g + 1) * GROUP_WIDTH], bmat[:, gs])
        y_parts.append(y_diag + y_off * ecs_x[:, g * GROUP_WIDTH:(g + 1) * GROUP_WIDTH])
        for r in range(HEADS_PER_GROUP):
            h = g * HEADS_PER_GROUP + r
            s_scr[h] = (s_scr[h] * chunk_decay[:, h:h + 1]
                        + ds_g[r * SSD_HEAD_DIM:(r + 1) * SSD_HEAD_DIM, :])

    y = _ssd_finish(jnp.concatenate(y_parts, axis=1), xs, _chunk_rows(z_ref, T), dsk_ref[...], ng_ref[...])
    y_ref[...] = y[0:y_ref.shape[0], :].astype(y_ref.dtype)

    @pl.when(c == n_chunks - 1)
    def _():
        sfin_ref[0] = s_scr[...]
        cst_ref[0] = xp_scr[pl.ds(CARRY_ROWS + l_valid - (SSD_CONV - 1), SSD_CONV - 1), :]

    if n_chunks > 1:
        xp_scr[0:CARRY_ROWS, :] = xp_scr[T:T + CARRY_ROWS, :]


def _ssd(group, y, dtr, s0, conv_prev8, p):
    nb, nc = group.n_seq, group.chunks_per_seq
    rows = y.shape[0]
    kern = functools.partial(_ssd_kernel, group.last_chunk_valid, nc)
    return pl.pallas_call(
        kern,
        grid=(nb, nc),
        in_specs=[
            _chunk_spec(group, D_MODEL, Y_Z),
            _chunk_spec(group, D_MODEL, Y_XS),
            _chunk_spec(group, SSD_BC, Y_SSDB),
            _chunk_spec(group, SSD_BC, Y_SSDC),
            _chunk_spec(group, LANES, 0),
            pl.BlockSpec((1, SSD_HEADS, SSD_HEAD_DIM, SSD_STATE), lambda b, c: (b, 0, 0, 0)),
            pl.BlockSpec((1, CARRY_ROWS, SSD_CONV_DIM), lambda b, c: (b, 0, 0)),
            _const_spec((SSD_CONV, SSD_CONV_DIM)),
            _const_spec((1, SSD_CONV_DIM)),
            _const_spec((1, LANES)),
            _const_spec((1, LANES)),
            _const_spec((1, SSD_D_INNER)),
            _const_spec((1, SSD_D_INNER)),
            _const_spec((LANES, SSD_D_INNER)),
        ],
        out_specs=[
            _chunk_spec(group, D_MODEL, 0),
            pl.BlockSpec((1, SSD_HEADS, SSD_HEAD_DIM, SSD_STATE), lambda b, c: (b, 0, 0, 0)),
            pl.BlockSpec((1, SSD_CONV - 1, SSD_CONV_DIM), lambda b, c: (b, 0, 0)),
        ],
        out_shape=[
            jax.ShapeDtypeStruct((rows, D_MODEL), BF16),
            jax.ShapeDtypeStruct((nb, SSD_HEADS, SSD_HEAD_DIM, SSD_STATE), F32),
            jax.ShapeDtypeStruct((nb, SSD_CONV - 1, SSD_CONV_DIM), F32),
        ],
        scratch_shapes=[
            pltpu.VMEM((SSD_HEADS, SSD_HEAD_DIM, SSD_STATE), F32),
            pltpu.VMEM((CARRY_ROWS + CHUNK, SSD_CONV_DIM), F32),
        ],
        compiler_params=_cparams("arbitrary", "arbitrary"),
        name="ssd",
    )(y, y, y, y, dtr, s0, conv_prev8,
      p["ssd_conv_w"], p["ssd_conv_b"], p["ssd_dt_bias"], p["ssd_a_log"], p["ssd_d"],
      p["ssd_norm_g"], p["ssd_expand"])


def _sconv_kernel(n_tiles, bg_ref, cg_ref, xs_ref, cp_ref, cw_ref, y_ref, cst_ref, xp_scr):
    T = bg_ref.shape[0]
    c = pl.program_id(2)

    @pl.when(c == 0)
    def _():
        xp_scr[0:CARRY_ROWS, :] = cp_ref[0]

    xp_scr[CARRY_ROWS:CARRY_ROWS + T, :] = cg_ref[...].astype(F32) * xs_ref[...].astype(F32)
    y = bg_ref[...].astype(F32) * _causal_conv(xp_scr, [cw_ref[j:j + 1, :] for j in range(SC_CONV)], T)
    y_ref[...] = y.astype(y_ref.dtype)

    @pl.when(c == n_tiles - 1)
    def _():
        cst_ref[0] = xp_scr[pl.ds(CARRY_ROWS + T - (SC_CONV - 1), SC_CONV - 1), :]

    xp_scr[0:CARRY_ROWS, :] = xp_scr[T:T + CARRY_ROWS, :]


def _sconv(group, y, conv_prev8, p):
    nb = group.n_seq
    tm = group.tm_rowwise
    nt = group.seq_rows // tm
    rows = y.shape[0]
    tok = lambda piece: pl.BlockSpec((tm, HALF), lambda b, h, c: (b * nt + c, piece + h))
    return pl.pallas_call(
        functools.partial(_sconv_kernel, nt),
        grid=(nb, D_MODEL // HALF, nt),
        in_specs=[
            tok(Y_BG), tok(Y_CG), tok(Y_XSC),
            pl.BlockSpec((1, CARRY_ROWS, HALF), lambda b, h, c: (b, 0, h)),
            pl.BlockSpec((SC_CONV, HALF), lambda b, h, c: (0, h)),
        ],
        out_specs=[tok(0), pl.BlockSpec((1, SC_CONV - 1, HALF), lambda b, h, c: (b, 0, h))],
        out_shape=[
            jax.ShapeDtypeStruct((rows, D_MODEL), BF16),
            jax.ShapeDtypeStruct((nb, SC_CONV - 1, D_MODEL), F32),
        ],
        scratch_shapes=[pltpu.VMEM((CARRY_ROWS + tm, HALF), F32)],
        compiler_params=_cparams("arbitrary", "arbitrary", "arbitrary"),
        name="sconv",
    )(y, y, y, conv_prev8, p["sc_conv_w"])


def _sconv_s_kernel(bg_ref, cg_ref, xs_ref, h1_ref, h2_ref, cw_ref, y_ref, u_ref):
    u = cg_ref[...].astype(F32) * xs_ref[...].astype(F32)
    u_ref[...] = u
    pos = _seq_pos(*u.shape)
    conv = (cw_ref[2:3, :] * u + cw_ref[1:2, :] * _shift_rows(u, 1, h1_ref[...], pos)
            + cw_ref[0:1, :] * _shift_rows(u, 2, h2_ref[...], pos))
    y_ref[...] = (bg_ref[...].astype(F32) * conv).astype(y_ref.dtype)


def _sconv_s(group, y, conv_state, p):
    rows = y.shape[0]
    tm = group.tm_rowwise
    tok = lambda piece: pl.BlockSpec((tm, HALF), lambda i, h: (i, piece + h))
    h1, h2 = _history_tiles(conv_state, SC_CONV)
    yb, u = pl.pallas_call(
        _sconv_s_kernel,
        grid=(rows // tm, D_MODEL // HALF),
        in_specs=[tok(Y_BG), tok(Y_CG), tok(Y_XSC), tok(0), tok(0),
                  pl.BlockSpec((SC_CONV, HALF), lambda i, h: (0, h))],
        out_specs=[tok(0), tok(0)],
        out_shape=[jax.ShapeDtypeStruct((rows, D_MODEL), BF16), jax.ShapeDtypeStruct((rows, D_MODEL), F32)],
        compiler_params=_cparams("arbitrary", "arbitrary"),
        name="sconv_s",
    )(y, y, y, h1, h2, p["sc_conv_w"])
    l = group.last_chunk_valid
    return yb, u.reshape(group.n_seq, SAMPLE_ROWS, D_MODEL)[:, l - (SC_CONV - 1):l]


def _spread_kv(k_rows, v_rows):
    kf = k_rows.astype(F32)
    vf = v_rows.astype(F32)
    blk = lax.broadcasted_iota(jnp.int32, kf.shape, 1) // ATTN_HEAD_DIM
    kspread, vplaced = [], []
    for g in range(ATTN_KV_HEADS):
        tk = jnp.where(blk == g, kf, 0.0)
        uk = tk + pltpu.roll(tk, ATTN_HEAD_DIM, 1)
        kspread.append((uk + pltpu.roll(uk, 2 * ATTN_HEAD_DIM, 1)).astype(BF16))
        tv = jnp.where(blk == g, vf, 0.0)
        tv1 = pltpu.roll(tv, ATTN_HEAD_DIM, 1)
        shifted = (tv, tv1, pltpu.roll(tv, 2 * ATTN_HEAD_DIM, 1), pltpu.roll(tv1, 2 * ATTN_HEAD_DIM, 1))
        vplaced.append([shifted[(r - g) % ATTN_KV_HEADS].astype(BF16) for r in range(ATTN_GROUP)])
    return kspread, vplaced


def _attn_s_kernel(tq_rows, n_new, qa_ref, qb_ref, k_ref, v_ref, kp_ref, vp_ref, sink_ref, bias_ref,
                   y_ref, kout_ref, vout_ref):
    T = tq_rows
    scale = ATTN_HEAD_DIM ** -0.5
    blocks = []
    n_sub = qa_ref.shape[0] // SAMPLE_ROWS
    q_all = jnp.concatenate([qa_ref[...].astype(F32), qb_ref[...].astype(F32)], axis=1)
    k_all, v_all = (r[...].astype(F32) for r in (k_ref, v_ref))
    keep = WINDOW - n_new
    pad_to = lambda x, rows: jnp.concatenate([x, jnp.zeros((rows - x.shape[0], x.shape[1]), F32)], axis=0)
    for sb in range(n_sub):
        rows = slice(sb * SAMPLE_ROWS, (sb + 1) * SAMPLE_ROWS)
        ks_cur, vp_cur = _spread_kv(pad_to(k_all[rows], WINDOW).astype(BF16),
                                    pad_to(v_all[rows], WINDOW).astype(BF16))
        ks_prev, vp_prev = _spread_kv(kp_ref[0, sb].astype(BF16), vp_ref[0, sb].astype(BF16))
        blocks.append(((pad_to(q_all[rows], T) * scale).astype(BF16), ks_prev, ks_cur, vp_prev, vp_cur))
        for src, new, dst in ((kp_ref, k_all, kout_ref), (vp_ref, v_all, vout_ref)):
            dst[0, sb, pl.ds(0, keep), :] = src[0, sb, pl.ds(n_new, keep), :]
            dst[0, sb, pl.ds(keep, n_new), :] = new[sb * SAMPLE_ROWS:sb * SAMPLE_ROWS + n_new, :]

    q_blk = lax.broadcasted_iota(jnp.int32, (T, ATTN_KV), 1) // ATTN_HEAD_DIM
    pairs = [(i, g) for i in range(len(blocks)) for g in range(ATTN_KV_HEADS)]
    scores = []
    for i, g in pairs:
        q, ks_prev, ks_cur = blocks[i][0], blocks[i][1], blocks[i][2]
        qg = q[:, g * ATTN_KV:(g + 1) * ATTN_KV]
        q4 = jnp.concatenate([jnp.where(q_blk == r, qg, jnp.zeros_like(qg))
                              for r in range(ATTN_GROUP)], axis=0)
        keys = jnp.concatenate([ks_prev[g], ks_cur[g]], axis=0)
        scores.append(_dot_nt(q4, keys))
    bias = [bias_ref[0, g * ATTN_GROUP:(g + 1) * ATTN_GROUP].reshape(ATTN_GROUP * T, 2 * WINDOW)
            for g in range(ATTN_KV_HEADS)]
    sinks = [jnp.concatenate([jnp.broadcast_to(sink_ref[:, h:h + 1], (T, 1))
                              for h in range(g * ATTN_GROUP, (g + 1) * ATTN_GROUP)], axis=0)
             for g in range(ATTN_KV_HEADS)]
    n = range(len(pairs))
    s = [scores[j] - bias[pairs[j][1]] for j in n]
    m = [jnp.maximum(jnp.max(s[j], axis=-1, keepdims=True), sinks[pairs[j][1]]) for j in n]
    e = [jnp.exp(s[j] - m[j]) for j in n]
    den = [jnp.sum(e[j], axis=-1, keepdims=True) + jnp.exp(sinks[pairs[j][1]] - m[j]) for j in n]
    pn = [(e[j] * (1.0 / den[j])).astype(BF16) for j in n]
    probs = [jnp.concatenate([pn[j][r * T:(r + 1) * T] for r in range(ATTN_GROUP)], axis=1)
             for j in n]
    outs = []
    for j, (i, g) in enumerate(pairs):
        vp_prev, vp_cur = blocks[i][3], blocks[i][4]
        vals = jnp.concatenate([blk for r in range(ATTN_GROUP)
                                for blk in (vp_prev[g][r], vp_cur[g][r])], axis=0)
        outs.append(_dot(probs[j], vals))
    per_block = [jnp.concatenate(outs[i * ATTN_KV_HEADS:(i + 1) * ATTN_KV_HEADS], axis=1)
                 for i in range(len(blocks))]
    y_ref[...] = jnp.concatenate([y[0:SAMPLE_ROWS] for y in per_block], axis=0).astype(y_ref.dtype)


def _attn_t_kernel(qa_ref, qb_ref, k_ref, v_ref, sink_ref, bias_ref, y_ref, kprev_scr, vtprev_scr):
    T = CHUNK
    c = pl.program_id(1)
    read_slot = (c + 1) % 2
    write_slot = c % 2

    @pl.when(c == 0)
    def _():
        kprev_scr[read_slot] = jnp.zeros(kprev_scr.shape[1:], BF16)
        vtprev_scr[read_slot] = jnp.zeros(vtprev_scr.shape[1:], BF16)

    scale = ATTN_HEAD_DIM ** -0.5
    q = jnp.concatenate([qa_ref[...].astype(F32), qb_ref[...].astype(F32)], axis=1)
    q_t = (q * scale).T.astype(BF16)
    k_cur = k_ref[...]
    vt_cur = v_ref[...].astype(F32).T.astype(BF16)
    k_all = jnp.concatenate([kprev_scr[read_slot], k_cur], axis=0)
    vt_all = jnp.concatenate([vtprev_scr[read_slot], vt_cur], axis=1)

    groups = range(ATTN_KV_HEADS)
    zero_rows = jnp.zeros((ATTN_HEAD_DIM, ATTN_GROUP * T), BF16)
    scores = []
    for g in groups:
        heads = jnp.concatenate([q_t[(g * ATTN_GROUP + r) * ATTN_HEAD_DIM:(g * ATTN_GROUP + r + 1) * ATTN_HEAD_DIM]
                                 for r in range(ATTN_GROUP)], axis=1)
        q4_t = jnp.concatenate([heads if gg == g else zero_rows for gg in groups], axis=0)
        scores.append(_dot(k_all, q4_t))
    sinks = [jnp.concatenate([jnp.broadcast_to(sink_ref[:, h:h + 1], (1, T))
                              for h in range(g * ATTN_GROUP, (g + 1) * ATTN_GROUP)], axis=1)
             for g in groups]
    s = [scores[g] - bias_ref[0, g] for g in groups]
    m = [jnp.maximum(jnp.max(s[g], axis=0, keepdims=True), sinks[g]) for g in groups]
    e = [jnp.exp(s[g] - m[g]) for g in groups]
    den = [jnp.sum(e[g], axis=0, keepdims=True) + jnp.exp(sinks[g] - m[g]) for g in groups]
    p_t = [(e[g] * (1.0 / den[g])).astype(BF16) for g in groups]
    o_t = [_dot(vt_all[g * ATTN_HEAD_DIM:(g + 1) * ATTN_HEAD_DIM, :], p_t[g]) for g in groups]
    y_t = jnp.concatenate([o_t[g][:, r * T:(r + 1) * T] for g in groups for r in range(ATTN_GROUP)],
                          axis=0)
    y_ref[...] = y_t.T.astype(y_ref.dtype)
    kprev_scr[write_slot] = k_cur
    vtprev_scr[write_slot] = vt_cur


def _attn_t(group, y, p):
    nb, nc = group.n_seq, group.chunks_per_seq
    rows = y.shape[0]
    T = CHUNK
    table = _attn_bias_table(T).reshape(2, ATTN_KV_HEADS, ATTN_GROUP, T, 2 * WINDOW)
    table = table.transpose(0, 1, 4, 2, 3).reshape(2, ATTN_KV_HEADS, 2 * WINDOW, ATTN_GROUP * T)
    return pl.pallas_call(
        _attn_t_kernel,
        grid=(nb, nc),
        in_specs=[
            _chunk_spec(group, HALF, Y_Q), _chunk_spec(group, HALF, Y_Q + 1),
            _chunk_spec(group, ATTN_KV, Y_K), _chunk_spec(group, ATTN_KV, Y_V),
            _const_spec((1, LANES)),
            pl.BlockSpec((1, ATTN_KV_HEADS, 2 * WINDOW, ATTN_GROUP * T), lambda b, c: (jnp.minimum(c, 1), 0, 0, 0)),
        ],
        out_specs=_chunk_spec(group, D_MODEL, 0),
        out_shape=jax.ShapeDtypeStruct((rows, D_MODEL), BF16),
        scratch_shapes=[pltpu.VMEM((2, WINDOW, ATTN_KV), BF16), pltpu.VMEM((2, ATTN_KV, WINDOW), BF16)],
        compiler_params=_cparams("arbitrary", "arbitrary"),
        name="attn_p",
    )(y, y, y, y, p["attn_sinks"], table)


def _attn_bias_table(q_rows):
    t = jnp.arange(q_rows)[:, None]
    j = jnp.arange(2 * WINDOW)[None, :]
    dist = WINDOW + t - j
    band = (dist >= 0) & (dist <= WINDOW)
    slopes = 2.0 ** (-8.0 * jnp.arange(1, ATTN_HEADS + 1, dtype=F32) / ATTN_HEADS)
    bias = slopes[:, None, None] * dist.astype(F32)[None]
    first = jnp.where(band & (j >= WINDOW), bias, 1e30)
    later = jnp.where(band, bias, 1e30)
    return jnp.stack([first, later], axis=0)


def _attn_s(group, y, layer, k_cache, v_cache, p):
    rows = y.shape[0]
    tq = group.attn_q_rows
    n_sub = group.attn_seqs_per_step
    blk_rows = SAMPLE_ROWS * n_sub
    blk = lambda width, piece: pl.BlockSpec((blk_rows, width), lambda b: (b, piece))
    cache_spec = pl.BlockSpec((1, n_sub, WINDOW, ATTN_KV), lambda b: (layer, b, 0, 0))
    return pl.pallas_call(
        functools.partial(_attn_s_kernel, tq, group.last_chunk_valid),
        grid=(group.n_seq // n_sub,),
        in_specs=[
            blk(HALF, Y_Q), blk(HALF, Y_Q + 1), blk(ATTN_KV, Y_K), blk(ATTN_KV, Y_V), cache_spec, cache_spec,
            _const_spec((1, LANES)),
            pl.BlockSpec((1, ATTN_HEADS, tq, 2 * WINDOW), lambda b: (1, 0, 0, 0)),
        ],
        out_specs=[blk(D_MODEL, 0), cache_spec, cache_spec],
        out_shape=[jax.ShapeDtypeStruct((rows, D_MODEL), BF16),
                   jax.ShapeDtypeStruct(k_cache.shape, F32), jax.ShapeDtypeStruct(v_cache.shape, F32)],
        input_output_aliases={4: 1, 5: 2},
        compiler_params=_cparams("arbitrary"),
        name="attn_s",
    )(y, y, y, y, k_cache, v_cache, p["attn_sinks"], _attn_bias_table(tq))


def _gelu_tanh(x):
    return 0.5 * x * (1.0 + jnp.tanh(math.sqrt(2.0 / math.pi) * (x + 0.044715 * (x * x * x))))


def _gmlp_kernel(u_ref, v_ref, lg_ref, lb_ref, ws_ref, bs_ref, y_ref):
    T = CHUNK
    n_chunks = u_ref.shape[0] // T
    u = _gelu_tanh(u_ref[...].astype(F32))
    v = _gelu_tanh(v_ref[...].astype(F32))
    vc = v - jnp.mean(v, axis=-1, keepdims=True)
    v = vc * lax.rsqrt(jnp.mean(vc * vc, axis=-1, keepdims=True) + EPS) * lg_ref[...] + lb_ref[...]
    row = lax.broadcasted_iota(jnp.int32, (T, T), 0)
    col = lax.broadcasted_iota(jnp.int32, (T, T), 1)
    causal = row >= col
    vb = v.astype(BF16)
    mixed = [[None] * GM_GROUPS for _ in range(n_chunks)]
    for g in range(GM_GROUPS):
        w = jnp.where(causal, ws_ref[g], 0.0).astype(BF16)
        gs = slice(g * GM_GROUP_WIDTH, (g + 1) * GM_GROUP_WIDTH)
        v_g = jnp.concatenate([vb[c * T:(c + 1) * T, gs] for c in range(n_chunks)], axis=1)
        m_g = _dot(w, v_g)
        for c in range(n_chunks):
            mixed[c][g] = m_g[:, c * GM_GROUP_WIDTH:(c + 1) * GM_GROUP_WIDTH] + bs_ref[:, gs]
    mixed = jnp.concatenate([jnp.concatenate(m, axis=1) for m in mixed], axis=0)
    y_ref[...] = (u * mixed).astype(y_ref.dtype)


def _gmlp(group, y, p):
    tm = group.tm_rowwise
    rows = y.shape[0]
    tok = lambda piece: pl.BlockSpec((tm, D_MODEL), lambda i: (i, piece))
    yd = pl.pallas_call(
        _gmlp_kernel,
        grid=(rows // tm,),
        in_specs=[
            tok(Y_U), tok(Y_VGM),
            _const_spec((1, D_MODEL)),
            _const_spec((1, D_MODEL)),
            _const_spec((GM_GROUPS, CHUNK, CHUNK)),
            _const_spec((CHUNK, D_MODEL)),
        ],
        out_specs=tok(0),
        out_shape=jax.ShapeDtypeStruct((rows, D_MODEL), BF16),
        compiler_params=_cparams("arbitrary"),
        name="gmlp",
    )(y, y, p["gm_ln_g"], p["gm_ln_b"], p["gm_w_s"], p["gm_b_exp"])
    return yd, None


def _gmlp_s_kernel(n_diag, u_ref, v_ref, lg_ref, lb_ref, wd_ref, bs_ref, y_ref, v_out):
    u = _gelu_tanh(u_ref[...].astype(F32))
    v = _gelu_tanh(v_ref[...].astype(F32))
    vc = v - jnp.mean(v, axis=-1, keepdims=True)
    v = vc * lax.rsqrt(jnp.mean(vc * vc, axis=-1, keepdims=True) + EPS) * lg_ref[...] + lb_ref[...]
    mixed = bs_ref[...] + wd_ref[0] * v
    for d in range(1, n_diag):
        mixed = mixed + wd_ref[d] * pltpu.roll(v, d, 0)
    y_ref[...] = (u * mixed).astype(y_ref.dtype)
    v_out[...] = v


def _gmlp_s_tables(group, w_s, b_s):
    l = group.last_chunk_valid
    reps = group.tm_rowwise // SAMPLE_ROWS
    t = jnp.arange(SAMPLE_ROWS)
    expand = lambda per_group: jnp.tile(jnp.repeat(per_group.T, GM_GROUP_WIDTH, axis=1), (reps, 1))
    diags = []
    for d in range(l):
        src = t - d
        w = w_s[:, t, jnp.clip(src, 0, None)]
        diags.append(expand(jnp.where((src >= 0) & (t < l), w, 0.0)))
    return jnp.stack(diags, axis=0), expand(b_s[:, :SAMPLE_ROWS])


def _gmlp_s(group, y, p):
    rows = y.shape[0]
    tm = group.tm_rowwise
    l = group.last_chunk_valid
    tok = lambda piece: pl.BlockSpec((tm, D_MODEL), lambda i: (i, piece))
    wd, bs = _gmlp_s_tables(group, p["gm_w_s"], p["gm_b_s"])
    return pl.pallas_call(
        functools.partial(_gmlp_s_kernel, l),
        grid=(rows // tm,),
        in_specs=[tok(Y_U), tok(Y_VGM), _const_spec((1, D_MODEL)), _const_spec((1, D_MODEL)),
                  _const_spec((l, tm, D_MODEL)), _const_spec((tm, D_MODEL))],
        out_specs=[tok(0), tok(0)],
        out_shape=[jax.ShapeDtypeStruct((rows, D_MODEL), BF16), jax.ShapeDtypeStruct((rows, D_MODEL), F32)],
        compiler_params=_cparams("arbitrary"),
        name="gmlp_s",
    )(y, y, p["gm_ln_g"], p["gm_ln_b"], wd, bs)


def _merge_kernel(x_ref, ga_ref, ya_ref, yb_ref, yc_ref, yd_ref, g0_ref, g1_ref, g2_ref, g3_ref,
                  wb_ref, wo_ref, o_ref):
    merged = None
    for i, (y_ref, g_ref) in enumerate(((ya_ref, g0_ref), (yb_ref, g1_ref), (yc_ref, g2_ref),
                                        (yd_ref, g3_ref))):
        gate = jax.nn.sigmoid(g_ref[...].astype(F32))
        term = gate * _dot(y_ref[...], wb_ref[i])
        merged = term if merged is None else merged + term
    o = _dot(merged.astype(BF16), wo_ref[...])
    o_ref[...] = x_ref[...] + _mod_rows(ga_ref) * o


def _merge(group, x, mod, branches, y, p):
    tm = group.tm_merge
    rows = x.shape[0]
    tok = lambda piece: pl.BlockSpec((tm, D_MODEL), lambda i: (i, piece))
    return pl.pallas_call(
        _merge_kernel,
        grid=(rows // tm,),
        in_specs=[tok(0), _mod_spec(group, tm, 2)] + [tok(0)] * 4 + [tok(Y_G0 + i) for i in range(4)] + [
            pl.BlockSpec((N_BRANCH, D_MODEL, D_MODEL), lambda i: (0, 0, 0), pipeline_mode=pl.Buffered(1)),
            pl.BlockSpec((D_MODEL, D_MODEL), lambda i: (0, 0), pipeline_mode=pl.Buffered(1)),
        ],
        out_specs=tok(0),
        out_shape=jax.ShapeDtypeStruct((rows, D_MODEL), F32),
        compiler_params=_cparams("arbitrary"),
        name="merge",
    )(x, mod, *branches, y, y, y, y, p["w_branch"], p["w_o"])


def _ffn_kernel(seg8, l_valid, tiles_per_seq, final_norm,
                x_ref, g_ref, sc_ref, sh_ref, ga_ref, cp_ref, wa_ref, wg_ref, cw_ref, cb_ref, wd_ref,
                gf_ref, o_ref, cst_ref, carry_scr, xp_scr, act_scr):
    tm = x_ref.shape[0]
    i = pl.program_id(0)
    x = x_ref[...]
    h = _modnorm(x, g_ref[...], _mod_rows(sc_ref), _mod_rows(sh_ref)).astype(BF16)
    first = CARRY_ROWS - (FFN_CONV - 1)

    if not seg8:
        @pl.when(i % tiles_per_seq == 0)
        def _():
            carry_scr[...] = cp_ref[0]

    def conv_piece(up, half, cs, xp):
        off = half * D_FF
        col = slice(off + cs.start, off + cs.stop)
        if seg8:
            tb = tm // SAMPLE_ROWS
            xp[:, 0:CARRY_ROWS, :] = cp_ref[:, :, col]
            xp[:, CARRY_ROWS:CARRY_ROWS + SAMPLE_ROWS, :] = up.reshape(tb, SAMPLE_ROWS, MXU_WIDTH)
            acc = None
            for j in range(FFN_CONV):
                term = cw_ref[j:j + 1, col] * xp[:, pl.ds(first + j, SAMPLE_ROWS), :]
                acc = term if acc is None else acc + term
            cst_ref[:, :, col] = xp[:, pl.ds(CARRY_ROWS + l_valid - (FFN_CONV - 1), FFN_CONV - 1), :]
            return acc.reshape(tm, MXU_WIDTH) + cb_ref[:, col]
        xp[0:CARRY_ROWS, :] = carry_scr[:, col]
        xp[CARRY_ROWS:CARRY_ROWS + tm, :] = up
        acc = None
        for j in range(FFN_CONV):
            term = cw_ref[j:j + 1, col] * xp[pl.ds(first + j, tm), :]
            acc = term if acc is None else acc + term
        carry_scr[:, col] = xp[tm:tm + CARRY_ROWS, :]
        return acc + cb_ref[:, col]

    n_pieces = D_FF // MXU_WIDTH
    pieces = [slice(k * MXU_WIDTH, (k + 1) * MXU_WIDTH) for k in range(n_pieces)]
    ups = (_dot(h, wa_ref[:, pieces[0]]), _dot(h, wg_ref[:, pieces[0]]))
    for k in range(n_pieces):
        up_a, up_g = ups
        if k + 1 < n_pieces:
            ups = (_dot(h, wa_ref[:, pieces[k + 1]]), _dot(h, wg_ref[:, pieces[k + 1]]))
        a = conv_piece(up_a, 0, pieces[k], xp_scr.at[2 * (k % 2)])
        g = conv_piece(up_g, 1, pieces[k], xp_scr.at[2 * (k % 2) + 1])
        act_scr[:, pieces[k]] = (_silu(a) * g).astype(BF16)

    if not seg8:
        @pl.when(i % tiles_per_seq == tiles_per_seq - 1)
        def _():
            cst_ref[0] = carry_scr[CARRY_ROWS - (FFN_CONV - 1):CARRY_ROWS, :]

    out = x + _mod_rows(ga_ref) * _dot(act_scr[...], wd_ref[...])
    if final_norm:
        out = out * lax.rsqrt(jnp.mean(out * out, axis=-1, keepdims=True) + EPS) * gf_ref[...]
    o_ref[...] = out


def _ffn(group, x, mod, conv_prev8, p, g_final, final_norm):
    tm = group.tm_ffn
    rows = x.shape[0]
    seg8 = not group.is_prompt
    tiles_per_seq = 1 if seg8 else group.seq_rows // tm
    single = dict(pipeline_mode=pl.Buffered(1))
    if seg8:
        tb = tm // SAMPLE_ROWS
        cp_spec = pl.BlockSpec((tb, CARRY_ROWS, 2 * D_FF), lambda i: (i, 0, 0))
        cst_spec = pl.BlockSpec((tb, FFN_CONV - 1, 2 * D_FF), lambda i: (i, 0, 0))
        xp_shape = (tb, CARRY_ROWS + SAMPLE_ROWS, MXU_WIDTH)
    else:
        cp_spec = pl.BlockSpec((1, CARRY_ROWS, 2 * D_FF), lambda i: (i // tiles_per_seq, 0, 0))
        cst_spec = pl.BlockSpec((1, FFN_CONV - 1, 2 * D_FF), lambda i: (i // tiles_per_seq, 0, 0))
        xp_shape = (CARRY_ROWS + tm, MXU_WIDTH)
    kern = functools.partial(_ffn_kernel, seg8, group.last_chunk_valid, tiles_per_seq, final_norm)
    return pl.pallas_call(
        kern,
        grid=(rows // tm,),
        in_specs=[
            pl.BlockSpec((tm, D_MODEL), lambda i: (i, 0)),
            _const_spec((1, D_MODEL)),
            _mod_spec(group, tm, 4),
            _mod_spec(group, tm, 3),
            _mod_spec(group, tm, 5),
            cp_spec,
            pl.BlockSpec((D_MODEL, D_FF), lambda i: (0, 0), **single),
            pl.BlockSpec((D_MODEL, D_FF), lambda i: (0, 0), **single),
            _const_spec((FFN_CONV, 2 * D_FF)),
            _const_spec((1, 2 * D_FF)),
            pl.BlockSpec((D_FF, D_MODEL), lambda i: (0, 0), **single),
            _const_spec((1, D_MODEL)),
        ],
        out_specs=[pl.BlockSpec((tm, D_MODEL), lambda i: (i, 0)), cst_spec],
        out_shape=[
            jax.ShapeDtypeStruct((rows, D_MODEL), F32),
            jax.ShapeDtypeStruct((group.n_seq, FFN_CONV - 1, 2 * D_FF), F32),
        ],
        scratch_shapes=[
            pltpu.VMEM((CARRY_ROWS, 2 * D_FF), F32),
            pltpu.VMEM((4,) + xp_shape, F32),
            pltpu.VMEM((tm, D_FF), BF16),
        ],
        compiler_params=_cparams("arbitrary"),
        name="ffn",
    )(x, p["g_norm_ffn"], mod, mod, mod, conv_prev8, p["w_up_a"], p["w_up_g"], p["ffn_conv_w"],
      p["ffn_conv_b"], p["w_down"], g_final)


class _Group:
    def __init__(self, is_prompt, n_seq, seq_len):
        self.is_prompt = is_prompt
        self.n_seq = n_seq
        if is_prompt:
            self.seq_rows = seq_len
            self.chunk_rows = CHUNK
            self.chunks_per_seq = seq_len // CHUNK
            self.last_chunk_valid = CHUNK
            self.per_token_mod = False
            self.tm_inproj = 1024
            self.tm_merge = 512
            self.tm_ffn = 512
            self.tm_rowwise = 512
        else:
            self.attn_q_rows = BF16_SUBLANES
            self.tm_rowwise = 256
            self.attn_seqs_per_step = 4
            self.seq_rows = SAMPLE_ROWS
            self.chunk_rows = SAMPLE_ROWS
            self.chunks_per_seq = 1
            self.last_chunk_valid = seq_len
            self.per_token_mod = True
            self.tm_inproj = 512
            self.tm_merge = 512
            self.tm_ffn = 128


def _pad_front(state, rows):
    return jnp.pad(state, ((0, 0), (rows - state.shape[1], 0), (0, 0)))


def _layer_params(i, w):
    pad_heads = lambda v: jnp.pad(v, (0, LANES - SSD_HEADS)).reshape(1, LANES)
    w_in = w["w_in"][i]
    head_of_channel = jnp.arange(SSD_D_INNER) // SSD_HEAD_DIM
    expand = (jnp.arange(LANES)[:, None] == head_of_channel[None, :]).astype(BF16)
    return {
        "g_norm_mix": w["g_norm_mix"][i].reshape(1, D_MODEL),
        "w_main": jnp.concatenate([w_in[:, a:b] for a, b in _W_IN_PIECES], axis=1).astype(BF16),
        "w_dt": jnp.pad(w_in[:, _IN_OFF_DT:_IN_OFF_DT + SSD_HEADS],
                        ((0, 0), (0, LANES - SSD_HEADS))).astype(BF16),
        "ssd_conv_w": w["ssd_conv_w"][i],
        "ssd_conv_b": w["ssd_conv_b"][i].reshape(1, SSD_CONV_DIM),
        "ssd_dt_bias": pad_heads(w["ssd_dt_bias"][i]),
        "ssd_a_log": pad_heads(w["ssd_a_log"][i]),
        "ssd_d": jnp.repeat(w["ssd_d"][i], SSD_HEAD_DIM).reshape(1, SSD_D_INNER),
        "ssd_norm_g": w["ssd_norm_g"][i].reshape(1, SSD_D_INNER),
        "ssd_expand": expand,
        "sc_conv_w": w["sc_conv_w"][i],
        "attn_sinks": jnp.pad(w["attn_sinks"][i], (0, LANES - ATTN_HEADS)).reshape(1, LANES),
        "gm_ln_g": w["gm_ln_g"][i].reshape(1, D_MODEL),
        "gm_ln_b": w["gm_ln_b"][i].reshape(1, D_MODEL),
        "gm_w_s": w["gm_w_s"][i],
        "gm_b_s": w["gm_b_s"][i],
        "gm_b_exp": jnp.repeat(w["gm_b_s"][i].T, GM_GROUP_WIDTH, axis=1),
        "w_branch": w["w_branch"][i].astype(BF16),
        "w_o": w["w_o"][i].astype(BF16),
        "g_norm_ffn": w["g_norm_ffn"][i].reshape(1, D_MODEL),
        "w_up_a": w["ffn_w_up"][i][:, :D_FF].astype(BF16),
        "w_up_g": w["ffn_w_up"][i][:, D_FF:].astype(BF16),
        "ffn_conv_w": w["ffn_conv_w"][i],
        "ffn_conv_b": w["ffn_conv_b"][i].reshape(1, 2 * D_FF),
        "w_down": w["ffn_w_down"][i].astype(BF16),
    }


def _run_group(group, x, mod_all, states, params, g_final):
    n = group.n_seq
    outs = []
    if not group.is_prompt:
        ssm_all = states[0]
        k_all = states[3].reshape(DEPTH, n, WINDOW, ATTN_KV)
        v_all = states[4].reshape(DEPTH, n, WINDOW, ATTN_KV)
    for i in range(DEPTH):
        p = params[i]
        mod = mod_all[i]
        y, dtr = _inproj(group, x, mod, p["g_norm_mix"], p["w_main"], p["w_dt"])
        if group.is_prompt:
            ssm0 = jnp.zeros((n, SSD_HEADS, SSD_HEAD_DIM, SSD_STATE), F32)
            ssdc0 = jnp.zeros((n, CARRY_ROWS, SSD_CONV_DIM), F32)
            scc0 = jnp.zeros((n, CARRY_ROWS, D_MODEL), F32)
            ffc0 = jnp.zeros((n, FFN_CONV - 1, 2 * D_FF), F32)
            ya, ssm1, ssdc1 = _ssd(group, y, dtr, ssm0, ssdc0, p)
            yb, scc1 = _sconv(group, y, scc0, p)
            yd, v_rows = _gmlp(group, y, p)
            yc = _attn_t(group, y, p)
        else:
            ffc0 = states[5][i]
            ya, ssm_all, ssdc1 = _ssd_s(group, y, dtr, i, ssm_all, states[1][i], p)
            yb, scc1 = _sconv_s(group, y, states[2][i], p)
            yd, v_rows = _gmlp_s(group, y, p)
            yc, k_all, v_all = _attn_s(group, y, i, k_all, v_all, p)
        x = _merge(group, x, mod, (ya, yb, yc, yd), y, p)
        x, ffc1 = _ffn(group, x, mod, _pad_front(ffc0, CARRY_ROWS), p, g_final, i == DEPTH - 1)

        kcol = slice(Y_K * SSD_BC, (Y_K + 1) * SSD_BC)
        vcol = slice(Y_V * SSD_BC, (Y_V + 1) * SSD_BC)
        if group.is_prompt:
            y3 = y.reshape(n, group.seq_rows, Y_COLS)
            k1 = y3[:, group.seq_rows - WINDOW:, kcol].astype(F32)
            v1 = y3[:, group.seq_rows - WINDOW:, vcol].astype(F32)
            k1 = k1.reshape(n, WINDOW, ATTN_KV_HEADS, ATTN_HEAD_DIM)
            v1 = v1.reshape(n, WINDOW, ATTN_KV_HEADS, ATTN_HEAD_DIM)
            outs.append((ssm1, ssdc1, scc1, k1, v1, ffc1))
        else:
            gv = v_rows.reshape(n, SAMPLE_ROWS, D_MODEL)[:, :group.last_chunk_valid]
            outs.append((ssdc1, scc1, ffc1, gv))
    stacked = tuple(jnp.stack([o[j] for o in outs], axis=0) for j in range(len(outs[0])))
    if not group.is_prompt:
        ssdc, scc, ffc, gv = stacked
        cache_shape = (DEPTH, n, WINDOW, ATTN_KV_HEADS, ATTN_HEAD_DIM)
        stacked = (ssm_all, ssdc, scc, k_all.reshape(cache_shape), v_all.reshape(cache_shape), ffc, gv)
    return x, stacked


def kernel(x_prompt, x_sample, c_prompt, c_sample, state_ssm, state_ssd_conv, state_sc_conv, cache_k,
           cache_v, state_ffn_conv, w_ada, b_ada, g_norm_mix, w_in, ssd_conv_w, ssd_conv_b, ssd_dt_bias,
           ssd_a_log, ssd_d, ssd_norm_g, sc_conv_w, attn_sinks, gm_ln_g, gm_ln_b, gm_w_s, gm_b_s,
           w_branch, w_o, g_norm_ffn, ffn_w_up, ffn_conv_w, ffn_conv_b, ffn_w_down, g_final):
    weights = dict(g_norm_mix=g_norm_mix, w_in=w_in, ssd_conv_w=ssd_conv_w, ssd_conv_b=ssd_conv_b,
                   ssd_dt_bias=ssd_dt_bias, ssd_a_log=ssd_a_log, ssd_d=ssd_d, ssd_norm_g=ssd_norm_g,
                   sc_conv_w=sc_conv_w, attn_sinks=attn_sinks, gm_ln_g=gm_ln_g, gm_ln_b=gm_ln_b,
                   gm_w_s=gm_w_s, gm_b_s=gm_b_s, w_branch=w_branch, w_o=w_o, g_norm_ffn=g_norm_ffn,
                   ffn_w_up=ffn_w_up, ffn_conv_w=ffn_conv_w, ffn_conv_b=ffn_conv_b, ffn_w_down=ffn_w_down)
    params = [_layer_params(i, weights) for i in range(DEPTH)]
    gf = g_final.reshape(1, D_MODEL)

    nb_p, len_p, _ = x_prompt.shape
    nb_s, len_s, _ = x_sample.shape
    assert len_p % CHUNK == 0 and len_s <= SAMPLE_ROWS and len_s >= SSD_CONV - 1
    prompt = _Group(True, nb_p, len_p)
    sample = _Group(False, nb_s, len_s)

    c_rows = nb_p + nb_s
    c_pad = -c_rows % SAMPLE_ROWS
    c_all = jnp.pad(jnp.concatenate([c_prompt, c_sample], axis=0), ((0, c_pad), (0, 0)))
    mod = _ada(c_all, w_ada, b_ada)
    mod_p = mod[:, :nb_p].reshape(DEPTH, nb_p, 1, 6 * D_MODEL)
    mod_s = jnp.repeat(mod[:, nb_p:nb_p + nb_s], SAMPLE_ROWS, axis=1)

    xp = x_prompt.reshape(nb_p * len_p, D_MODEL)
    xs = jnp.pad(x_sample, ((0, 0), (0, SAMPLE_ROWS - len_s), (0, 0))).reshape(nb_s * SAMPLE_ROWS, D_MODEL)

    y_p, st_p = _run_group(prompt, xp, mod_p, (None,) * 6, params, gf)
    y_s, st_s = _run_group(sample, xs, mod_s,
                           (state_ssm, state_ssd_conv, state_sc_conv, cache_k, cache_v, state_ffn_conv),
                           params, gf)
    y_prompt = y_p.reshape(nb_p, len_p, D_MODEL)
    y_sample = y_s.reshape(nb_s, SAMPLE_ROWS, D_MODEL)[:, :len_s]
    return (y_prompt, y_sample) + st_p + st_s
```

```python
import functools
import math

import jax
import jax.numpy as jnp
from jax import lax
from jax.experimental import pallas as pl
from jax.experimental.pallas import tpu as pltpu

F32 = jnp.float32
BF16 = jnp.bfloat16

D_MODEL = 1024
DEPTH = 4
N_BRANCH = 4
SSD_HEADS = 16
SSD_HEAD_DIM = 64
SSD_GROUPS = 4
SSD_STATE = 64
SSD_CONV = 4
SSD_D_INNER = 1024
SSD_BC = SSD_GROUPS * SSD_STATE
SSD_CONV_DIM = SSD_D_INNER + 2 * SSD_BC
SC_CONV = 3
ATTN_HEADS = 16
ATTN_KV_HEADS = 4
ATTN_GROUP = 4
ATTN_HEAD_DIM = 64
ATTN_KV = ATTN_KV_HEADS * ATTN_HEAD_DIM
WINDOW = 128
GM_GROUPS = 8
GM_GROUP_WIDTH = 128
D_FF = 2816
FFN_CONV = 3
EPS = 1e-6

CHUNK = 128
SAMPLE_ROWS = 8
CARRY_ROWS = 8
LANES = 128
BF16_SUBLANES = 16
MXU_WIDTH = 256
VMEM_LIMIT = 56 * 1024 * 1024

(Y_Z, Y_XS, Y_BG, Y_CG, Y_XSC, Y_Q, Y_U, Y_VGM, Y_G0) = range(9)
Y_WIDE = 12
Y_SSDB, Y_SSDC, Y_K, Y_V = (Y_WIDE * 4 + i for i in range(4))
Y_COLS = Y_WIDE * D_MODEL + 4 * SSD_BC
INPROJ_TN = Y_COLS // 4

_IN_OFF_XBC = 1024
_IN_OFF_DT = 2560
_IN_OFF_BCX = 2576
_IN_OFF_Q = 5648
_IN_OFF_K = 6672
_IN_OFF_V = 6928
_IN_OFF_UV = 7184
_IN_OFF_GATES = 9232
_W_IN_PIECES = (
    (0, 1024),
    (_IN_OFF_XBC, _IN_OFF_XBC + 1024),
    (_IN_OFF_BCX, _IN_OFF_BCX + 1024),
    (_IN_OFF_BCX + 1024, _IN_OFF_BCX + 2048),
    (_IN_OFF_BCX + 2048, _IN_OFF_BCX + 3072),
    (_IN_OFF_Q, _IN_OFF_Q + 1024),
    (_IN_OFF_UV, _IN_OFF_UV + 1024),
    (_IN_OFF_UV + 1024, _IN_OFF_UV + 2048),
    (_IN_OFF_GATES, _IN_OFF_GATES + 4096),
    (_IN_OFF_XBC + 1024, _IN_OFF_XBC + 1280),
    (_IN_OFF_XBC + 1280, _IN_OFF_XBC + 1536),
    (_IN_OFF_K, _IN_OFF_K + 256),
    (_IN_OFF_V, _IN_OFF_V + 256),
)


def _cparams(*sem):
    return pltpu.CompilerParams(dimension_semantics=sem, vmem_limit_bytes=VMEM_LIMIT)


def _const_spec(shape):
    zeros = (0,) * len(shape)
    return pl.BlockSpec(shape, lambda *_: zeros)


def _silu(x):
    return x * jax.nn.sigmoid(x)


def _mod_rows(ref):
    return ref[0] if len(ref.shape) == 3 else ref[...]


def _modnorm(x, g, sc, sh):
    r = lax.rsqrt(jnp.mean(x * x, axis=-1, keepdims=True) + EPS)
    return (x * r * g) * (1.0 + sc) + sh


def _split3(x):
    hi = x.astype(BF16)
    r1 = x - hi.astype(F32)
    mid = r1.astype(BF16)
    lo = (r1 - mid.astype(F32)).astype(BF16)
    return hi, mid, lo


def _dot(a, b):
    return jnp.dot(a, b, preferred_element_type=F32)


def _dot_nt(a, b):
    return lax.dot_general(a, b, (((1,), (1,)), ((), ())), preferred_element_type=F32)


def _dot_tn(a, b):
    return lax.dot_general(a, b, (((0,), (0,)), ((), ())), preferred_element_type=F32)


def _dot_exact_lhs(a_bf16, x):
    hi, mid, lo = _split3(x)
    return _dot(a_bf16, hi) + _dot(a_bf16, mid) + _dot(a_bf16, lo)


def _dot_exact_rhs(x, e_bf16):
    hi, mid, lo = _split3(x)
    return _dot(hi, e_bf16) + _dot(mid, e_bf16) + _dot(lo, e_bf16)


def _chunk_rows(ref, rows):
    x = ref[...].astype(F32)
    if x.shape[0] == rows:
        return x
    pad = jnp.zeros((rows - x.shape[0], x.shape[1]), F32)
    return jnp.concatenate([x, pad], axis=0)


def _ada_kernel(c_ref, w_ref, b_ref, o_ref):
    s = _silu(c_ref[...]).astype(BF16)
    o_ref[0] = _dot(s, w_ref[0].astype(BF16)) + b_ref[0]


def _ada(c_all, w_ada, b_ada):
    rows = c_all.shape[0]
    n_tiles = w_ada.shape[-1] // D_MODEL
    return pl.pallas_call(
        _ada_kernel,
        grid=(DEPTH, n_tiles),
        in_specs=[
            _const_spec((rows, D_MODEL)),
            pl.BlockSpec((1, D_MODEL, D_MODEL), lambda l, n: (l, 0, n)),
            pl.BlockSpec((1, 1, D_MODEL), lambda l, n: (l, 0, n)),
        ],
        out_specs=pl.BlockSpec((1, rows, D_MODEL), lambda l, n: (l, 0, n)),
        out_shape=jax.ShapeDtypeStruct((DEPTH, rows, 6 * D_MODEL), F32),
        compiler_params=_cparams("arbitrary", "arbitrary"),
        name="ada",
    )(c_all, w_ada, b_ada.reshape(DEPTH, 1, 6 * D_MODEL))


def _inproj_kernel(x_ref, g_ref, sc_ref, sh_ref, w_ref, wdt_ref, y_ref, dtr_ref, h_scr):
    @pl.when(pl.program_id(1) == 0)
    def _():
        h = _modnorm(x_ref[...], g_ref[...], _mod_rows(sc_ref), _mod_rows(sh_ref)).astype(BF16)
        h_scr[...] = h
        dtr_ref[...] = _dot(h, wdt_ref[...])

    y_ref[...] = _dot(h_scr[...], w_ref[...]).astype(y_ref.dtype)


def _mod_spec(group, tm, piece):
    if group.per_token_mod:
        return pl.BlockSpec((tm, D_MODEL), lambda i, *_: (i, piece))
    tiles_per_seq = group.seq_rows // tm
    return pl.BlockSpec((1, 1, D_MODEL), lambda i, *_: (i // tiles_per_seq, 0, piece))


def _inproj(group, x, mod, g, w_main, w_dt):
    tm = group.tm_inproj
    rows = x.shape[0]
    tn = INPROJ_TN
    return pl.pallas_call(
        _inproj_kernel,
        grid=(rows // tm, Y_COLS // tn),
        in_specs=[
            pl.BlockSpec((tm, D_MODEL), lambda i, n: (i, 0)),
            _const_spec((1, D_MODEL)),
            _mod_spec(group, tm, 1),
            _mod_spec(group, tm, 0),
            pl.BlockSpec((D_MODEL, tn), lambda i, n: (0, n)),
            _const_spec((D_MODEL, LANES)),
        ],
        out_specs=[
            pl.BlockSpec((tm, tn), lambda i, n: (i, n)),
            pl.BlockSpec((tm, LANES), lambda i, n: (i, 0)),
        ],
        out_shape=[
            jax.ShapeDtypeStruct((rows, Y_COLS), BF16),
            jax.ShapeDtypeStruct((rows, LANES), F32),
        ],
        scratch_shapes=[pltpu.VMEM((tm, D_MODEL), BF16)],
        compiler_params=_cparams("arbitrary", "arbitrary"),
        name="inproj",
    )(x, g, mod, mod, w_main, w_dt)


def _causal_conv(xp_ref, taps, rows):
    x = xp_ref[CARRY_ROWS:CARRY_ROWS + rows, :]
    tail = xp_ref[0:CARRY_ROWS, :]
    pos = lax.broadcasted_iota(jnp.int32, tail.shape, 0)
    width = len(taps)
    acc = taps[width - 1] * x
    for d in range(1, width):
        rolled = pltpu.roll(x, d, 0)
        first = jnp.where(pos >= d, rolled[0:CARRY_ROWS], pltpu.roll(tail, d, 0))
        shifted = jnp.concatenate([first, rolled[CARRY_ROWS:]], axis=0)
        acc = acc + taps[width - 1 - d] * shifted
    return acc


def _chunk_spec(group, width, piece):
    nc = group.chunks_per_seq
    return pl.BlockSpec((group.chunk_rows, width), lambda b, c: (b * nc + c, piece))


HEADS_PER_GROUP = SSD_HEADS // SSD_GROUPS
GROUP_WIDTH = HEADS_PER_GROUP * SSD_HEAD_DIM


def _softplus(x):
    return jnp.maximum(x, 0.0) + jnp.log1p(jnp.exp(-jnp.abs(x)))


def _ssd_intra_group(g, cs, cs_t, cmat, bmat, x_b, causal):
    T = cs.shape[0]
    gs = slice(g * SSD_STATE, (g + 1) * SSD_STATE)
    cb = _dot_nt(cmat[:, gs], bmat[:, gs])
    low_half = lax.broadcasted_iota(jnp.int32, (T, 2 * SSD_HEAD_DIM), 1) < SSD_HEAD_DIM
    pair_out = []
    for pr in range(HEADS_PER_GROUP // 2):
        h0 = g * HEADS_PER_GROUP + 2 * pr
        xp = x_b[:, h0 * SSD_HEAD_DIM:(h0 + 2) * SSD_HEAD_DIM]
        acc = None
        for k in range(2):
            h = h0 + k
            diff = cs[:, h:h + 1] - cs_t[h:h + 1, :]
            lmat = jnp.where(causal, jnp.exp(jnp.where(causal, diff, 0.0)), 0.0)
            m = (cb * lmat).astype(BF16)
            keep = low_half if k == 0 else jnp.logical_not(low_half)
            part = _dot(m, jnp.where(keep, xp, jnp.zeros_like(xp)))
            acc = part if acc is None else acc + part
        pair_out.append(acc)
    return jnp.concatenate(pair_out, axis=1)


def _ssd_finish(y, xs, z, dsk, ng):
    y = (y + xs * dsk) * _silu(z)
    gw = SSD_D_INNER // SSD_GROUPS
    normed = []
    for g in range(SSD_GROUPS):
        yg = y[:, g * gw:(g + 1) * gw]
        normed.append(yg * lax.rsqrt(jnp.mean(yg * yg, axis=-1, keepdims=True) + EPS))
    return jnp.concatenate(normed, axis=1) * ng


def _seq_pos(rows, width):
    return lax.broadcasted_iota(jnp.int32, (rows, width), 0) % SAMPLE_ROWS


def _shift_rows(x, d, head, pos):
    return jnp.where(pos >= d, pltpu.roll(x, d, 0), head)


def _ssd_s_kernel(l_valid, z_ref, xs_ref, bm_ref, cm_ref, dtr_ref, s0_ref, h1_ref, h2_ref, h3_ref,
                  cw_ref, cb_ref, dtb_ref, alog_ref, dsk_ref, ng_ref, e_ref,
                  y_ref, sfin_ref, raw_ref):
    T = CHUNK
    n_seq = T // SAMPLE_ROWS
    raw = jnp.concatenate([xs_ref[...].astype(F32), bm_ref[...].astype(F32),
                           cm_ref[...].astype(F32)], axis=1)
    raw_ref[...] = raw
    pos = _seq_pos(T, SSD_CONV_DIM)
    conv = (cb_ref[...] + cw_ref[3:4, :] * raw
            + cw_ref[2:3, :] * _shift_rows(raw, 1, h1_ref[...], pos)
            + cw_ref[1:2, :] * _shift_rows(raw, 2, h2_ref[...], pos)
            + cw_ref[0:1, :] * _shift_rows(raw, 3, h3_ref[...], pos))
    xbc = _silu(conv)
    xs = xbc[:, 0:SSD_D_INNER]
    bmat = xbc[:, SSD_D_INNER:SSD_D_INNER + SSD_BC].astype(BF16)
    cmat = xbc[:, SSD_D_INNER + SSD_BC:SSD_CONV_DIM].astype(BF16)

    dt = _softplus(dtr_ref[...] + dtb_ref[...])
    dt = jnp.where(_seq_pos(T, LANES) < l_valid, dt, 0.0)
    dta = dt * (-jnp.exp(alog_ref[...]))
    row = lax.broadcasted_iota(jnp.int32, (T, T), 0)
    col = lax.broadcasted_iota(jnp.int32, (T, T), 1)
    causal = (row >= col) & (row // SAMPLE_ROWS == col // SAMPLE_ROWS)
    tri = jnp.where(causal, 1.0, 0.0).astype(BF16)
    cs = _dot_exact_lhs(tri, dta)
    last = jnp.where(col == (row // SAMPLE_ROWS) * SAMPLE_ROWS + (SAMPLE_ROWS - 1), 1.0, 0.0).astype(BF16)
    total_rows = _dot_exact_lhs(last, cs)
    cs_t = cs.T
    e = e_ref[...]
    dt_x = _dot_exact_rhs(dt, e)
    ecs_x = _dot_exact_rhs(jnp.exp(cs), e)
    dec_x = _dot_exact_rhs(jnp.exp(total_rows - cs), e)
    x_dt = xs * dt_x
    x_b = x_dt.astype(BF16)
    x_dec_t = (x_dt * dec_x).T.astype(BF16)

    seq_of_row_wide = lax.broadcasted_iota(jnp.int32, (T, GROUP_WIDTH), 0) // SAMPLE_ROWS
    seq_of_row = lax.broadcasted_iota(jnp.int32, (T, SSD_STATE), 0) // SAMPLE_ROWS
    y_parts = []
    for g in range(SSD_GROUPS):
        gs = slice(g * SSD_STATE, (g + 1) * SSD_STATE)
        c_g, b_g = cmat[:, gs], bmat[:, gs]
        xt_g = x_dec_t[g * GROUP_WIDTH:(g + 1) * GROUP_WIDTH, :]
        y_off = jnp.zeros((T, GROUP_WIDTH), F32)
        for b in range(n_seq):
            s_bg = s0_ref[0, b, g * HEADS_PER_GROUP:(g + 1) * HEADS_PER_GROUP].reshape(GROUP_WIDTH, SSD_STATE)
            y_off = jnp.where(seq_of_row_wide == b, _dot_nt(c_g, s_bg.astype(BF16)), y_off)
            ds = _dot(xt_g, jnp.where(seq_of_row == b, b_g, jnp.zeros_like(b_g)))
            decay = jnp.exp(cs[b * SAMPLE_ROWS + SAMPLE_ROWS - 1:(b + 1) * SAMPLE_ROWS, :])
            for r in range(HEADS_PER_GROUP):
                h = g * HEADS_PER_GROUP + r
                sfin_ref[0, b, h] = (s0_ref[0, b, h] * decay[:, h:h + 1]
                                  + ds[r * SSD_HEAD_DIM:(r + 1) * SSD_HEAD_DIM, :])
        y_diag = _ssd_intra_group(g, cs, cs_t, cmat, bmat, x_b, causal)
        y_parts.append(y_diag + y_off * ecs_x[:, g * GROUP_WIDTH:(g + 1) * GROUP_WIDTH])
    y = _ssd_finish(jnp.concatenate(y_parts, axis=1), xs, z_ref[...].astype(F32), dsk_ref[...], ng_ref[...])
    y_ref[...] = y.astype(y_ref.dtype)


def _history_tiles(state, width):
    n, k, c = state.shape
    tiles = []
    for d in range(1, width):
        rows = [state[:, k + t - d] if t < d else jnp.zeros((n, c), state.dtype)
                for t in range(SAMPLE_ROWS)]
        tiles.append(jnp.stack(rows, axis=1).reshape(n * SAMPLE_ROWS, c))
    return tiles


def _ssd_s(group, y, dtr, layer, s_all, conv_state, p):
    rows = y.shape[0]
    n_seq = CHUNK // SAMPLE_ROWS
    tok = lambda width, piece: pl.BlockSpec((CHUNK, width), lambda i: (i, piece))
    state_spec = pl.BlockSpec((1, n_seq, SSD_HEADS, SSD_HEAD_DIM, SSD_STATE), lambda i: (layer, i, 0, 0, 0))
    h1, h2, h3 = _history_tiles(conv_state, SSD_CONV)
    ya, sfin, raw = pl.pallas_call(
        functools.partial(_ssd_s_kernel, group.last_chunk_valid),
        grid=(rows // CHUNK,),
        in_specs=[
            tok(D_MODEL, Y_Z), tok(D_MODEL, Y_XS), tok(SSD_BC, Y_SSDB), tok(SSD_BC, Y_SSDC),
            tok(LANES, 0), state_spec,
            tok(SSD_CONV_DIM, 0), tok(SSD_CONV_DIM, 0), tok(SSD_CONV_DIM, 0),
            _const_spec((SSD_CONV, SSD_CONV_DIM)), _const_spec((1, SSD_CONV_DIM)),
            _const_spec((1, LANES)), _const_spec((1, LANES)),
            _const_spec((1, SSD_D_INNER)), _const_spec((1, SSD_D_INNER)),
            _const_spec((LANES, SSD_D_INNER)),
        ],
        out_specs=[tok(D_MODEL, 0), state_spec, tok(SSD_CONV_DIM, 0)],
        out_shape=[
            jax.ShapeDtypeStruct((rows, D_MODEL), BF16),
            jax.ShapeDtypeStruct(s_all.shape, F32),
            jax.ShapeDtypeStruct((rows, SSD_CONV_DIM), F32),
        ],
        input_output_aliases={5: 1},
        compiler_params=_cparams("arbitrary"),
        name="ssd_s",
    )(y, y, y, y, dtr, s_all, h1, h2, h3,
      p["ssd_conv_w"], p["ssd_conv_b"], p["ssd_dt_bias"], p["ssd_a_log"], p["ssd_d"],
      p["ssd_norm_g"], p["ssd_expand"])
    l = group.last_chunk_valid
    conv_new = raw.reshape(group.n_seq, SAMPLE_ROWS, SSD_CONV_DIM)[:, l - (SSD_CONV - 1):l]
    return ya, sfin, conv_new


def _ssd_kernel(l_valid, n_chunks,
                z_ref, xs_ref, bm_ref, cm_ref, dtr_ref, s0_ref, cp_ref,
                cw_ref, cb_ref, dtb_ref, alog_ref, dsk_ref, ng_ref, e_ref,
                y_ref, sfin_ref, cst_ref,
                s_scr, xp_scr):
    T = CHUNK
    c = pl.program_id(1)

    @pl.when(c == 0)
    def _():
        s_scr[...] = s0_ref[0]
        xp_scr[0:CARRY_ROWS, :] = cp_ref[0]

    xp_scr[CARRY_ROWS:CARRY_ROWS + T, 0:SSD_D_INNER] = _chunk_rows(xs_ref, T)
    xp_scr[CARRY_ROWS:CARRY_ROWS + T, SSD_D_INNER:SSD_D_INNER + SSD_BC] = _chunk_rows(bm_ref, T)
    xp_scr[CARRY_ROWS:CARRY_ROWS + T, SSD_D_INNER + SSD_BC:SSD_CONV_DIM] = _chunk_rows(cm_ref, T)
    xbc = _silu(_causal_conv(xp_scr, [cw_ref[j:j + 1, :] for j in range(SSD_CONV)], T) + cb_ref[...])
    xs = xbc[:, 0:SSD_D_INNER]
    bmat = xbc[:, SSD_D_INNER:SSD_D_INNER + SSD_BC].astype(BF16)
    cmat = xbc[:, SSD_D_INNER + SSD_BC:SSD_CONV_DIM].astype(BF16)

    dt = _softplus(_chunk_rows(dtr_ref, T) + dtb_ref[...])
    row = lax.broadcasted_iota(jnp.int32, (T, T), 0)
    col = lax.broadcasted_iota(jnp.int32, (T, T), 1)
    if l_valid < T:
        rvalid = lax.broadcasted_iota(jnp.int32, (T, LANES), 0) < l_valid
        dt = jnp.where(rvalid, dt, 0.0)
    dta = dt * (-jnp.exp(alog_ref[...]))
    causal = row >= col
    tri = jnp.where(causal, 1.0, 0.0).astype(BF16)
    cs = _dot_exact_lhs(tri, dta)
    cs_t = cs.T
    total = cs[T - 1:T, :]
    e = e_ref[...]
    dt_x = _dot_exact_rhs(dt, e)
    ecs_x = _dot_exact_rhs(jnp.exp(cs), e)
    dec_x = _dot_exact_rhs(jnp.exp(total - cs), e)
    chunk_decay = jnp.exp(total)

    x_dt = xs * dt_x
    x_b = x_dt.astype(BF16)
    x_dec = (x_dt * dec_x).astype(BF16)

    y_parts = []
    for g in range(SSD_GROUPS):
        gs = slice(g * SSD_STATE, (g + 1) * SSD_STATE)
        s_g = s_scr[g * HEADS_PER_GROUP:(g + 1) * HEADS_PER_GROUP].reshape(GROUP_WIDTH, SSD_STATE)
        y_diag = _ssd_intra_group(g, cs, cs_t, cmat, bmat, x_b, causal)
        y_off = _dot_nt(cmat[:, gs], s_g.astype(BF16))
        ds_g = _dot_tn(x_dec[:, g * GROUP_WIDTH:(g + 1) * GROUP_WIDTH], bmat[:, gs])
        y_parts.append(y_diag + y_off * ecs_x[:, g * GROUP_WIDTH:(g + 1) * GROUP_WIDTH])
        for r in range(HEADS_PER_GROUP):
            h = g * HEADS_PER_GROUP + r
            s_scr[h] = (s_scr[h] * chunk_decay[:, h:h + 1]
                        + ds_g[r * SSD_HEAD_DIM:(r + 1) * SSD_HEAD_DIM, :])

    y = _ssd_finish(jnp.concatenate(y_parts, axis=1), xs, _chunk_rows(z_ref, T), dsk_ref[...], ng_ref[...])
    y_ref[...] = y[0:y_ref.shape[0], :].astype(y_ref.dtype)

    @pl.when(c == n_chunks - 1)
    def _():
        sfin_ref[0] = s_scr[...]
        cst_ref[0] = xp_scr[pl.ds(CARRY_ROWS + l_valid - (SSD_CONV - 1), SSD_CONV - 1), :]

    if n_chunks > 1:
        xp_scr[0:CARRY_ROWS, :] = xp_scr[T:T + CARRY_ROWS, :]


def _ssd(group, y, dtr, s0, conv_prev8, p):
    nb, nc = group.n_seq, group.chunks_per_seq
    rows = y.shape[0]
    kern = functools.partial(_ssd_kernel, group.last_chunk_valid, nc)
    return pl.pallas_call(
        kern,
        grid=(nb, nc),
        in_specs=[
            _chunk_spec(group, D_MODEL, Y_Z),
            _chunk_spec(group, D_MODEL, Y_XS),
            _chunk_spec(group, SSD_BC, Y_SSDB),
            _chunk_spec(group, SSD_BC, Y_SSDC),
            _chunk_spec(group, LANES, 0),
            pl.BlockSpec((1, SSD_HEADS, SSD_HEAD_DIM, SSD_STATE), lambda b, c: (b, 0, 0, 0)),
            pl.BlockSpec((1, CARRY_ROWS, SSD_CONV_DIM), lambda b, c: (b, 0, 0)),
            _const_spec((SSD_CONV, SSD_CONV_DIM)),
            _const_spec((1, SSD_CONV_DIM)),
            _const_spec((1, LANES)),
            _const_spec((1, LANES)),
            _const_spec((1, SSD_D_INNER)),
            _const_spec((1, SSD_D_INNER)),
            _const_spec((LANES, SSD_D_INNER)),
        ],
        out_specs=[
            _chunk_spec(group, D_MODEL, 0),
            pl.BlockSpec((1, SSD_HEADS, SSD_HEAD_DIM, SSD_STATE), lambda b, c: (b, 0, 0, 0)),
            pl.BlockSpec((1, SSD_CONV - 1, SSD_CONV_DIM), lambda b, c: (b, 0, 0)),
        ],
        out_shape=[
            jax.ShapeDtypeStruct((rows, D_MODEL), BF16),
            jax.ShapeDtypeStruct((nb, SSD_HEADS, SSD_HEAD_DIM, SSD_STATE), F32),
            jax.ShapeDtypeStruct((nb, SSD_CONV - 1, SSD_CONV_DIM), F32),
        ],
        scratch_shapes=[
            pltpu.VMEM((SSD_HEADS, SSD_HEAD_DIM, SSD_STATE), F32),
            pltpu.VMEM((CARRY_ROWS + CHUNK, SSD_CONV_DIM), F32),
        ],
        compiler_params=_cparams("arbitrary", "arbitrary"),
        name="ssd",
    )(y, y, y, y, dtr, s0, conv_prev8,
      p["ssd_conv_w"], p["ssd_conv_b"], p["ssd_dt_bias"], p["ssd_a_log"], p["ssd_d"],
      p["ssd_norm_g"], p["ssd_expand"])


def _sconv_kernel(n_tiles, bg_ref, cg_ref, xs_ref, cp_ref, cw_ref, y_ref, cst_ref, xp_scr):
    T = bg_ref.shape[0]
    c = pl.program_id(1)

    @pl.when(c == 0)
    def _():
        xp_scr[0:CARRY_ROWS, :] = cp_ref[0]

    xp_scr[CARRY_ROWS:CARRY_ROWS + T, :] = cg_ref[...].astype(F32) * xs_ref[...].astype(F32)
    y = bg_ref[...].astype(F32) * _causal_conv(xp_scr, [cw_ref[j:j + 1, :] for j in range(SC_CONV)], T)
    y_ref[...] = y.astype(y_ref.dtype)

    @pl.when(c == n_tiles - 1)
    def _():
        cst_ref[0] = xp_scr[pl.ds(CARRY_ROWS + T - (SC_CONV - 1), SC_CONV - 1), :]

    xp_scr[0:CARRY_ROWS, :] = xp_scr[T:T + CARRY_ROWS, :]


def _sconv(group, y, conv_prev8, p):
    nb = group.n_seq
    tm = group.tm_rowwise
    nt = group.seq_rows // tm
    rows = y.shape[0]
    tok = lambda piece: pl.BlockSpec((tm, D_MODEL), lambda b, c: (b * nt + c, piece))
    return pl.pallas_call(
        functools.partial(_sconv_kernel, nt),
        grid=(nb, nt),
        in_specs=[
            tok(Y_BG), tok(Y_CG), tok(Y_XSC),
            pl.BlockSpec((1, CARRY_ROWS, D_MODEL), lambda b, c: (b, 0, 0)),
            _const_spec((SC_CONV, D_MODEL)),
        ],
        out_specs=[tok(0), pl.BlockSpec((1, SC_CONV - 1, D_MODEL), lambda b, c: (b, 0, 0))],
        out_shape=[
            jax.ShapeDtypeStruct((rows, D_MODEL), BF16),
            jax.ShapeDtypeStruct((nb, SC_CONV - 1, D_MODEL), F32),
        ],
        scratch_shapes=[pltpu.VMEM((CARRY_ROWS + tm, D_MODEL), F32)],
        compiler_params=_cparams("arbitrary", "arbitrary"),
        name="sconv",
    )(y, y, y, conv_prev8, p["sc_conv_w"])


def _sconv_s_kernel(bg_ref, cg_ref, xs_ref, h1_ref, h2_ref, cw_ref, y_ref, u_ref):
    u = cg_ref[...].astype(F32) * xs_ref[...].astype(F32)
    u_ref[...] = u
    pos = _seq_pos(*u.shape)
    conv = (cw_ref[2:3, :] * u + cw_ref[1:2, :] * _shift_rows(u, 1, h1_ref[...], pos)
            + cw_ref[0:1, :] * _shift_rows(u, 2, h2_ref[...], pos))
    y_ref[...] = (bg_ref[...].astype(F32) * conv).astype(y_ref.dtype)


def _sconv_s(group, y, conv_state, p):
    rows = y.shape[0]
    tm = group.tm_rowwise
    tok = lambda piece: pl.BlockSpec((tm, D_MODEL), lambda i: (i, piece))
    h1, h2 = _history_tiles(conv_state, SC_CONV)
    yb, u = pl.pallas_call(
        _sconv_s_kernel,
        grid=(rows // tm,),
        in_specs=[tok(Y_BG), tok(Y_CG), tok(Y_XSC), tok(0), tok(0), _const_spec((SC_CONV, D_MODEL))],
        out_specs=[tok(0), tok(0)],
        out_shape=[jax.ShapeDtypeStruct((rows, D_MODEL), BF16), jax.ShapeDtypeStruct((rows, D_MODEL), F32)],
        compiler_params=_cparams("arbitrary"),
        name="sconv_s",
    )(y, y, y, h1, h2, p["sc_conv_w"])
    l = group.last_chunk_valid
    return yb, u.reshape(group.n_seq, SAMPLE_ROWS, D_MODEL)[:, l - (SC_CONV - 1):l]


def _spread_kv(k_rows, v_rows):
    kf = k_rows.astype(F32)
    vf = v_rows.astype(F32)
    blk = lax.broadcasted_iota(jnp.int32, kf.shape, 1) // ATTN_HEAD_DIM
    kspread, vplaced = [], []
    for g in range(ATTN_KV_HEADS):
        tk = jnp.where(blk == g, kf, 0.0)
        uk = tk + pltpu.roll(tk, ATTN_HEAD_DIM, 1)
        kspread.append((uk + pltpu.roll(uk, 2 * ATTN_HEAD_DIM, 1)).astype(BF16))
        tv = jnp.where(blk == g, vf, 0.0)
        tv1 = pltpu.roll(tv, ATTN_HEAD_DIM, 1)
        shifted = (tv, tv1, pltpu.roll(tv, 2 * ATTN_HEAD_DIM, 1), pltpu.roll(tv1, 2 * ATTN_HEAD_DIM, 1))
        vplaced.append([shifted[(r - g) % ATTN_KV_HEADS].astype(BF16) for r in range(ATTN_GROUP)])
    return kspread, vplaced


def _attn_s_kernel(tq_rows, n_new, q_ref, k_ref, v_ref, kp_ref, vp_ref, sink_ref, bias_ref,
                   y_ref, kout_ref, vout_ref):
    T = tq_rows
    scale = ATTN_HEAD_DIM ** -0.5
    blocks = []
    n_sub = q_ref.shape[0] // SAMPLE_ROWS
    q_all, k_all, v_all = (r[...].astype(F32) for r in (q_ref, k_ref, v_ref))
    keep = WINDOW - n_new
    pad_to = lambda x, rows: jnp.concatenate([x, jnp.zeros((rows - x.shape[0], x.shape[1]), F32)], axis=0)
    for sb in range(n_sub):
        rows = slice(sb * SAMPLE_ROWS, (sb + 1) * SAMPLE_ROWS)
        ks_cur, vp_cur = _spread_kv(pad_to(k_all[rows], WINDOW).astype(BF16),
                                    pad_to(v_all[rows], WINDOW).astype(BF16))
        ks_prev, vp_prev = _spread_kv(kp_ref[0, sb].astype(BF16), vp_ref[0, sb].astype(BF16))
        blocks.append(((pad_to(q_all[rows], T) * scale).astype(BF16), ks_prev, ks_cur, vp_prev, vp_cur))
        for src, new, dst in ((kp_ref, k_all, kout_ref), (vp_ref, v_all, vout_ref)):
            dst[0, sb, pl.ds(0, keep), :] = src[0, sb, pl.ds(n_new, keep), :]
            dst[0, sb, pl.ds(keep, n_new), :] = new[sb * SAMPLE_ROWS:sb * SAMPLE_ROWS + n_new, :]

    q_blk = lax.broadcasted_iota(jnp.int32, (T, ATTN_KV), 1) // ATTN_HEAD_DIM
    pairs = [(i, g) for i in range(len(blocks)) for g in range(ATTN_KV_HEADS)]
    scores = []
    for i, g in pairs:
        q, ks_prev, ks_cur = blocks[i][0], blocks[i][1], blocks[i][2]
        qg = q[:, g * ATTN_KV:(g + 1) * ATTN_KV]
        q4 = jnp.concatenate([jnp.where(q_blk == r, qg, jnp.zeros_like(qg))
                              for r in range(ATTN_GROUP)], axis=0)
        keys = jnp.concatenate([ks_prev[g], ks_cur[g]], axis=0)
        scores.append(_dot_nt(q4, keys))
    bias = [bias_ref[0, g * ATTN_GROUP:(g + 1) * ATTN_GROUP].reshape(ATTN_GROUP * T, 2 * WINDOW)
            for g in range(ATTN_KV_HEADS)]
    sinks = [jnp.concatenate([jnp.broadcast_to(sink_ref[:, h:h + 1], (T, 1))
                              for h in range(g * ATTN_GROUP, (g + 1) * ATTN_GROUP)], axis=0)
             for g in range(ATTN_KV_HEADS)]
    n = range(len(pairs))
    s = [scores[j] - bias[pairs[j][1]] for j in n]
    m = [jnp.maximum(jnp.max(s[j], axis=-1, keepdims=True), sinks[pairs[j][1]]) for j in n]
    e = [jnp.exp(s[j] - m[j]) for j in n]
    den = [jnp.sum(e[j], axis=-1, keepdims=True) + jnp.exp(sinks[pairs[j][1]] - m[j]) for j in n]
    pn = [(e[j] * (1.0 / den[j])).astype(BF16) for j in n]
    probs = [jnp.concatenate([pn[j][r * T:(r + 1) * T] for r in range(ATTN_GROUP)], axis=1)
             for j in n]
    outs = []
    for j, (i, g) in enumerate(pairs):
        vp_prev, vp_cur = blocks[i][3], blocks[i][4]
        vals = jnp.concatenate([blk for r in range(ATTN_GROUP)
                                for blk in (vp_prev[g][r], vp_cur[g][r])], axis=0)
        outs.append(_dot(probs[j], vals))
    per_block = [jnp.concatenate(outs[i * ATTN_KV_HEADS:(i + 1) * ATTN_KV_HEADS], axis=1)
                 for i in range(len(blocks))]
    y_ref[...] = jnp.concatenate([y[0:SAMPLE_ROWS] for y in per_block], axis=0).astype(y_ref.dtype)


def _attn_t_kernel(q_ref, k_ref, v_ref, sink_ref, bias_ref, y_ref, kprev_scr, vtprev_scr):
    T = CHUNK
    c = pl.program_id(1)
    read_slot = (c + 1) % 2
    write_slot = c % 2

    @pl.when(c == 0)
    def _():
        kprev_scr[read_slot] = jnp.zeros(kprev_scr.shape[1:], BF16)
        vtprev_scr[read_slot] = jnp.zeros(vtprev_scr.shape[1:], BF16)

    scale = ATTN_HEAD_DIM ** -0.5
    q_t = (q_ref[...].astype(F32) * scale).T.astype(BF16)
    k_cur = k_ref[...]
    vt_cur = v_ref[...].astype(F32).T.astype(BF16)
    k_all = jnp.concatenate([kprev_scr[read_slot], k_cur], axis=0)
    vt_all = jnp.concatenate([vtprev_scr[read_slot], vt_cur], axis=1)

    groups = range(ATTN_KV_HEADS)
    zero_rows = jnp.zeros((ATTN_HEAD_DIM, ATTN_GROUP * T), BF16)
    scores = []
    for g in groups:
        heads = jnp.concatenate([q_t[(g * ATTN_GROUP + r) * ATTN_HEAD_DIM:(g * ATTN_GROUP + r + 1) * ATTN_HEAD_DIM]
                                 for r in range(ATTN_GROUP)], axis=1)
        q4_t = jnp.concatenate([heads if gg == g else zero_rows for gg in groups], axis=0)
        scores.append(_dot(k_all, q4_t))
    sinks = [jnp.concatenate([jnp.broadcast_to(sink_ref[:, h:h + 1], (1, T))
                              for h in range(g * ATTN_GROUP, (g + 1) * ATTN_GROUP)], axis=1)
             for g in groups]
    s = [scores[g] - bias_ref[0, g] for g in groups]
    m = [jnp.maximum(jnp.max(s[g], axis=0, keepdims=True), sinks[g]) for g in groups]
    e = [jnp.exp(s[g] - m[g]) for g in groups]
    den = [jnp.sum(e[g], axis=0, keepdims=True) + jnp.exp(sinks[g] - m[g]) for g in groups]
    p_t = [(e[g] * (1.0 / den[g])).astype(BF16) for g in groups]
    o_t = [_dot(vt_all[g * ATTN_HEAD_DIM:(g + 1) * ATTN_HEAD_DIM, :], p_t[g]) for g in groups]
    y_t = jnp.concatenate([o_t[g][:, r * T:(r + 1) * T] for g in groups for r in range(ATTN_GROUP)],
                          axis=0)
    y_ref[...] = y_t.T.astype(y_ref.dtype)
    kprev_scr[write_slot] = k_cur
    vtprev_scr[write_slot] = vt_cur


def _attn_t(group, y, p):
    nb, nc = group.n_seq, group.chunks_per_seq
    rows = y.shape[0]
    T = CHUNK
    table = _attn_bias_table(T).reshape(2, ATTN_KV_HEADS, ATTN_GROUP, T, 2 * WINDOW)
    table = table.transpose(0, 1, 4, 2, 3).reshape(2, ATTN_KV_HEADS, 2 * WINDOW, ATTN_GROUP * T)
    return pl.pallas_call(
        _attn_t_kernel,
        grid=(nb, nc),
        in_specs=[
            _chunk_spec(group, D_MODEL, Y_Q), _chunk_spec(group, ATTN_KV, Y_K), _chunk_spec(group, ATTN_KV, Y_V),
            _const_spec((1, LANES)),
            pl.BlockSpec((1, ATTN_KV_HEADS, 2 * WINDOW, ATTN_GROUP * T), lambda b, c: (jnp.minimum(c, 1), 0, 0, 0)),
        ],
        out_specs=_chunk_spec(group, D_MODEL, 0),
        out_shape=jax.ShapeDtypeStruct((rows, D_MODEL), BF16),
        scratch_shapes=[pltpu.VMEM((2, WINDOW, ATTN_KV), BF16), pltpu.VMEM((2, ATTN_KV, WINDOW), BF16)],
        compiler_params=_cparams("arbitrary", "arbitrary"),
        name="attn_p",
    )(y, y, y, p["attn_sinks"], table)


def _attn_bias_table(q_rows):
    t = jnp.arange(q_rows)[:, None]
    j = jnp.arange(2 * WINDOW)[None, :]
    dist = WINDOW + t - j
    band = (dist >= 0) & (dist <= WINDOW)
    slopes = 2.0 ** (-8.0 * jnp.arange(1, ATTN_HEADS + 1, dtype=F32) / ATTN_HEADS)
    bias = slopes[:, None, None] * dist.astype(F32)[None]
    first = jnp.where(band & (j >= WINDOW), bias, 1e30)
    later = jnp.where(band, bias, 1e30)
    return jnp.stack([first, later], axis=0)


def _attn_s(group, y, layer, k_cache, v_cache, p):
    rows = y.shape[0]
    tq = group.attn_q_rows
    n_sub = group.attn_seqs_per_step
    blk_rows = SAMPLE_ROWS * n_sub
    blk = lambda width, piece: pl.BlockSpec((blk_rows, width), lambda b: (b, piece))
    cache_spec = pl.BlockSpec((1, n_sub, WINDOW, ATTN_KV), lambda b: (layer, b, 0, 0))
    return pl.pallas_call(
        functools.partial(_attn_s_kernel, tq, group.last_chunk_valid),
        grid=(group.n_seq // n_sub,),
        in_specs=[
            blk(D_MODEL, Y_Q), blk(ATTN_KV, Y_K), blk(ATTN_KV, Y_V), cache_spec, cache_spec,
            _const_spec((1, LANES)),
            pl.BlockSpec((1, ATTN_HEADS, tq, 2 * WINDOW), lambda b: (1, 0, 0, 0)),
        ],
        out_specs=[blk(D_MODEL, 0), cache_spec, cache_spec],
        out_shape=[jax.ShapeDtypeStruct((rows, D_MODEL), BF16),
                   jax.ShapeDtypeStruct(k_cache.shape, F32), jax.ShapeDtypeStruct(v_cache.shape, F32)],
        input_output_aliases={3: 1, 4: 2},
        compiler_params=_cparams("arbitrary"),
        name="attn_s",
    )(y, y, y, k_cache, v_cache, p["attn_sinks"], _attn_bias_table(tq))


def _gelu_tanh(x):
    return 0.5 * x * (1.0 + jnp.tanh(math.sqrt(2.0 / math.pi) * (x + 0.044715 * (x * x * x))))


def _gmlp_kernel(u_ref, v_ref, lg_ref, lb_ref, ws_ref, bs_ref, y_ref):
    T = CHUNK
    n_chunks = u_ref.shape[0] // T
    u = _gelu_tanh(u_ref[...].astype(F32))
    v = _gelu_tanh(v_ref[...].astype(F32))
    vc = v - jnp.mean(v, axis=-1, keepdims=True)
    v = vc * lax.rsqrt(jnp.mean(vc * vc, axis=-1, keepdims=True) + EPS) * lg_ref[...] + lb_ref[...]
    row = lax.broadcasted_iota(jnp.int32, (T, T), 0)
    col = lax.broadcasted_iota(jnp.int32, (T, T), 1)
    causal = row >= col
    vb = v.astype(BF16)
    mixed = [[None] * GM_GROUPS for _ in range(n_chunks)]
    for g in range(GM_GROUPS):
        w = jnp.where(causal, ws_ref[g], 0.0).astype(BF16)
        gs = slice(g * GM_GROUP_WIDTH, (g + 1) * GM_GROUP_WIDTH)
        v_g = jnp.concatenate([vb[c * T:(c + 1) * T, gs] for c in range(n_chunks)], axis=1)
        m_g = _dot(w, v_g)
        for c in range(n_chunks):
            mixed[c][g] = m_g[:, c * GM_GROUP_WIDTH:(c + 1) * GM_GROUP_WIDTH] + bs_ref[:, gs]
    mixed = jnp.concatenate([jnp.concatenate(m, axis=1) for m in mixed], axis=0)
    y_ref[...] = (u * mixed).astype(y_ref.dtype)


def _gmlp(group, y, p):
    tm = group.tm_rowwise
    rows = y.shape[0]
    tok = lambda piece: pl.BlockSpec((tm, D_MODEL), lambda i: (i, piece))
    yd = pl.pallas_call(
        _gmlp_kernel,
        grid=(rows // tm,),
        in_specs=[
            tok(Y_U), tok(Y_VGM),
            _const_spec((1, D_MODEL)),
            _const_spec((1, D_MODEL)),
            _const_spec((GM_GROUPS, CHUNK, CHUNK)),
            _const_spec((CHUNK, D_MODEL)),
        ],
        out_specs=tok(0),
        out_shape=jax.ShapeDtypeStruct((rows, D_MODEL), BF16),
        compiler_params=_cparams("arbitrary"),
        name="gmlp",
    )(y, y, p["gm_ln_g"], p["gm_ln_b"], p["gm_w_s"], p["gm_b_exp"])
    return yd, None


def _gmlp_s_kernel(n_diag, u_ref, v_ref, lg_ref, lb_ref, wd_ref, bs_ref, y_ref, v_out):
    u = _gelu_tanh(u_ref[...].astype(F32))
    v = _gelu_tanh(v_ref[...].astype(F32))
    vc = v - jnp.mean(v, axis=-1, keepdims=True)
    v = vc * lax.rsqrt(jnp.mean(vc * vc, axis=-1, keepdims=True) + EPS) * lg_ref[...] + lb_ref[...]
    mixed = bs_ref[...] + wd_ref[0] * v
    for d in range(1, n_diag):
        mixed = mixed + wd_ref[d] * pltpu.roll(v, d, 0)
    y_ref[...] = (u * mixed).astype(y_ref.dtype)
    v_out[...] = v


def _gmlp_s_tables(group, w_s, b_s):
    l = group.last_chunk_valid
    reps = group.tm_rowwise // SAMPLE_ROWS
    t = jnp.arange(SAMPLE_ROWS)
    expand = lambda per_group: jnp.tile(jnp.repeat(per_group.T, GM_GROUP_WIDTH, axis=1), (reps, 1))
    diags = []
    for d in range(l):
        src = t - d
        w = w_s[:, t, jnp.clip(src, 0, None)]
        diags.append(expand(jnp.where((src >= 0) & (t < l), w, 0.0)))
    return jnp.stack(diags, axis=0), expand(b_s[:, :SAMPLE_ROWS])


def _gmlp_s(group, y, p):
    rows = y.shape[0]
    tm = group.tm_rowwise
    l = group.last_chunk_valid
    tok = lambda piece: pl.BlockSpec((tm, D_MODEL), lambda i: (i, piece))
    wd, bs = _gmlp_s_tables(group, p["gm_w_s"], p["gm_b_s"])
    return pl.pallas_call(
        functools.partial(_gmlp_s_kernel, l),
        grid=(rows // tm,),
        in_specs=[tok(Y_U), tok(Y_VGM), _const_spec((1, D_MODEL)), _const_spec((1, D_MODEL)),
                  _const_spec((l, tm, D_MODEL)), _const_spec((tm, D_MODEL))],
        out_specs=[tok(0), tok(0)],
        out_shape=[jax.ShapeDtypeStruct((rows, D_MODEL), BF16), jax.ShapeDtypeStruct((rows, D_MODEL), F32)],
        compiler_params=_cparams("arbitrary"),
        name="gmlp_s",
    )(y, y, p["gm_ln_g"], p["gm_ln_b"], wd, bs)


def _merge_kernel(x_ref, ga_ref, ya_ref, yb_ref, yc_ref, yd_ref, g0_ref, g1_ref, g2_ref, g3_ref,
                  wb_ref, wo_ref, o_ref):
    merged = None
    for i, (y_ref, g_ref) in enumerate(((ya_ref, g0_ref), (yb_ref, g1_ref), (yc_ref, g2_ref),
                                        (yd_ref, g3_ref))):
        gate = jax.nn.sigmoid(g_ref[...].astype(F32))
        term = gate * _dot(y_ref[...], wb_ref[i])
        merged = term if merged is None else merged + term
    o = _dot(merged.astype(BF16), wo_ref[...])
    o_ref[...] = x_ref[...] + _mod_rows(ga_ref) * o


def _merge(group, x, mod, branches, y, p):
    tm = group.tm_merge
    rows = x.shape[0]
    tok = lambda piece: pl.BlockSpec((tm, D_MODEL), lambda i: (i, piece))
    return pl.pallas_call(
        _merge_kernel,
        grid=(rows // tm,),
        in_specs=[tok(0), _mod_spec(group, tm, 2)] + [tok(0)] * 4 + [tok(Y_G0 + i) for i in range(4)] + [
            pl.BlockSpec((N_BRANCH, D_MODEL, D_MODEL), lambda i: (0, 0, 0), pipeline_mode=pl.Buffered(1)),
            pl.BlockSpec((D_MODEL, D_MODEL), lambda i: (0, 0), pipeline_mode=pl.Buffered(1)),
        ],
        out_specs=tok(0),
        out_shape=jax.ShapeDtypeStruct((rows, D_MODEL), F32),
        compiler_params=_cparams("arbitrary"),
        name="merge",
    )(x, mod, *branches, y, y, y, y, p["w_branch"], p["w_o"])


def _ffn_kernel(seg8, l_valid, tiles_per_seq, final_norm,
                x_ref, g_ref, sc_ref, sh_ref, ga_ref, cp_ref, wa_ref, wg_ref, cw_ref, cb_ref, wd_ref,
                gf_ref, o_ref, cst_ref, carry_scr, xp_scr, act_scr):
    tm = x_ref.shape[0]
    i = pl.program_id(0)
    x = x_ref[...]
    h = _modnorm(x, g_ref[...], _mod_rows(sc_ref), _mod_rows(sh_ref)).astype(BF16)
    first = CARRY_ROWS - (FFN_CONV - 1)

    if not seg8:
        @pl.when(i % tiles_per_seq == 0)
        def _():
            carry_scr[...] = cp_ref[0]

    def conv_piece(up, half, cs, xp):
        off = half * D_FF
        col = slice(off + cs.start, off + cs.stop)
        if seg8:
            tb = tm // SAMPLE_ROWS
            xp[:, 0:CARRY_ROWS, :] = cp_ref[:, :, col]
            xp[:, CARRY_ROWS:CARRY_ROWS + SAMPLE_ROWS, :] = up.reshape(tb, SAMPLE_ROWS, MXU_WIDTH)
            acc = None
            for j in range(FFN_CONV):
                term = cw_ref[j:j + 1, col] * xp[:, pl.ds(first + j, SAMPLE_ROWS), :]
                acc = term if acc is None else acc + term
            cst_ref[:, :, col] = xp[:, pl.ds(CARRY_ROWS + l_valid - (FFN_CONV - 1), FFN_CONV - 1), :]
            return acc.reshape(tm, MXU_WIDTH) + cb_ref[:, col]
        xp[0:CARRY_ROWS, :] = carry_scr[:, col]
        xp[CARRY_ROWS:CARRY_ROWS + tm, :] = up
        acc = None
        for j in range(FFN_CONV):
            term = cw_ref[j:j + 1, col] * xp[pl.ds(first + j, tm), :]
            acc = term if acc is None else acc + term
        carry_scr[:, col] = xp[tm:tm + CARRY_ROWS, :]
        return acc + cb_ref[:, col]

    n_pieces = D_FF // MXU_WIDTH
    pieces = [slice(k * MXU_WIDTH, (k + 1) * MXU_WIDTH) for k in range(n_pieces)]
    ups = (_dot(h, wa_ref[:, pieces[0]]), _dot(h, wg_ref[:, pieces[0]]))
    for k in range(n_pieces):
        up_a, up_g = ups
        if k + 1 < n_pieces:
            ups = (_dot(h, wa_ref[:, pieces[k + 1]]), _dot(h, wg_ref[:, pieces[k + 1]]))
        a = conv_piece(up_a, 0, pieces[k], xp_scr.at[2 * (k % 2)])
        g = conv_piece(up_g, 1, pieces[k], xp_scr.at[2 * (k % 2) + 1])
        act_scr[:, pieces[k]] = (_silu(a) * g).astype(BF16)

    if not seg8:
        @pl.when(i % tiles_per_seq == tiles_per_seq - 1)
        def _():
            cst_ref[0] = carry_scr[CARRY_ROWS - (FFN_CONV - 1):CARRY_ROWS, :]

    out = x + _mod_rows(ga_ref) * _dot(act_scr[...], wd_ref[...])
    if final_norm:
        out = out * lax.rsqrt(jnp.mean(out * out, axis=-1, keepdims=True) + EPS) * gf_ref[...]
    o_ref[...] = out


def _ffn(group, x, mod, conv_prev8, p, g_final, final_norm):
    tm = group.tm_ffn
    rows = x.shape[0]
    seg8 = not group.is_prompt
    tiles_per_seq = 1 if seg8 else group.seq_rows // tm
    single = dict(pipeline_mode=pl.Buffered(1))
    if seg8:
        tb = tm // SAMPLE_ROWS
        cp_spec = pl.BlockSpec((tb, CARRY_ROWS, 2 * D_FF), lambda i: (i, 0, 0))
        cst_spec = pl.BlockSpec((tb, FFN_CONV - 1, 2 * D_FF), lambda i: (i, 0, 0))
        xp_shape = (tb, CARRY_ROWS + SAMPLE_ROWS, MXU_WIDTH)
    else:
        cp_spec = pl.BlockSpec((1, CARRY_ROWS, 2 * D_FF), lambda i: (i // tiles_per_seq, 0, 0))
        cst_spec = pl.BlockSpec((1, FFN_CONV - 1, 2 * D_FF), lambda i: (i // tiles_per_seq, 0, 0))
        xp_shape = (CARRY_ROWS + tm, MXU_WIDTH)
    kern = functools.partial(_ffn_kernel, seg8, group.last_chunk_valid, tiles_per_seq, final_norm)
    return pl.pallas_call(
        kern,
        grid=(rows // tm,),
        in_specs=[
            pl.BlockSpec((tm, D_MODEL), lambda i: (i, 0)),
            _const_spec((1, D_MODEL)),
            _mod_spec(group, tm, 4),
            _mod_spec(group, tm, 3),
            _mod_spec(group, tm, 5),
            cp_spec,
            pl.BlockSpec((D_MODEL, D_FF), lambda i: (0, 0), **single),
            pl.BlockSpec((D_MODEL, D_FF), lambda i: (0, 0), **single),
            _const_spec((FFN_CONV, 2 * D_FF)),
            _const_spec((1, 2 * D_FF)),
            pl.BlockSpec((D_FF, D_MODEL), lambda i: (0, 0), **single),
            _const_spec((1, D_MODEL)),
        ],
        out_specs=[pl.BlockSpec((tm, D_MODEL), lambda i: (i, 0)), cst_spec],
        out_shape=[
            jax.ShapeDtypeStruct((rows, D_MODEL), F32),
            jax.ShapeDtypeStruct((group.n_seq, FFN_CONV - 1, 2 * D_FF), F32),
        ],
        scratch_shapes=[
            pltpu.VMEM((CARRY_ROWS, 2 * D_FF), F32),
            pltpu.VMEM((4,) + xp_shape, F32),
            pltpu.VMEM((tm, D_FF), BF16),
        ],
        compiler_params=_cparams("arbitrary"),
        name="ffn",
    )(x, p["g_norm_ffn"], mod, mod, mod, conv_prev8, p["w_up_a"], p["w_up_g"], p["ffn_conv_w"],
      p["ffn_conv_b"], p["w_down"], g_final)


class _Group:
    def __init__(self, is_prompt, n_seq, seq_len):
        self.is_prompt = is_prompt
        self.n_seq = n_seq
        if is_prompt:
            self.seq_rows = seq_len
            self.chunk_rows = CHUNK
            self.chunks_per_seq = seq_len // CHUNK
            self.last_chunk_valid = CHUNK
            self.per_token_mod = False
            self.tm_inproj = 1024
            self.tm_merge = 512
            self.tm_ffn = 512
            self.tm_rowwise = 512
        else:
            self.attn_q_rows = BF16_SUBLANES
            self.tm_rowwise = 256
            self.attn_seqs_per_step = 8
            self.seq_rows = SAMPLE_ROWS
            self.chunk_rows = SAMPLE_ROWS
            self.chunks_per_seq = 1
            self.last_chunk_valid = seq_len
            self.per_token_mod = True
            self.tm_inproj = 512
            self.tm_merge = 512
            self.tm_ffn = 128


def _pad_front(state, rows):
    return jnp.pad(state, ((0, 0), (rows - state.shape[1], 0), (0, 0)))


def _layer_params(i, w):
    pad_heads = lambda v: jnp.pad(v, (0, LANES - SSD_HEADS)).reshape(1, LANES)
    w_in = w["w_in"][i]
    head_of_channel = jnp.arange(SSD_D_INNER) // SSD_HEAD_DIM
    expand = (jnp.arange(LANES)[:, None] == head_of_channel[None, :]).astype(BF16)
    return {
        "g_norm_mix": w["g_norm_mix"][i].reshape(1, D_MODEL),
        "w_main": jnp.concatenate([w_in[:, a:b] for a, b in _W_IN_PIECES], axis=1).astype(BF16),
        "w_dt": jnp.pad(w_in[:, _IN_OFF_DT:_IN_OFF_DT + SSD_HEADS],
                        ((0, 0), (0, LANES - SSD_HEADS))).astype(BF16),
        "ssd_conv_w": w["ssd_conv_w"][i],
        "ssd_conv_b": w["ssd_conv_b"][i].reshape(1, SSD_CONV_DIM),
        "ssd_dt_bias": pad_heads(w["ssd_dt_bias"][i]),
        "ssd_a_log": pad_heads(w["ssd_a_log"][i]),
        "ssd_d": jnp.repeat(w["ssd_d"][i], SSD_HEAD_DIM).reshape(1, SSD_D_INNER),
        "ssd_norm_g": w["ssd_norm_g"][i].reshape(1, SSD_D_INNER),
        "ssd_expand": expand,
        "sc_conv_w": w["sc_conv_w"][i],
        "attn_sinks": jnp.pad(w["attn_sinks"][i], (0, LANES - ATTN_HEADS)).reshape(1, LANES),
        "gm_ln_g": w["gm_ln_g"][i].reshape(1, D_MODEL),
        "gm_ln_b": w["gm_ln_b"][i].reshape(1, D_MODEL),
        "gm_w_s": w["gm_w_s"][i],
        "gm_b_s": w["gm_b_s"][i],
        "gm_b_exp": jnp.repeat(w["gm_b_s"][i].T, GM_GROUP_WIDTH, axis=1),
        "w_branch": w["w_branch"][i].astype(BF16),
        "w_o": w["w_o"][i].astype(BF16),
        "g_norm_ffn": w["g_norm_ffn"][i].reshape(1, D_MODEL),
        "w_up_a": w["ffn_w_up"][i][:, :D_FF].astype(BF16),
        "w_up_g": w["ffn_w_up"][i][:, D_FF:].astype(BF16),
        "ffn_conv_w": w["ffn_conv_w"][i],
        "ffn_conv_b": w["ffn_conv_b"][i].reshape(1, 2 * D_FF),
        "w_down": w["ffn_w_down"][i].astype(BF16),
    }


def _run_group(group, x, mod_all, states, params, g_final):
    n = group.n_seq
    outs = []
    if not group.is_prompt:
        ssm_all = states[0]
        k_all = states[3].reshape(DEPTH, n, WINDOW, ATTN_KV)
        v_all = states[4].reshape(DEPTH, n, WINDOW, ATTN_KV)
    for i in range(DEPTH):
        p = params[i]
        mod = mod_all[i]
        y, dtr = _inproj(group, x, mod, p["g_norm_mix"], p["w_main"], p["w_dt"])
        if group.is_prompt:
            ssm0 = jnp.zeros((n, SSD_HEADS, SSD_HEAD_DIM, SSD_STATE), F32)
            ssdc0 = jnp.zeros((n, CARRY_ROWS, SSD_CONV_DIM), F32)
            scc0 = jnp.zeros((n, CARRY_ROWS, D_MODEL), F32)
            ffc0 = jnp.zeros((n, FFN_CONV - 1, 2 * D_FF), F32)
            ya, ssm1, ssdc1 = _ssd(group, y, dtr, ssm0, ssdc0, p)
            yb, scc1 = _sconv(group, y, scc0, p)
            yd, v_rows = _gmlp(group, y, p)
            yc = _attn_t(group, y, p)
        else:
            ffc0 = states[5][i]
            ya, ssm_all, ssdc1 = _ssd_s(group, y, dtr, i, ssm_all, states[1][i], p)
            yb, scc1 = _sconv_s(group, y, states[2][i], p)
            yd, v_rows = _gmlp_s(group, y, p)
            yc, k_all, v_all = _attn_s(group, y, i, k_all, v_all, p)
        x = _merge(group, x, mod, (ya, yb, yc, yd), y, p)
        x, ffc1 = _ffn(group, x, mod, _pad_front(ffc0, CARRY_ROWS), p, g_final, i == DEPTH - 1)

        kcol = slice(Y_K * SSD_BC, (Y_K + 1) * SSD_BC)
        vcol = slice(Y_V * SSD_BC, (Y_V + 1) * SSD_BC)
        if group.is_prompt:
            y3 = y.reshape(n, group.seq_rows, Y_COLS)
            k1 = y3[:, group.seq_rows - WINDOW:, kcol].astype(F32)
            v1 = y3[:, group.seq_rows - WINDOW:, vcol].astype(F32)
            k1 = k1.reshape(n, WINDOW, ATTN_KV_HEADS, ATTN_HEAD_DIM)
            v1 = v1.reshape(n, WINDOW, ATTN_KV_HEADS, ATTN_HEAD_DIM)
            outs.append((ssm1, ssdc1, scc1, k1, v1, ffc1))
        else:
            gv = v_rows.reshape(n, SAMPLE_ROWS, D_MODEL)[:, :group.last_chunk_valid]
            outs.append((ssdc1, scc1, ffc1, gv))
    stacked = tuple(jnp.stack([o[j] for o in outs], axis=0) for j in range(len(outs[0])))
    if not group.is_prompt:
        ssdc, scc, ffc, gv = stacked
        cache_shape = (DEPTH, n, WINDOW, ATTN_KV_HEADS, ATTN_HEAD_DIM)
        stacked = (ssm_all, ssdc, scc, k_all.reshape(cache_shape), v_all.reshape(cache_shape), ffc, gv)
    return x, stacked


def kernel(x_prompt, x_sample, c_prompt, c_sample, state_ssm, state_ssd_conv, state_sc_conv, cache_k,
           cache_v, state_ffn_conv, w_ada, b_ada, g_norm_mix, w_in, ssd_conv_w, ssd_conv_b, ssd_dt_bias,
           ssd_a_log, ssd_d, ssd_norm_g, sc_conv_w, attn_sinks, gm_ln_g, gm_ln_b, gm_w_s, gm_b_s,
           w_branch, w_o, g_norm_ffn, ffn_w_up, ffn_conv_w, ffn_conv_b, ffn_w_down, g_final):
    weights = dict(g_norm_mix=g_norm_mix, w_in=w_in, ssd_conv_w=ssd_conv_w, ssd_conv_b=ssd_conv_b,
                   ssd_dt_bias=ssd_dt_bias, ssd_a_log=ssd_a_log, ssd_d=ssd_d, ssd_norm_g=ssd_norm_g,
                   sc_conv_w=sc_conv_w, attn_sinks=attn_sinks, gm_ln_g=gm_ln_g, gm_ln_b=gm_ln_b,
                   gm_w_s=gm_w_s, gm_b_s=gm_b_s, w_branch=w_branch, w_o=w_o, g_norm_ffn=g_norm_ffn,
                   ffn_w_up=ffn_w_up, ffn_conv_w=ffn_conv_w, ffn_conv_b=ffn_conv_b, ffn_w_down=ffn_w_down)
    params = [_layer_params(i, weights) for i in range(DEPTH)]
    gf = g_final.reshape(1, D_MODEL)

    nb_p, len_p, _ = x_prompt.shape
    nb_s, len_s, _ = x_sample.shape
    assert len_p % CHUNK == 0 and len_s <= SAMPLE_ROWS and len_s >= SSD_CONV - 1
    prompt = _Group(True, nb_p, len_p)
    sample = _Group(False, nb_s, len_s)

    c_rows = nb_p + nb_s
    c_pad = -c_rows % SAMPLE_ROWS
    c_all = jnp.pad(jnp.concatenate([c_prompt, c_sample], axis=0), ((0, c_pad), (0, 0)))
    mod = _ada(c_all, w_ada, b_ada)
    mod_p = mod[:, :nb_p].reshape(DEPTH, nb_p, 1, 6 * D_MODEL)
    mod_s = jnp.repeat(mod[:, nb_p:nb_p + nb_s], SAMPLE_ROWS, axis=1)

    xp = x_prompt.reshape(nb_p * len_p, D_MODEL)
    xs = jnp.pad(x_sample, ((0, 0), (0, SAMPLE_ROWS - len_s), (0, 0))).reshape(nb_s * SAMPLE_ROWS, D_MODEL)

    y_p, st_p = _run_group(prompt, xp, mod_p, (None,) * 6, params, gf)
    y_s, st_s = _run_group(sample, xs, mod_s,
                           (state_ssm, state_ssd_conv, state_sc_conv, cache_k, cache_v, state_ffn_conv),
                           params, gf)
    y_prompt = y_p.reshape(nb_p, len_p, D_MODEL)
    y_sample = y_s.reshape(nb_s, SAMPLE_ROWS, D_MODEL)[:, :len_s]
    return (y_prompt, y_sample) + st_p + st_s
```
